```python
import math
import jax, jax.numpy as jnp
from jax import lax
import numpy as np

D_MODEL = 1024
BATCH = 16
SEQ = 4096
DEPTH = 2
DEC_BATCH = 4
DEC_SEQ = 8192
PAST_LEN = 128

GRID_W = 64
HEAD_DIM = 64
ATT_Q_HEADS = 4
ATT_KV_HEADS = 2
ATT_GQ = ATT_Q_HEADS // ATT_KV_HEADS
ATT_WIDTH = ATT_Q_HEADS * HEAD_DIM
ATT_KV_WIDTH = ATT_KV_HEADS * HEAD_DIM
Q_BLOCK = 128
ROPE_THETA = 10000.0
FNET_HEADS = 4
FNET_WIDTH = FNET_HEADS * HEAD_DIM
POOL_WINDOWS = (2, 4, 8, 16)
POOL_GROUPS = 4
POOL_GROUP_DIM = 64
POOL_WIDTH = POOL_GROUPS * POOL_GROUP_DIM
SSD_HEADS = 4
SSD_HEAD_DIM = 64
SSD_WIDTH = SSD_HEADS * SSD_HEAD_DIM
SSD_GROUPS = 2
SSD_STATE = 64
SSD_CONV = 4
SSD_CHUNK = 128
SSD_XBC = SSD_WIDTH + 2 * SSD_GROUPS * SSD_STATE
SSD_DT = 2 * SSD_HEADS
MIX_WIDTH = ATT_WIDTH + FNET_WIDTH + POOL_WIDTH + SSD_WIDTH
IN_SPLITS = (ATT_WIDTH, ATT_KV_WIDTH, ATT_KV_WIDTH, FNET_WIDTH, POOL_WIDTH, SSD_WIDTH, SSD_XBC, SSD_DT)
N_IN = sum(IN_SPLITS)
N_GROUPS = 4
EXPERTS_PER_GROUP = 4
N_EXPERTS = N_GROUPS * EXPERTS_PER_GROUP
EXPERT_TOPK = 2
EXPERT_FF = 256
DEEPNORM_ALPHA = (2 * DEPTH) ** 0.25
DEEPNORM_BETA = (8 * DEPTH) ** -0.25
LN_EPS = 1e-5
RMS_EPS = 1e-6

kernel_name = 'hybrid_parallel_bidir_encoder'


def layer_norm(x, g, b):
    xf = x.astype(jnp.float32)
    mu = jnp.mean(xf, axis=-1, keepdims=True)
    xc = xf - mu
    var = jnp.mean(xc * xc, axis=-1, keepdims=True)
    return (xc * lax.rsqrt(var + LN_EPS) * g.astype(jnp.float32) + b.astype(jnp.float32)).astype(x.dtype)


def rms_norm(x, g):
    xf = x.astype(jnp.float32)
    return (xf * lax.rsqrt(jnp.mean(xf * xf, axis=-1, keepdims=True) + RMS_EPS) * g.astype(jnp.float32)).astype(x.dtype)


def axial_rope_tables(seq_len):
    rows = seq_len // GRID_W
    row = jnp.repeat(jnp.arange(rows, dtype=jnp.float32), GRID_W)
    col = jnp.tile(jnp.arange(GRID_W, dtype=jnp.float32), rows)
    half = HEAD_DIM // 2
    freqs = 1.0 / (ROPE_THETA ** (jnp.arange(0, half, 2, dtype=jnp.float32) / half))
    ang = jnp.concatenate([row[:, None] * freqs, col[:, None] * freqs], axis=-1)
    return jnp.cos(ang), jnp.sin(ang)


def apply_axial_rope(x, cos, sin):
    f = HEAD_DIM // 4
    xf = x.astype(jnp.float32)
    c = cos[:, None, :]
    s = sin[:, None, :]
    outs = []
    for a in range(2):
        seg = xf[..., a * 2 * f:(a + 1) * 2 * f]
        x1, x2 = seg[..., :f], seg[..., f:]
        ca, sa = c[..., a * f:(a + 1) * f], s[..., a * f:(a + 1) * f]
        outs += [x1 * ca - x2 * sa, x2 * ca + x1 * sa]
    return jnp.concatenate(outs, axis=-1).astype(x.dtype)


def axial_gqa_attention(q, k, v, q_g, k_g, cos, sin):
    b, s = q.shape[:2]
    q = apply_axial_rope(rms_norm(q, q_g), cos, sin)
    k = apply_axial_rope(rms_norm(k, k_g), cos, sin)
    nblk = s // Q_BLOCK
    qb = q.reshape(b, nblk, Q_BLOCK, ATT_KV_HEADS, ATT_GQ, HEAD_DIM).transpose(1, 0, 3, 4, 2, 5)
    kt = k.transpose(0, 2, 1, 3)
    vt = v.transpose(0, 2, 1, 3)
    scale = HEAD_DIM ** -0.5

    def block(qi):
        sc = jnp.einsum('bkgqd,bksd->bkgqs', qi, kt, preferred_element_type=jnp.float32) * scale
        p = jax.nn.softmax(sc, axis=-1)
        return jnp.einsum('bkgqs,bksd->bkgqd', p.astype(vt.dtype), vt)

    o = lax.map(block, qb)
    return o.transpose(1, 0, 4, 2, 3, 5).reshape(b, s, ATT_WIDTH)


def fourier_mix(u, w_fnet):
    b, s, _ = u.shape
    uf = u.astype(jnp.float32).reshape(b, s, FNET_HEADS, HEAD_DIM)
    mixed = jnp.fft.fft2(uf, axes=(1, 3), norm='ortho').real
    return mixed.reshape(b, s, FNET_WIDTH).astype(u.dtype) @ w_fnet


def multiscale_pool(u, w_pool, pool_scale):
    b, s, _ = u.shape
    uf = u.astype(jnp.float32).reshape(b, s, POOL_GROUPS, POOL_GROUP_DIM)
    cs = jnp.concatenate([jnp.zeros_like(uf[:, :1]), jnp.cumsum(uf, axis=1)], axis=1)
    t = jnp.arange(s)
    outs = []
    for gi, w in enumerate(POOL_WINDOWS):
        lo = jnp.clip(t - w // 2, 0, s - 1)
        hi = jnp.clip(t + (w - w // 2) - 1, 0, s - 1)
        csg = cs[:, :, gi]
        win_sum = jnp.take(csg, hi + 1, axis=1) - jnp.take(csg, lo, axis=1)
        cnt = (hi - lo + 1).astype(jnp.float32)[None, :, None]
        outs.append(win_sum / cnt - uf[:, :, gi])
    pooled = jnp.stack(outs, axis=2).astype(u.dtype)
    mixed = jnp.einsum('bsgc,gcd->bsgd', pooled, w_pool).reshape(b, s, POOL_WIDTH)
    return mixed * pool_scale


def ssd_scan(x, dt, a, bm, cm):
    b, s, h, p = x.shape
    n = bm.shape[-1]
    l = SSD_CHUNK
    nc = s // l
    xd = (x * dt[..., None]).reshape(b, nc, l, h, p)
    a_dt = (a * dt).reshape(b, nc, l, h).transpose(0, 3, 1, 2)
    bc = bm.reshape(b, nc, l, h, n)
    cc = cm.reshape(b, nc, l, h, n)
    a_cs = jnp.cumsum(a_dt, axis=-1)
    lower = jnp.tril(jnp.ones((l, l), dtype=bool))
    decay = jnp.exp(jnp.where(lower, a_cs[..., :, None] - a_cs[..., None, :], -jnp.inf))
    scores = jnp.einsum('bclhn,bcshn->bhcls', cc, bc) * decay
    y_diag = jnp.einsum('bhcls,bcshp->bclhp', scores, xd)
    decay_states = jnp.exp(a_cs[..., -1:] - a_cs)
    states = jnp.einsum('bclhn,bhcl,bclhp->bchpn', bc, decay_states, xd)
    chunk_decay = jnp.exp(a_cs[..., -1])

    def step(carry, inp):
        st, dec = inp
        return carry * dec[..., None, None] + st, carry

    init = jnp.zeros((b, h, p, n), jnp.float32)
    _, prev = lax.scan(step, init, (states.transpose(1, 0, 2, 3, 4), chunk_decay.transpose(2, 0, 1)))
    prev = prev.transpose(1, 0, 2, 3, 4)
    y_off = jnp.einsum('bclhn,bchpn,bhcl->bclhp', cc, prev, jnp.exp(a_cs))
    return (y_diag + y_off).reshape(b, s, h, p)


def bidir_ssd(z, xbc, dt_raw, conv_w, conv_b, dt_bias, a_log, d_skip, norm_g):
    b, s, _ = xbc.shape
    pad_l = SSD_CONV // 2
    pad_r = SSD_CONV - 1 - pad_l
    xp = jnp.pad(xbc, ((0, 0), (pad_l, pad_r), (0, 0)))
    conv = conv_b
    for k in range(SSD_CONV):
        conv = conv + xp[:, k:k + s] * conv_w[k]
    xbc = jax.nn.silu(conv).astype(jnp.float32)
    nb = SSD_GROUPS * SSD_STATE
    xs = xbc[..., :SSD_WIDTH].reshape(b, s, SSD_HEADS, SSD_HEAD_DIM)
    hpg = SSD_HEADS // SSD_GROUPS
    bm = jnp.repeat(xbc[..., SSD_WIDTH:SSD_WIDTH + nb].reshape(b, s, SSD_GROUPS, SSD_STATE), hpg, axis=2)
    cm = jnp.repeat(xbc[..., SSD_WIDTH + nb:].reshape(b, s, SSD_GROUPS, SSD_STATE), hpg, axis=2)
    dt = jax.nn.softplus(dt_raw.astype(jnp.float32).reshape(b, s, 2, SSD_HEADS) + dt_bias.astype(jnp.float32))
    a = -jnp.exp(a_log.astype(jnp.float32))
    y_f = ssd_scan(xs, dt[:, :, 0], a[0], bm, cm)
    fl = lambda t: jnp.flip(t, axis=1)
    y_b = fl(ssd_scan(fl(xs), fl(dt[:, :, 1]), a[1], fl(bm), fl(cm)))
    y = y_f + y_b + xs * d_skip.astype(jnp.float32)[:, None]
    y = y.reshape(b, s, SSD_WIDTH) * jax.nn.silu(z.astype(jnp.float32))
    y = rms_norm(y.reshape(b, s, SSD_GROUPS, SSD_WIDTH // SSD_GROUPS), norm_g.reshape(SSD_GROUPS, -1))
    return y.reshape(b, s, SSD_WIDTH).astype(z.dtype)


def hier_moe(x, w_group, b_group, w_router, b_router, w_gate, w_up, w_down):
    b, s, d = x.shape
    xt = x.reshape(-1, d)
    g_prob = jax.nn.softmax((xt @ w_group).astype(jnp.float32) + b_group.astype(jnp.float32), axis=-1)
    g_p, g_idx = lax.top_k(g_prob, 1)
    e_logits = ((xt @ w_router).astype(jnp.float32) + b_router.astype(jnp.float32)).reshape(-1, N_GROUPS, EXPERTS_PER_GROUP)
    e_in = jnp.take_along_axis(e_logits, g_idx[:, :, None], axis=1)[:, 0]
    e_top, e_idx = lax.top_k(e_in, EXPERT_TOPK)
    e_w = jax.nn.softmax(e_top, axis=-1)
    within = jnp.sum(jax.nn.one_hot(e_idx, EXPERTS_PER_GROUP, dtype=jnp.float32) * e_w[..., None], axis=1)
    gates = (jax.nn.one_hot(g_idx[:, 0], N_GROUPS, dtype=jnp.float32) * g_p)[:, :, None] * within[:, None, :]
    gates = gates.reshape(-1, N_EXPERTS).astype(x.dtype)
    y = jnp.zeros_like(xt)
    for e in range(N_EXPERTS):
        h = jax.nn.silu(xt @ w_gate[e]) * (xt @ w_up[e])
        y = y + gates[:, e:e + 1] * (h @ w_down[e])
    return y.reshape(b, s, d)


def encoder_layer(x, cos, sin, w_in, q_norm_g, k_norm_g, w_fnet, w_pool, pool_scale, conv_w, conv_b,
                  dt_bias, a_log, d_skip, ssd_norm_g, w_out, ln1_g, ln1_b, w_group, b_group, w_router,
                  b_router, w_gate, w_up, w_down, ln2_g, ln2_b):
    b, s, _ = x.shape
    h = x @ w_in
    idx = np.cumsum(IN_SPLITS)[:-1].tolist()
    q, k, v, u_f, u_p, z, xbc, dt_raw = jnp.split(h, idx, axis=-1)
    att = axial_gqa_attention(q.reshape(b, s, ATT_Q_HEADS, HEAD_DIM),
                              k.reshape(b, s, ATT_KV_HEADS, HEAD_DIM),
                              v.reshape(b, s, ATT_KV_HEADS, HEAD_DIM), q_norm_g, k_norm_g, cos, sin)
    fno = fourier_mix(u_f, w_fnet)
    pol = multiscale_pool(u_p, w_pool, pool_scale)
    ssd = bidir_ssd(z, xbc, dt_raw, conv_w, conv_b, dt_bias, a_log, d_skip, ssd_norm_g)
    mix = jnp.concatenate([att, fno, pol, ssd], axis=-1) @ w_out
    x = layer_norm(DEEPNORM_ALPHA * x + mix, ln1_g, ln1_b)
    x = layer_norm(DEEPNORM_ALPHA * x + hier_moe(x, w_group, b_group, w_router, b_router, w_gate, w_up, w_down),
                   ln2_g, ln2_b)
    return x


def run_trunk(x, weights):
    cos, sin = axial_rope_tables(x.shape[1])
    for layer in range(DEPTH):
        x = encoder_layer(x, cos, sin, *[w[layer] for w in weights])
    return x


def setup_inputs(seed: int = 0) -> dict:
    key = jax.random.key(seed)
    ks = jax.random.split(key, 32)
    f32 = jnp.float32
    L = DEPTH

    def nrm(k, shape, scale):
        return jax.random.normal(k, shape, f32) * scale

    dt0 = jnp.exp(jax.random.uniform(ks[10], (L, 2, SSD_HEADS), f32, math.log(1e-3), math.log(1e-1)))
    return {
        'x_prompt': jax.random.normal(ks[0], (BATCH, SEQ, D_MODEL), f32),
        'x_sample': jax.random.normal(ks[1], (DEC_BATCH, DEC_SEQ, D_MODEL), f32),
        'w_in': nrm(ks[2], (L, D_MODEL, N_IN), D_MODEL ** -0.5),
        'q_norm_g': 1.0 + nrm(ks[3], (L, HEAD_DIM), 0.02),
        'k_norm_g': 1.0 + nrm(ks[4], (L, HEAD_DIM), 0.02),
        'w_fnet': nrm(ks[5], (L, FNET_WIDTH, FNET_WIDTH), FNET_WIDTH ** -0.5),
        'w_pool': nrm(ks[6], (L, POOL_GROUPS, POOL_GROUP_DIM, POOL_GROUP_DIM), POOL_GROUP_DIM ** -0.5),
        'pool_scale': 1.0 + nrm(ks[7], (L, POOL_WIDTH), 0.02),
        'conv_w': nrm(ks[8], (L, SSD_CONV, SSD_XBC), SSD_CONV ** -0.5),
        'conv_b': nrm(ks[9], (L, SSD_XBC), 0.02),
        'dt_bias': dt0 + jnp.log(-jnp.expm1(-dt0)),
        'a_log': jnp.log(jax.random.uniform(ks[11], (L, 2, SSD_HEADS), f32, 1.0, 16.0)),
        'd_skip': 1.0 + nrm(ks[12], (L, SSD_HEADS), 0.02),
        'ssd_norm_g': 1.0 + nrm(ks[13], (L, SSD_WIDTH), 0.02),
        'w_out': nrm(ks[14], (L, MIX_WIDTH, D_MODEL), DEEPNORM_BETA * MIX_WIDTH ** -0.5),
        'ln1_g': 1.0 + nrm(ks[15], (L, D_MODEL), 0.02),
        'ln1_b': nrm(ks[16], (L, D_MODEL), 0.02),
        'w_group': nrm(ks[17], (L, D_MODEL, N_GROUPS), D_MODEL ** -0.5),
        'b_group': nrm(ks[18], (L, N_GROUPS), 0.01),
        'w_router': nrm(ks[19], (L, D_MODEL, N_EXPERTS), D_MODEL ** -0.5),
        'b_router': nrm(ks[20], (L, N_EXPERTS), 0.01),
        'w_gate': nrm(ks[21], (L, N_EXPERTS, D_MODEL, EXPERT_FF), D_MODEL ** -0.5),
        'w_up': nrm(ks[22], (L, N_EXPERTS, D_MODEL, EXPERT_FF), D_MODEL ** -0.5),
        'w_down': nrm(ks[23], (L, N_EXPERTS, EXPERT_FF, D_MODEL), DEEPNORM_BETA * EXPERT_FF ** -0.5),
        'ln2_g': 1.0 + nrm(ks[24], (L, D_MODEL), 0.02),
        'ln2_b': nrm(ks[25], (L, D_MODEL), 0.02),
    }


def reference(x_prompt, x_sample, w_in, q_norm_g, k_norm_g, w_fnet, w_pool, pool_scale, conv_w, conv_b,
              dt_bias, a_log, d_skip, ssd_norm_g, w_out, ln1_g, ln1_b, w_group, b_group, w_router, b_router,
              w_gate, w_up, w_down, ln2_g, ln2_b):
    weights = (w_in, q_norm_g, k_norm_g, w_fnet, w_pool, pool_scale, conv_w, conv_b, dt_bias, a_log,
               d_skip, ssd_norm_g, w_out, ln1_g, ln1_b, w_group, b_group, w_router, b_router,
               w_gate, w_up, w_down, ln2_g, ln2_b)
    y_prompt = run_trunk(x_prompt, weights)
    y_sample = run_trunk(x_sample, weights)
    return (y_prompt, y_sample)
```

```python
import functools
import math

import numpy as np
import jax
import jax.numpy as jnp
from jax import lax
from jax.experimental import pallas as pl
from jax.experimental.pallas import tpu as pltpu

F32 = jnp.float32
BF16 = jnp.bfloat16

D_MODEL = 1024
DEPTH = 2
GRID_W = 64
HEAD_DIM = 64
ROPE_THETA = 10000.0
ATT_Q_HEADS = 4
ATT_KV_HEADS = 2
POOL_WINDOWS = (2, 4, 8, 16)
SSD_HEADS = 4
SSD_CONV = 4
SSD_CHUNK = 128
N_GROUPS = 4
EXPERTS_PER_GROUP = 4
N_EXPERTS = 16
EXPERT_FF = 256
DEEPNORM_ALPHA = (2 * DEPTH) ** 0.25
LN_EPS = 1e-5
RMS_EPS = 1e-6

LANES = 128
SUBLANES = 8
HALO = SUBLANES
VMEM_LIMIT = 56 * 1024 * 1024

C_Q, C_K, C_V, C_UF, C_UP, C_Z, C_XBC, C_DT, N_PROJ = 0, 256, 384, 512, 768, 1024, 1280, 1792, 1920
HI = lax.Precision.HIGHEST


def _cparams(sem):
    return pltpu.CompilerParams(dimension_semantics=sem, vmem_limit_bytes=VMEM_LIMIT)


def _dot(a, b):
    return jnp.dot(a, b, preferred_element_type=F32)


def _dot_nt(a, b):
    return lax.dot_general(a, b, (((1,), (1,)), ((), ())), preferred_element_type=F32)


def _silu(x):
    return x * (1.0 / (1.0 + jnp.exp(-x)))


def _layer_norm(x, g, b):
    mu = jnp.mean(x, axis=-1, keepdims=True)
    xc = x - mu
    var = jnp.mean(xc * xc, axis=-1, keepdims=True)
    return xc * lax.rsqrt(var + LN_EPS) * g + b


def _head_sumsq(x, ones_bd):
    sq = x * x
    hi = sq.astype(BF16)
    lo = (sq - hi.astype(F32)).astype(BF16)
    return _dot(hi, ones_bd) + _dot(lo, ones_bd)


def _rope(x, cosv, sinv, first_half):
    w = x.shape[-1]
    partner = jnp.where(first_half, pltpu.roll(x, w - 16, axis=1), pltpu.roll(x, 16, axis=1))
    return x * cosv + partner * sinv


def _in_proj_kernel(x_ref, w_ref, qg_ref, kg_ref, cos_ref, sin_ref, bd_ref, dftc_ref, dtb_ref,
                    q_ref, k_ref, v_ref, ab_ref, up_ref, z_ref, xbc_ref, dt_ref):
    xb = x_ref[...].astype(BF16)
    h = _dot(xb, w_ref[...])
    cosv = cos_ref[...]
    sinv = sin_ref[...]
    bd = bd_ref[...]
    lane = lax.broadcasted_iota(jnp.int32, (1, LANES), 1)
    first_half = (lane & 31) < 16
    for c in range(2):
        qc = h[:, C_Q + c * LANES:C_Q + (c + 1) * LANES]
        ss = _head_sumsq(qc, bd)
        qn = qc * lax.rsqrt(ss * (1.0 / HEAD_DIM) + RMS_EPS) * qg_ref[...]
        qr = _rope(qn, cosv, sinv, first_half) * (HEAD_DIM ** -0.5)
        q_ref[:, c * LANES:(c + 1) * LANES] = qr.astype(BF16)
    kc = h[:, C_K:C_K + LANES]
    ss = _head_sumsq(kc, bd)
    kn = kc * lax.rsqrt(ss * (1.0 / HEAD_DIM) + RMS_EPS) * kg_ref[...]
    k_ref[...] = _rope(kn, cosv, sinv, first_half).astype(BF16)
    v_ref[...] = h[:, C_V:C_V + LANES].astype(BF16)
    uf = h[:, C_UF:C_UF + 256].astype(BF16)
    ab_ref[...] = _dot(uf, dftc_ref[...]).astype(BF16)
    up_ref[...] = h[:, C_UP:C_UP + 256]
    z_ref[...] = h[:, C_Z:C_Z + 256]
    xbc_ref[...] = h[:, C_XBC:C_XBC + 512]
    dr = h[:, C_DT:C_DT + LANES] + dtb_ref[...]
    dt_ref[...] = jnp.maximum(dr, 0.0) + jnp.log(1.0 + jnp.exp(-jnp.abs(dr)))


def _in_proj(x, w, qg, kg, cosv, sinv, bd, dftc, dtb, seq):
    t = x.shape[0]
    tm = min(512, seq)
    nseq = seq // tm
    full = lambda a: pl.BlockSpec(a.shape, lambda i: (0,) * a.ndim)
    row = lambda wdt: pl.BlockSpec((tm, wdt), lambda i: (i, 0))
    pos = pl.BlockSpec((tm, LANES), lambda i: (i % nseq, 0))
    outs = [(256, BF16), (128, BF16), (128, BF16), (512, BF16), (256, F32), (256, F32), (512, F32),
            (128, F32)]
    return pl.pallas_call(
        _in_proj_kernel,
        grid=(t // tm,),
        in_specs=[row(D_MODEL), full(w), full(qg), full(kg), pos, pos, full(bd), full(dftc), full(dtb)],
        out_specs=[row(wd) for wd, _ in outs],
        out_shape=[jax.ShapeDtypeStruct((t, wd), dt) for wd, dt in outs],
        compiler_params=_cparams(("parallel",)),
        name="in_proj",
    )(x, w, qg, kg, cosv, sinv, bd, dftc, dtb)


def _attn_kernel(q_ref, k_ref, v_ref, o_ref):
    k = k_ref[...]
    v = v_ref[...]
    lane = lax.broadcasted_iota(jnp.int32, (1, LANES), 1)
    for c in range(2):
        qc = q_ref[:, c * LANES:(c + 1) * LANES]
        out = None
        for j in range(ATT_KV_HEADS):
            sel = (lane >> 6) == j
            qm = jnp.where(sel, qc, jnp.zeros_like(qc))
            s = _dot_nt(qm, k)
            m = jnp.max(s, axis=-1, keepdims=True)
            p = jnp.exp(s - m)
            l = jnp.sum(p, axis=-1, keepdims=True)
            o = _dot(p.astype(BF16), v) / l
            out = o if out is None else jnp.where(sel, o, out)
        o_ref[:, c * LANES:(c + 1) * LANES] = out.astype(BF16)


def _attention(q, k, v, b, s):
    tq = min(256, s)
    q3, k3, v3 = q.reshape(b, s, 256), k.reshape(b, s, LANES), v.reshape(b, s, LANES)
    out = pl.pallas_call(
        _attn_kernel,
        grid=(b, s // tq),
        in_specs=[pl.BlockSpec((None, tq, 256), lambda bi, i: (bi, i, 0)),
                  pl.BlockSpec((None, s, LANES), lambda bi, i: (bi, 0, 0)),
                  pl.BlockSpec((None, s, LANES), lambda bi, i: (bi, 0, 0))],
        out_specs=pl.BlockSpec((None, tq, 256), lambda bi, i: (bi, i, 0)),
        out_shape=jax.ShapeDtypeStruct((b, s, 256), BF16),
        compiler_params=_cparams(("parallel", "parallel")),
        name="attention",
    )(q3, k3, v3)
    return out.reshape(b * s, 256)


def _fourier_kernel(c_ref, s_ref, ab_ref, w_ref, o_ref):
    acc = _dot(c_ref[...], ab_ref[:, 0:256]) + _dot(s_ref[...], ab_ref[:, 256:512])
    o_ref[...] = _dot(acc.astype(BF16), w_ref[...]).astype(BF16)


def _fourier(cmat, smat, ab, w_fnet, b, s):
    tm = min(512, s) if s <= 4096 else 256
    ab3 = ab.reshape(b, s, 512)
    out = pl.pallas_call(
        _fourier_kernel,
        grid=(s // tm, b),
        in_specs=[pl.BlockSpec((tm, s), lambda i, bi: (i, 0)),
                  pl.BlockSpec((tm, s), lambda i, bi: (i, 0)),
                  pl.BlockSpec((None, s, 512), lambda i, bi: (bi, 0, 0)),
                  pl.BlockSpec((256, 256), lambda i, bi: (0, 0))],
        out_specs=pl.BlockSpec((None, tm, 256), lambda i, bi: (bi, i, 0)),
        out_shape=jax.ShapeDtypeStruct((b, s, 256), BF16),
        compiler_params=_cparams(("parallel", "parallel")),
        name="fourier",
    )(cmat, smat, ab3, w_fnet)
    return out.reshape(b * s, 256)


def _halo_specs(tt, width, s):
    nb = tt // HALO
    last = s // HALO - 1
    main = pl.BlockSpec((None, tt, width), lambda bi, i: (bi, i, 0))
    prev = pl.BlockSpec((None, HALO, width), lambda bi, i: (bi, jnp.maximum(i * nb - 1, 0), 0))
    nxt = pl.BlockSpec((None, HALO, width), lambda bi, i: (bi, jnp.minimum((i + 1) * nb, last), 0))
    return [main, prev, nxt]


def _with_halo(main_ref, prev_ref, next_ref):
    i = pl.program_id(1)
    n = pl.num_programs(1)
    prev = jnp.where(i > 0, prev_ref[...], 0.0)
    nxt = jnp.where(i < n - 1, next_ref[...], 0.0)
    return jnp.concatenate([prev, main_ref[...], nxt], axis=0)


def _shifted(ext, d, tt):
    n = ext.shape[0]
    r = ext if d == 0 else pltpu.roll(ext, (-d) % n, axis=0)
    return r[HALO:HALO + tt]


def _pool_kernel(u_ref, up_ref, un_ref, w_ref, sc_ref, o_ref, *, seq):
    tt = u_ref.shape[0]
    ext = _with_halo(u_ref, up_ref, un_ref)
    n = ext.shape[0]
    sh = lambda a, d: a if d == 0 else pltpu.roll(a, (-d) % n, axis=0)
    p2 = sh(ext, -1) + ext
    p4 = sh(p2, -1) + sh(p2, 1)
    p8 = sh(p4, -2) + sh(p4, 2)
    p16 = sh(p8, -4) + sh(p8, 4)
    t = pl.program_id(1) * tt + lax.broadcasted_iota(jnp.int32, (tt, 1), 0)
    grp = lax.broadcasted_iota(jnp.int32, (1, 256), 1) >> 6
    u = u_ref[...]
    pooled = None
    for gi, (w, pw) in enumerate(zip(POOL_WINDOWS, (p2, p4, p8, p16))):
        lo = jnp.clip(t - w // 2, 0, seq - 1)
        hi = jnp.clip(t + (w - w // 2) - 1, 0, seq - 1)
        cnt = (hi - lo + 1).astype(F32)
        val = pw[HALO:HALO + tt] / cnt - u
        pooled = val if pooled is None else jnp.where(grp == gi, val, pooled)
    o_ref[...] = (_dot(pooled.astype(BF16), w_ref[...]) * sc_ref[...]).astype(BF16)


def _pool(u, w_bd, scale, b, s):
    tt = min(1024, s)
    u3 = u.reshape(b, s, 256)
    full = lambda a: pl.BlockSpec(a.shape, lambda bi, i: (0,) * a.ndim)
    out = pl.pallas_call(
        functools.partial(_pool_kernel, seq=s),
        grid=(b, s // tt),
        in_specs=_halo_specs(tt, 256, s) + [full(w_bd), full(scale)],
        out_specs=pl.BlockSpec((None, tt, 256), lambda bi, i: (bi, i, 0)),
        out_shape=jax.ShapeDtypeStruct((b, s, 256), BF16),
        compiler_params=_cparams(("parallel", "parallel")),
        name="pool",
    )(u3, u3, u3, w_bd, scale)
    return out.reshape(b * s, 256)


def _conv_kernel(x_ref, xp_ref, xn_ref, w_ref, b_ref, o_ref):
    tt = x_ref.shape[0]
    ext = _with_halo(x_ref, xp_ref, xn_ref)
    acc = b_ref[...] + _shifted(ext, -2, tt) * w_ref[0:1, :]
    for kk in range(1, SSD_CONV):
        acc = acc + _shifted(ext, kk - 2, tt) * w_ref[kk:kk + 1, :]
    o_ref[...] = _silu(acc)


def _conv(xbc, conv_w, conv_b, b, s):
    tt = min(1024, s)
    x3 = xbc.reshape(b, s, 512)
    full = lambda a: pl.BlockSpec(a.shape, lambda bi, i: (0,) * a.ndim)
    return pl.pallas_call(
        _conv_kernel,
        grid=(b, s // tt),
        in_specs=_halo_specs(tt, 512, s) + [full(conv_w), full(conv_b)],
        out_specs=pl.BlockSpec((None, tt, 512), lambda bi, i: (bi, i, 0)),
        out_shape=jax.ShapeDtypeStruct((b, s, 512), F32),
        compiler_params=_cparams(("parallel", "parallel")),
        name="ssd_conv",
    )(x3, x3, x3, conv_w, conv_b)


def _expand_heads(v, off):
    head = lax.broadcasted_iota(jnp.int32, (1, 256), 1) >> 6
    out = v[:, off + 3:off + 4]
    for h in (2, 1, 0):
        out = jnp.where(head == h, v[:, off + h:off + h + 1], out)
    return out


def _ssd_chunk(xc, dtc, a_row, carry, off, backward):
    L = SSD_CHUNK
    xs = xc[:, 0:256]
    bmat = xc[:, 256:384]
    cmat = xc[:, 384:512]
    r = lax.broadcasted_iota(jnp.int32, (L, L), 0)
    c = lax.broadcasted_iota(jnp.int32, (L, L), 1)
    keep = (c >= r) if backward else (c <= r)
    tri = keep.astype(F32)
    adt = dtc * a_row
    cs = jnp.dot(tri, adt, precision=HI, preferred_element_type=F32)
    cs_t = cs.T
    tot = cs[0:1, :] if backward else cs[L - 1:L, :]
    e_cs = jnp.exp(cs)
    ds = jnp.exp(tot - cs)
    xd = xs * _expand_heads(dtc, off)
    xd_b = xd.astype(BF16)
    b_t = bmat.T.astype(BF16)
    grp = lax.broadcasted_iota(jnp.int32, (1, LANES), 1) >> 6
    head = lax.broadcasted_iota(jnp.int32, (1, 256), 1) >> 6
    g_mats = [_dot(jnp.where(grp == g, cmat, 0.0).astype(BF16), b_t) for g in range(2)]
    y = None
    for h in range(SSD_HEADS):
        diff = cs[:, off + h:off + h + 1] - cs_t[off + h:off + h + 1, :]
        decay = jnp.where(keep, jnp.exp(jnp.where(keep, diff, 0.0)), 0.0)
        sc = (g_mats[h // 2] * decay).astype(BF16)
        yd = _dot(sc, xd_b)
        y = yd if y is None else jnp.where(head == h, yd, y)
    y = y + _dot(cmat.astype(BF16), carry.astype(BF16)) * _expand_heads(e_cs, off)
    st = _dot(b_t, (xd * _expand_heads(ds, off)).astype(BF16))
    row_grp = lax.broadcasted_iota(jnp.int32, (LANES, 1), 0) >> 6
    st = jnp.where(row_grp == (head >> 1), st, 0.0)
    new_carry = carry * _expand_heads(jnp.exp(tot), off) + st
    return y, new_carry


def _ssd_kernel(xf_ref, dtf_ref, xb_ref, dtb_ref, alog_ref, dskip_ref, yf_ref, yb_ref, carry_ref):
    @pl.when(pl.program_id(1) == 0)
    def _():
        carry_ref[...] = jnp.zeros_like(carry_ref)

    nch = xf_ref.shape[0] // SSD_CHUNK
    a_row = -jnp.exp(alog_ref[...])
    dskip = dskip_ref[...]

    def body(ci, _):
        rf = pl.ds(pl.multiple_of(ci * SSD_CHUNK, SSD_CHUNK), SSD_CHUNK)
        xc = xf_ref[rf, :]
        y, cf = _ssd_chunk(xc, dtf_ref[rf, :], a_row, carry_ref[0], 0, False)
        carry_ref[0] = cf
        yf_ref[rf, :] = y + xc[:, 0:256] * dskip
        rb = pl.ds(pl.multiple_of((nch - 1 - ci) * SSD_CHUNK, SSD_CHUNK), SSD_CHUNK)
        y, cb = _ssd_chunk(xb_ref[rb, :], dtb_ref[rb, :], a_row, carry_ref[1], SSD_HEADS, True)
        carry_ref[1] = cb
        yb_ref[rb, :] = y
        return 0

    lax.fori_loop(0, nch, body, 0)


def _ssd(xc, dt, alog_row, dskip_row, b, s):
    tt = min(512, s)
    nt = s // tt
    dt3 = dt.reshape(b, s, LANES)
    fwd = lambda w: pl.BlockSpec((None, tt, w), lambda bi, i: (bi, i, 0))
    bwd = lambda w: pl.BlockSpec((None, tt, w), lambda bi, i: (bi, nt - 1 - i, 0))
    full = lambda a: pl.BlockSpec(a.shape, lambda bi, i: (0,) * a.ndim)
    yf, yb = pl.pallas_call(
        _ssd_kernel,
        grid=(b, nt),
        in_specs=[fwd(512), fwd(LANES), bwd(512), bwd(LANES), full(alog_row), full(dskip_row)],
        out_specs=[fwd(256), bwd(256)],
        out_shape=[jax.ShapeDtypeStruct((b, s, 256), F32)] * 2,
        scratch_shapes=[pltpu.VMEM((2, LANES, 256), F32)],
        compiler_params=_cparams(("parallel", "arbitrary")),
        name="ssd_scan",
    )(xc, dt3, xc, dt3, alog_row, dskip_row)
    return yf.reshape(b * s, 256), yb.reshape(b * s, 256)


def _out_proj_kernel(x_ref, att_ref, fno_ref, pol_ref, yf_ref, yb_ref, z_ref, w_ref, ng_ref,
                     g_ref, b_ref, o_ref):
    y = (yf_ref[...] + yb_ref[...]) * _silu(z_ref[...])
    ng = ng_ref[...]
    parts = []
    for gi in range(2):
        yg = y[:, gi * LANES:(gi + 1) * LANES]
        ms = jnp.mean(yg * yg, axis=-1, keepdims=True)
        parts.append((yg * lax.rsqrt(ms + RMS_EPS) * ng[:, gi * LANES:(gi + 1) * LANES]).astype(BF16))
    mix = _dot(att_ref[...], w_ref[0:256, :]) + _dot(fno_ref[...], w_ref[256:512, :])
    mix = mix + _dot(pol_ref[...], w_ref[512:768, :])
    mix = mix + _dot(parts[0], w_ref[768:896, :]) + _dot(parts[1], w_ref[896:1024, :])
    o_ref[...] = _layer_norm(DEEPNORM_ALPHA * x_ref[...] + mix, g_ref[...], b_ref[...])


def _out_proj(x, att, fno, pol, yf, yb, z, w, ng, g, bb):
    t = x.shape[0]
    tm = min(512, t)
    row = lambda wd: pl.BlockSpec((tm, wd), lambda i: (i, 0))
    full = lambda a: pl.BlockSpec(a.shape, lambda i: (0,) * a.ndim)
    return pl.pallas_call(
        _out_proj_kernel,
        grid=(t // tm,),
        in_specs=[row(D_MODEL)] + [row(256)] * 6 + [full(w), full(ng), full(g), full(bb)],
        out_specs=row(D_MODEL),
        out_shape=jax.ShapeDtypeStruct((t, D_MODEL), F32),
        compiler_params=_cparams(("parallel",)),
        name="out_proj_ln",
    )(x, att, fno, pol, yf, yb, z, w, ng, g, bb)


def _router_gates(x, wr, br):
    logits = jnp.dot(x, wr, precision=HI, preferred_element_type=F32) + br
    lane = lax.broadcasted_iota(jnp.int32, (1, LANES), 1)
    ninf = -jnp.inf
    gl = jnp.where(lane < N_GROUPS, logits, ninf)
    gmax = jnp.max(gl, axis=-1, keepdims=True)
    g_p = 1.0 / jnp.sum(jnp.exp(gl - gmax), axis=-1, keepdims=True)
    g_idx = jnp.min(jnp.where(gl == gmax, lane, LANES), axis=-1, keepdims=True)
    e_lane = lane - N_GROUPS
    in_grp = (e_lane >= 0) & (e_lane < N_EXPERTS) & ((e_lane >> 2) == g_idx)
    el = jnp.where(in_grp, logits, ninf)
    m1 = jnp.max(el, axis=-1, keepdims=True)
    i1 = jnp.min(jnp.where(el == m1, lane, LANES), axis=-1, keepdims=True)
    el2 = jnp.where(lane == i1, ninf, el)
    m2 = jnp.max(el2, axis=-1, keepdims=True)
    i2 = jnp.min(jnp.where(el2 == m2, lane, LANES), axis=-1, keepdims=True)
    e2 = jnp.exp(m2 - m1)
    w1 = 1.0 / (1.0 + e2)
    w2 = e2 * w1
    return g_p * (jnp.where(lane == i1, w1, 0.0) + jnp.where(lane == i2, w2, 0.0))


def _moe_kernel(x_ref, wr_ref, br_ref, wg_ref, wu_ref, wd_ref, g_ref, b_ref, o_ref,
                xb_ref, gate_ref, acc_ref):
    e = pl.program_id(1)

    @pl.when(e == 0)
    def _():
        x = x_ref[...]
        xb_ref[...] = x.astype(BF16)
        gate_ref[...] = _router_gates(x, wr_ref[...], br_ref[...])
        acc_ref[...] = jnp.zeros_like(acc_ref)

    xb = xb_ref[...]
    lane = lax.broadcasted_iota(jnp.int32, (1, LANES), 1)
    gate = jnp.sum(jnp.where(lane == e + N_GROUPS, gate_ref[...], 0.0), axis=-1, keepdims=True)
    hid = _silu(_dot(xb, wg_ref[...])) * _dot(xb, wu_ref[...])
    acc_ref[...] += _dot((hid * gate).astype(BF16), wd_ref[...])

    @pl.when(e == N_EXPERTS - 1)
    def _():
        o_ref[...] = _layer_norm(DEEPNORM_ALPHA * x_ref[...] + acc_ref[...], g_ref[...], b_ref[...])


def _moe(x, wr, br, wg, wu, wd, g, bb):
    t = x.shape[0]
    tm = min(1024, t)
    full = lambda a: pl.BlockSpec(a.shape, lambda i, e: (0,) * a.ndim)
    return pl.pallas_call(
        _moe_kernel,
        grid=(t // tm, N_EXPERTS),
        in_specs=[pl.BlockSpec((tm, D_MODEL), lambda i, e: (i, 0)), full(wr), full(br),
                  pl.BlockSpec((None, D_MODEL, EXPERT_FF), lambda i, e: (e, 0, 0)),
                  pl.BlockSpec((None, D_MODEL, EXPERT_FF), lambda i, e: (e, 0, 0)),
                  pl.BlockSpec((None, EXPERT_FF, D_MODEL), lambda i, e: (e, 0, 0)),
                  full(g), full(bb)],
        out_specs=pl.BlockSpec((tm, D_MODEL), lambda i, e: (i, 0)),
        out_shape=jax.ShapeDtypeStruct((t, D_MODEL), F32),
        scratch_shapes=[pltpu.VMEM((tm, D_MODEL), BF16), pltpu.VMEM((tm, LANES), F32),
                        pltpu.VMEM((tm, D_MODEL), F32)],
        compiler_params=_cparams(("parallel", "arbitrary")),
        name="moe_ln",
    )(x, wr, br, wg, wu, wd, g, bb)


def _rope_tables(seq):
    rows = seq // GRID_W
    row = jnp.repeat(jnp.arange(rows, dtype=F32), GRID_W)
    col = jnp.tile(jnp.arange(GRID_W, dtype=F32), rows)
    half = HEAD_DIM // 2
    freqs = 1.0 / (ROPE_THETA ** (jnp.arange(0, half, 2, dtype=F32) / half))
    ar, ac = row[:, None] * freqs, col[:, None] * freqs
    cosv = jnp.concatenate([jnp.cos(ar), jnp.cos(ar), jnp.cos(ac), jnp.cos(ac)], axis=-1)
    sinv = jnp.concatenate([-jnp.sin(ar), jnp.sin(ar), -jnp.sin(ac), jnp.sin(ac)], axis=-1)
    return jnp.tile(cosv, (1, 2)), jnp.tile(sinv, (1, 2))


def _position_dft(seq):
    blk = 64
    k = jnp.arange(seq, dtype=jnp.int32)
    ang = lambda j: (2.0 * math.pi / seq) * ((j[:, None] * k[None, :]) % seq).astype(F32)
    a_hi = ang(jnp.arange(0, seq, blk, dtype=jnp.int32))
    a_lo = ang(jnp.arange(blk, dtype=jnp.int32))
    ch, sh, cl, sl = jnp.cos(a_hi), jnp.sin(a_hi), jnp.cos(a_lo), jnp.sin(a_lo)
    sc = seq ** -0.5
    cmat = (ch[:, None, :] * cl[None] - sh[:, None, :] * sl[None]) * sc
    smat = (sh[:, None, :] * cl[None] + ch[:, None, :] * sl[None]) * (-sc)
    return cmat.reshape(seq, seq).astype(BF16), smat.reshape(seq, seq).astype(BF16)


def _channel_dft():
    n = np.arange(HEAD_DIM)
    ang = 2.0 * np.pi * ((n[:, None] * n[None, :]) % HEAD_DIM) / HEAD_DIM
    eye = np.eye(4)
    cb = np.kron(eye, np.cos(ang)) * HEAD_DIM ** -0.5
    sb = np.kron(eye, np.sin(ang)) * HEAD_DIM ** -0.5
    return jnp.asarray(np.concatenate([cb, sb], axis=1), dtype=BF16)


def _layer_params(l, w_in, q_norm_g, k_norm_g, w_fnet, w_pool, pool_scale, conv_w, conv_b, dt_bias,
                  a_log, d_skip, ssd_norm_g, w_out, ln1_g, ln1_b, w_group, b_group, w_router, b_router,
                  w_gate, w_up, w_down, ln2_g, ln2_b):
    wi = w_in[l]
    hperm = np.array([0, 2, 1, 3])
    wq = wi[:, 0:256].reshape(D_MODEL, 4, HEAD_DIM)[:, hperm].reshape(D_MODEL, 256)
    w_proj = jnp.concatenate([wq, wi[:, 256:1792], jnp.pad(wi[:, 1792:1800], ((0, 0), (0, 120)))],
                             axis=1).astype(BF16)
    wo = w_out[l]
    wo_att = wo[0:256].reshape(4, HEAD_DIM, D_MODEL)[hperm].reshape(256, D_MODEL)
    w_o = jnp.concatenate([wo_att, wo[256:]], axis=0).astype(BF16)
    row = lambda v, n: jnp.pad(v.reshape(1, -1), ((0, 0), (0, n - v.size)))
    w_pool_bd = jnp.zeros((256, 256), F32)
    for gi in range(4):
        w_pool_bd = w_pool_bd.at[gi * 64:(gi + 1) * 64, gi * 64:(gi + 1) * 64].set(w_pool[l, gi])
    return dict(
        w_proj=w_proj,
        qg=jnp.tile(q_norm_g[l], 2).reshape(1, LANES), kg=jnp.tile(k_norm_g[l], 2).reshape(1, LANES),
        dtb=row(dt_bias[l], LANES), w_fnet=w_fnet[l].astype(BF16),
        w_pool=w_pool_bd.astype(BF16), pool_scale=pool_scale[l].reshape(1, 256),
        conv_w=conv_w[l], conv_b=conv_b[l].reshape(1, 512),
        alog=row(a_log[l], LANES), dskip=jnp.repeat(d_skip[l], 64).reshape(1, 256),
        ssd_ng=ssd_norm_g[l].reshape(1, 256), w_o=w_o,
        ln1_g=ln1_g[l].reshape(1, D_MODEL), ln1_b=ln1_b[l].reshape(1, D_MODEL),
        wr=jnp.pad(jnp.concatenate([w_group[l], w_router[l]], axis=1), ((0, 0), (0, LANES - 20))),
        br=row(jnp.concatenate([b_group[l], b_router[l]]), LANES),
        wg=w_gate[l].astype(BF16), wu=w_up[l].astype(BF16), wd=w_down[l].astype(BF16),
        ln2_g=ln2_g[l].reshape(1, D_MODEL), ln2_b=ln2_b[l].reshape(1, D_MODEL),
    )


def _trunk(x3, params, tables):
    b, s, _ = x3.shape
    x = x3.reshape(b * s, D_MODEL)
    cosv, sinv, cmat, smat, bd, dftc = tables
    for p in params:
        q, k, v, ab, up, z, xbc, dt = _in_proj(x, p["w_proj"], p["qg"], p["kg"], cosv, sinv, bd, dftc,
                                               p["dtb"], s)
        att = _attention(q, k, v, b, s)
        fno = _fourier(cmat, smat, ab, p["w_fnet"], b, s)
        pol = _pool(up, p["w_pool"], p["pool_scale"], b, s)
        xc = _conv(xbc, p["conv_w"], p["conv_b"], b, s)
        yf, yb = _ssd(xc, dt, p["alog"], p["dskip"], b, s)
        x = _out_proj(x, att, fno, pol, yf, yb, z, p["w_o"], p["ssd_ng"], p["ln1_g"], p["ln1_b"])
        x = _moe(x, p["wr"], p["br"], p["wg"], p["wu"], p["wd"], p["ln2_g"], p["ln2_b"])
    return x.reshape(b, s, D_MODEL)


def _tables(seq):
    cosv, sinv = _rope_tables(seq)
    cmat, smat = _position_dft(seq)
    lane = np.arange(LANES)
    bd = jnp.asarray((lane[:, None] // HEAD_DIM) == (lane[None, :] // HEAD_DIM), dtype=BF16)
    return cosv, sinv, cmat, smat, bd, _channel_dft()


def kernel(x_prompt, x_sample, w_in, q_norm_g, k_norm_g, w_fnet, w_pool, pool_scale, conv_w, conv_b, dt_bias, a_log, d_skip, ssd_norm_g, w_out, ln1_g, ln1_b, w_group, b_group, w_router, b_router, w_gate, w_up, w_down, ln2_g, ln2_b):
    weights = (w_in, q_norm_g, k_norm_g, w_fnet, w_pool, pool_scale, conv_w, conv_b, dt_bias, a_log,
               d_skip, ssd_norm_g, w_out, ln1_g, ln1_b, w_group, b_group, w_router, b_router,
               w_gate, w_up, w_down, ln2_g, ln2_b)
    params = [_layer_params(l, *weights) for l in range(w_in.shape[0])]
    y_prompt = _trunk(x_prompt, params, _tables(x_prompt.shape[1]))
    y_sample = _trunk(x_sample, params, _tables(x_sample.shape[1]))
    return (y_prompt, y_sample)
```

```python
import functools
import math

import numpy as np
import jax
import jax.numpy as jnp
from jax import lax
from jax.experimental import pallas as pl
from jax.experimental.pallas import tpu as pltpu

F32 = jnp.float32
BF16 = jnp.bfloat16

D_MODEL = 1024
DEPTH = 2
GRID_W = 64
HEAD_DIM = 64
ROPE_THETA = 10000.0
ATT_Q_HEADS = 4
ATT_KV_HEADS = 2
POOL_WINDOWS = (2, 4, 8, 16)
SSD_HEADS = 4
SSD_CONV = 4
SSD_CHUNK = 128
N_GROUPS = 4
EXPERTS_PER_GROUP = 4
N_EXPERTS = 16
EXPERT_FF = 256
DEEPNORM_ALPHA = (2 * DEPTH) ** 0.25
LN_EPS = 1e-5
RMS_EPS = 1e-6

LANES = 128
SUBLANES = 8
HALO = SUBLANES
VMEM_LIMIT = 56 * 1024 * 1024

C_Q, C_K, C_V, C_UF, C_UP, C_Z, C_XBC, C_DT, N_PROJ = 0, 256, 384, 512, 768, 1024, 1280, 1792, 1920
HI = lax.Precision.HIGHEST


def _cparams(sem):
    return pltpu.CompilerParams(dimension_semantics=sem, vmem_limit_bytes=VMEM_LIMIT)


def _dot(a, b):
    return jnp.dot(a, b, preferred_element_type=F32)


def _dot_nt(a, b):
    return lax.dot_general(a, b, (((1,), (1,)), ((), ())), preferred_element_type=F32)


def _silu(x):
    return x * (1.0 / (1.0 + jnp.exp(-x)))


def _layer_norm(x, g, b):
    mu = jnp.mean(x, axis=-1, keepdims=True)
    xc = x - mu
    var = jnp.mean(xc * xc, axis=-1, keepdims=True)
    return xc * lax.rsqrt(var + LN_EPS) * g + b


def _head_sumsq(x, ones_bd):
    sq = x * x
    hi = sq.astype(BF16)
    lo = (sq - hi.astype(F32)).astype(BF16)
    return _dot(hi, ones_bd) + _dot(lo, ones_bd)


def _rope(x, cosv, sinv, first_half):
    w = x.shape[-1]
    partner = jnp.where(first_half, pltpu.roll(x, w - 16, axis=1), pltpu.roll(x, 16, axis=1))
    return x * cosv + partner * sinv


def _in_proj_kernel(x_ref, w_ref, qg_ref, kg_ref, cos_ref, sin_ref, bd_ref, dftc_ref, dtb_ref,
                    q_ref, k_ref, v_ref, ab_ref, up_ref, z_ref, xbc_ref, dt_ref):
    xb = x_ref[...].astype(BF16)
    h = _dot(xb, w_ref[...])
    cosv = cos_ref[...]
    sinv = sin_ref[...]
    bd = bd_ref[...]
    lane = lax.broadcasted_iota(jnp.int32, (1, LANES), 1)
    first_half = (lane & 31) < 16
    for c in range(2):
        qc = h[:, C_Q + c * LANES:C_Q + (c + 1) * LANES]
        ss = _head_sumsq(qc, bd)
        qn = qc * lax.rsqrt(ss * (1.0 / HEAD_DIM) + RMS_EPS) * qg_ref[...]
        qr = _rope(qn, cosv, sinv, first_half) * (HEAD_DIM ** -0.5)
        q_ref[:, c * LANES:(c + 1) * LANES] = qr.astype(BF16)
    kc = h[:, C_K:C_K + LANES]
    ss = _head_sumsq(kc, bd)
    kn = kc * lax.rsqrt(ss * (1.0 / HEAD_DIM) + RMS_EPS) * kg_ref[...]
    k_ref[...] = _rope(kn, cosv, sinv, first_half).astype(BF16)
    v_ref[...] = h[:, C_V:C_V + LANES].astype(BF16)
    uf = h[:, C_UF:C_UF + 256].astype(BF16)
    ab_ref[...] = _dot(uf, dftc_ref[...]).astype(BF16)
    up_ref[...] = h[:, C_UP:C_UP + 256]
    z_ref[...] = h[:, C_Z:C_Z + 256]
    xbc_ref[...] = h[:, C_XBC:C_XBC + 512]
    dr = h[:, C_DT:C_DT + LANES] + dtb_ref[...]
    dt_ref[...] = jnp.maximum(dr, 0.0) + jnp.log(1.0 + jnp.exp(-jnp.abs(dr)))


def _in_proj(x, w, qg, kg, cosv, sinv, bd, dftc, dtb, seq):
    t = x.shape[0]
    tm = min(512, seq)
    nseq = seq // tm
    full = lambda a: pl.BlockSpec(a.shape, lambda i: (0,) * a.ndim)
    row = lambda wdt: pl.BlockSpec((tm, wdt), lambda i: (i, 0))
    pos = pl.BlockSpec((tm, LANES), lambda i: (i % nseq, 0))
    outs = [(256, BF16), (128, BF16), (128, BF16), (512, BF16), (256, F32), (256, F32), (512, F32),
            (128, F32)]
    return pl.pallas_call(
        _in_proj_kernel,
        grid=(t // tm,),
        in_specs=[row(D_MODEL), full(w), full(qg), full(kg), pos, pos, full(bd), full(dftc), full(dtb)],
        out_specs=[row(wd) for wd, _ in outs],
        out_shape=[jax.ShapeDtypeStruct((t, wd), dt) for wd, dt in outs],
        compiler_params=_cparams(("parallel",)),
        name="in_proj",
    )(x, w, qg, kg, cosv, sinv, bd, dftc, dtb)


def _attn_kernel(q_ref, k_ref, v_ref, o_ref):
    k = k_ref[...]
    v = v_ref[...]
    lane = lax.broadcasted_iota(jnp.int32, (1, LANES), 1)
    for c in range(2):
        qc = q_ref[:, c * LANES:(c + 1) * LANES]
        out = None
        for j in range(ATT_KV_HEADS):
            sel = (lane >> 6) == j
            qm = jnp.where(sel, qc, jnp.zeros_like(qc))
            s = _dot_nt(qm, k)
            m = jnp.max(s, axis=-1, keepdims=True)
            p = jnp.exp(s - m)
            l = jnp.sum(p, axis=-1, keepdims=True)
            o = _dot(p.astype(BF16), v) / l
            out = o if out is None else jnp.where(sel, o, out)
        o_ref[:, c * LANES:(c + 1) * LANES] = out.astype(BF16)


def _attention(q, k, v, b, s):
    tq = min(256, s)
    q3, k3, v3 = q.reshape(b, s, 256), k.reshape(b, s, LANES), v.reshape(b, s, LANES)
    out = pl.pallas_call(
        _attn_kernel,
        grid=(b, s // tq),
        in_specs=[pl.BlockSpec((None, tq, 256), lambda bi, i: (bi, i, 0)),
                  pl.BlockSpec((None, s, LANES), lambda bi, i: (bi, 0, 0)),
                  pl.BlockSpec((None, s, LANES), lambda bi, i: (bi, 0, 0))],
        out_specs=pl.BlockSpec((None, tq, 256), lambda bi, i: (bi, i, 0)),
        out_shape=jax.ShapeDtypeStruct((b, s, 256), BF16),
        compiler_params=_cparams(("parallel", "parallel")),
        name="attention",
    )(q3, k3, v3)
    return out.reshape(b * s, 256)


def _fourier_kernel(c_ref, s_ref, ab_ref, w_ref, o_ref):
    acc = _dot(c_ref[...], ab_ref[:, 0:256]) + _dot(s_ref[...], ab_ref[:, 256:512])
    o_ref[...] = _dot(acc.astype(BF16), w_ref[...]).astype(BF16)


def _fourier(cmat, smat, ab, w_fnet, b, s):
    tm = min(512, s) if s <= 4096 else 256
    ab3 = ab.reshape(b, s, 512)
    out = pl.pallas_call(
        _fourier_kernel,
        grid=(s // tm, b),
        in_specs=[pl.BlockSpec((tm, s), lambda i, bi: (i, 0)),
                  pl.BlockSpec((tm, s), lambda i, bi: (i, 0)),
                  pl.BlockSpec((None, s, 512), lambda i, bi: (bi, 0, 0)),
                  pl.BlockSpec((256, 256), lambda i, bi: (0, 0))],
        out_specs=pl.BlockSpec((None, tm, 256), lambda i, bi: (bi, i, 0)),
        out_shape=jax.ShapeDtypeStruct((b, s, 256), BF16),
        compiler_params=_cparams(("parallel", "parallel")),
        name="fourier",
    )(cmat, smat, ab3, w_fnet)
    return out.reshape(b * s, 256)


def _halo_specs(tt, width, s):
    nb = tt // HALO
    last = s // HALO - 1
    main = pl.BlockSpec((None, tt, width), lambda bi, i: (bi, i, 0))
    prev = pl.BlockSpec((None, HALO, width), lambda bi, i: (bi, jnp.maximum(i * nb - 1, 0), 0))
    nxt = pl.BlockSpec((None, HALO, width), lambda bi, i: (bi, jnp.minimum((i + 1) * nb, last), 0))
    return [main, prev, nxt]


def _with_halo(main_ref, prev_ref, next_ref):
    i = pl.program_id(1)
    n = pl.num_programs(1)
    prev = jnp.where(i > 0, prev_ref[...], 0.0)
    nxt = jnp.where(i < n - 1, next_ref[...], 0.0)
    return jnp.concatenate([prev, main_ref[...], nxt], axis=0)


def _shifted(ext, d, tt):
    n = ext.shape[0]
    r = ext if d == 0 else pltpu.roll(ext, (-d) % n, axis=0)
    return r[HALO:HALO + tt]


def _pool_kernel(u_ref, up_ref, un_ref, w_ref, sc_ref, o_ref, *, seq):
    tt = u_ref.shape[0]
    ext = _with_halo(u_ref, up_ref, un_ref)
    n = ext.shape[0]
    sh = lambda a, d: a if d == 0 else pltpu.roll(a, (-d) % n, axis=0)
    p2 = sh(ext, -1) + ext
    p4 = sh(p2, -1) + sh(p2, 1)
    p8 = sh(p4, -2) + sh(p4, 2)
    p16 = sh(p8, -4) + sh(p8, 4)
    t = pl.program_id(1) * tt + lax.broadcasted_iota(jnp.int32, (tt, 1), 0)
    grp = lax.broadcasted_iota(jnp.int32, (1, 256), 1) >> 6
    u = u_ref[...]
    pooled = None
    for gi, (w, pw) in enumerate(zip(POOL_WINDOWS, (p2, p4, p8, p16))):
        lo = jnp.clip(t - w // 2, 0, seq - 1)
        hi = jnp.clip(t + (w - w // 2) - 1, 0, seq - 1)
        cnt = (hi - lo + 1).astype(F32)
        val = pw[HALO:HALO + tt] / cnt - u
        pooled = val if pooled is None else jnp.where(grp == gi, val, pooled)
    o_ref[...] = (_dot(pooled.astype(BF16), w_ref[...]) * sc_ref[...]).astype(BF16)


def _pool(u, w_bd, scale, b, s):
    tt = min(1024, s)
    u3 = u.reshape(b, s, 256)
    full = lambda a: pl.BlockSpec(a.shape, lambda bi, i: (0,) * a.ndim)
    out = pl.pallas_call(
        functools.partial(_pool_kernel, seq=s),
        grid=(b, s // tt),
        in_specs=_halo_specs(tt, 256, s) + [full(w_bd), full(scale)],
        out_specs=pl.BlockSpec((None, tt, 256), lambda bi, i: (bi, i, 0)),
        out_shape=jax.ShapeDtypeStruct((b, s, 256), BF16),
        compiler_params=_cparams(("parallel", "parallel")),
        name="pool",
    )(u3, u3, u3, w_bd, scale)
    return out.reshape(b * s, 256)


def _conv_kernel(x_ref, xp_ref, xn_ref, w_ref, b_ref, o_ref):
    tt = x_ref.shape[0]
    ext = _with_halo(x_ref, xp_ref, xn_ref)
    acc = b_ref[...] + _shifted(ext, -2, tt) * w_ref[0:1, :]
    for kk in range(1, SSD_CONV):
        acc = acc + _shifted(ext, kk - 2, tt) * w_ref[kk:kk + 1, :]
    o_ref[...] = _silu(acc)


def _conv(xbc, conv_w, conv_b, b, s):
    tt = min(1024, s)
    x3 = xbc.reshape(b, s, 512)
    full = lambda a: pl.BlockSpec(a.shape, lambda bi, i: (0,) * a.ndim)
    return pl.pallas_call(
        _conv_kernel,
        grid=(b, s // tt),
        in_specs=_halo_specs(tt, 512, s) + [full(conv_w), full(conv_b)],
        out_specs=pl.BlockSpec((None, tt, 512), lambda bi, i: (bi, i, 0)),
        out_shape=jax.ShapeDtypeStruct((b, s, 512), F32),
        compiler_params=_cparams(("parallel", "parallel")),
        name="ssd_conv",
    )(x3, x3, x3, conv_w, conv_b)


def _expand_heads(v, off):
    head = lax.broadcasted_iota(jnp.int32, (1, 256), 1) >> 6
    out = v[:, off + 3:off + 4]
    for h in (2, 1, 0):
        out = jnp.where(head == h, v[:, off + h:off + h + 1], out)
    return out


def _ssd_chunk(xc, dtc, a_row, carry, off, backward):
    L = SSD_CHUNK
    xs = xc[:, 0:256]
    bmat = xc[:, 256:384]
    cmat = xc[:, 384:512]
    r = lax.broadcasted_iota(jnp.int32, (L, L), 0)
    c = lax.broadcasted_iota(jnp.int32, (L, L), 1)
    keep = (c >= r) if backward else (c <= r)
    tri = keep.astype(F32)
    adt = dtc * a_row
    cs = jnp.dot(tri, adt, precision=HI, preferred_element_type=F32)
    cs_t = cs.T
    tot = cs[0:1, :] if backward else cs[L - 1:L, :]
    e_cs = jnp.exp(cs)
    ds = jnp.exp(tot - cs)
    xd = xs * _expand_heads(dtc, off)
    xd_b = xd.astype(BF16)
    b_t = bmat.T.astype(BF16)
    grp = lax.broadcasted_iota(jnp.int32, (1, LANES), 1) >> 6
    head = lax.broadcasted_iota(jnp.int32, (1, 256), 1) >> 6
    g_mats = [_dot(jnp.where(grp == g, cmat, 0.0).astype(BF16), b_t) for g in range(2)]
    y = None
    for h in range(SSD_HEADS):
        diff = cs[:, off + h:off + h + 1] - cs_t[off + h:off + h + 1, :]
        decay = jnp.where(keep, jnp.exp(jnp.where(keep, diff, 0.0)), 0.0)
        sc = (g_mats[h // 2] * decay).astype(BF16)
        yd = _dot(sc, xd_b)
        y = yd if y is None else jnp.where(head == h, yd, y)
    y = y + _dot(cmat.astype(BF16), carry.astype(BF16)) * _expand_heads(e_cs, off)
    st = _dot(b_t, (xd * _expand_heads(ds, off)).astype(BF16))
    row_grp = lax.broadcasted_iota(jnp.int32, (LANES, 1), 0) >> 6
    st = jnp.where(row_grp == (head >> 1), st, 0.0)
    new_carry = carry * _expand_heads(jnp.exp(tot), off) + st
    return y, new_carry


def _ssd_kernel(xf_ref, dtf_ref, xb_ref, dtb_ref, alog_ref, dskip_ref, yf_ref, yb_ref, carry_ref):
    @pl.when(pl.program_id(1) == 0)
    def _():
        carry_ref[...] = jnp.zeros_like(carry_ref)

    nch = xf_ref.shape[0] // SSD_CHUNK
    a_row = -jnp.exp(alog_ref[...])
    dskip = dskip_ref[...]

    def body(ci, _):
        rf = pl.ds(pl.multiple_of(ci * SSD_CHUNK, SSD_CHUNK), SSD_CHUNK)
        xc = xf_ref[rf, :]
        y, cf = _ssd_chunk(xc, dtf_ref[rf, :], a_row, carry_ref[0], 0, False)
        carry_ref[0] = cf
        yf_ref[rf, :] = y + xc[:, 0:256] * dskip
        rb = pl.ds(pl.multiple_of((nch - 1 - ci) * SSD_CHUNK, SSD_CHUNK), SSD_CHUNK)
        y, cb = _ssd_chunk(xb_ref[rb, :], dtb_ref[rb, :], a_row, carry_ref[1], SSD_HEADS, True)
        carry_ref[1] = cb
        yb_ref[rb, :] = y
        return 0

    lax.fori_loop(0, nch, body, 0)


def _ssd(xc, dt, alog_row, dskip_row, b, s):
    tt = min(512, s)
    nt = s // tt
    dt3 = dt.reshape(b, s, LANES)
    fwd = lambda w: pl.BlockSpec((None, tt, w), lambda bi, i: (bi, i, 0))
    bwd = lambda w: pl.BlockSpec((None, tt, w), lambda bi, i: (bi, nt - 1 - i, 0))
    full = lambda a: pl.BlockSpec(a.shape, lambda bi, i: (0,) * a.ndim)
    yf, yb = pl.pallas_call(
        _ssd_kernel,
        grid=(b, nt),
        in_specs=[fwd(512), fwd(LANES), bwd(512), bwd(LANES), full(alog_row), full(dskip_row)],
        out_specs=[fwd(256), bwd(256)],
        out_shape=[jax.ShapeDtypeStruct((b, s, 256), F32)] * 2,
        scratch_shapes=[pltpu.VMEM((2, LANES, 256), F32)],
        compiler_params=_cparams(("parallel", "arbitrary")),
        name="ssd_scan",
    )(xc, dt3, xc, dt3, alog_row, dskip_row)
    return yf.reshape(b * s, 256), yb.reshape(b * s, 256)


def _out_proj_kernel(x_ref, att_ref, fno_ref, pol_ref, yf_ref, yb_ref, z_ref, w_ref, ng_ref,
                     g_ref, b_ref, o_ref):
    y = (yf_ref[...] + yb_ref[...]) * _silu(z_ref[...])
    ng = ng_ref[...]
    parts = []
    for gi in range(2):
        yg = y[:, gi * LANES:(gi + 1) * LANES]
        ms = jnp.mean(yg * yg, axis=-1, keepdims=True)
        parts.append((yg * lax.rsqrt(ms + RMS_EPS) * ng[:, gi * LANES:(gi + 1) * LANES]).astype(BF16))
    mix = _dot(att_ref[...], w_ref[0:256, :]) + _dot(fno_ref[...], w_ref[256:512, :])
    mix = mix + _dot(pol_ref[...], w_ref[512:768, :])
    mix = mix + _dot(parts[0], w_ref[768:896, :]) + _dot(parts[1], w_ref[896:1024, :])
    o_ref[...] = _layer_norm(DEEPNORM_ALPHA * x_ref[...] + mix, g_ref[...], b_ref[...])


def _out_proj(x, att, fno, pol, yf, yb, z, w, ng, g, bb):
    t = x.shape[0]
    tm = min(512, t)
    row = lambda wd: pl.BlockSpec((tm, wd), lambda i: (i, 0))
    full = lambda a: pl.BlockSpec(a.shape, lambda i: (0,) * a.ndim)
    return pl.pallas_call(
        _out_proj_kernel,
        grid=(t // tm,),
        in_specs=[row(D_MODEL)] + [row(256)] * 6 + [full(w), full(ng), full(g), full(bb)],
        out_specs=row(D_MODEL),
        out_shape=jax.ShapeDtypeStruct((t, D_MODEL), F32),
        compiler_params=_cparams(("parallel",)),
        name="out_proj_ln",
    )(x, att, fno, pol, yf, yb, z, w, ng, g, bb)


MOE_TILE = 1024
MOE_CHUNK = 128
MOE_PERM_ROWS = 256


def _router(x, w2, br):
    xh = x.astype(BF16)
    xl = (x - xh.astype(F32)).astype(BF16)
    l1 = _dot(xh, w2)
    logits = l1[:, 0:LANES] + l1[:, LANES:2 * LANES] + _dot(xl, w2[:, 0:LANES]) + br
    lane = lax.broadcasted_iota(jnp.int32, (1, LANES), 1)
    ninf = -jnp.inf
    gl = jnp.where(lane < N_GROUPS, logits, ninf)
    gmax = jnp.max(gl, axis=-1, keepdims=True)
    g_p = 1.0 / jnp.sum(jnp.exp(gl - gmax), axis=-1, keepdims=True)
    g_idx = jnp.min(jnp.where(gl == gmax, lane, LANES), axis=-1, keepdims=True)
    e_lane = lane - N_GROUPS
    in_grp = (e_lane >= 0) & (e_lane < N_EXPERTS) & ((e_lane >> 2) == g_idx)
    el = jnp.where(in_grp, logits, ninf)
    m1 = jnp.max(el, axis=-1, keepdims=True)
    i1 = jnp.min(jnp.where(el == m1, lane, LANES), axis=-1, keepdims=True)
    el2 = jnp.where(lane == i1, ninf, el)
    m2 = jnp.max(el2, axis=-1, keepdims=True)
    i2 = jnp.min(jnp.where(el2 == m2, lane, LANES), axis=-1, keepdims=True)
    e2 = jnp.exp(m2 - m1)
    w1 = 1.0 / (1.0 + e2)
    gates = g_p * (jnp.where(lane == i1, w1, 0.0) + jnp.where(lane == i2, e2 * w1, 0.0))
    return gates, g_idx


def _moe_kernel(x_ref, wr_ref, br_ref, tri_ref, wg_ref, wu_ref, wd_ref, g_ref, b_ref, o_ref,
                xs_ref, gs_ref, pos_ref, acc_ref, off_ref):
    grp = pl.program_id(1)
    n = x_ref.shape[0]
    lane = lax.broadcasted_iota(jnp.int32, (1, LANES), 1)

    @pl.when(grp == 0)
    def _route_and_sort():
        x = x_ref[...]
        gates, g_idx = _router(x, wr_ref[...], br_ref[...])
        onehot = lane == g_idx
        csum = _dot(tri_ref[...], jnp.where(onehot, 1.0, 0.0).astype(BF16))
        cnt = csum[n - 1:n, :].astype(jnp.int32)
        c0, c1, c2 = cnt[0, 0], cnt[0, 1], cnt[0, 2]
        off_ref[0] = 0
        off_ref[1] = c0
        off_ref[2] = c0 + c1
        off_ref[3] = c0 + c1 + c2
        off_ref[4] = n
        offv = jnp.where(lane == 1, c0, jnp.where(lane == 2, c0 + c1, jnp.where(lane == 3, c0 + c1 + c2, 0)))
        pos = jnp.sum(jnp.where(onehot, csum + offv.astype(F32), 0.0), axis=-1, keepdims=True) - 1.0
        pos_ref[...] = jnp.broadcast_to(pos, (n, LANES))
        pos_row = pos_ref[...].T[0:1, :].astype(jnp.int32)
        xb = x.astype(BF16)
        g_hi = gates.astype(BF16)
        g2 = jnp.concatenate([g_hi, (gates - g_hi.astype(F32)).astype(BF16)], axis=1)
        for r in range(n // MOE_PERM_ROWS):
            rows = lax.broadcasted_iota(jnp.int32, (MOE_PERM_ROWS, n), 0) + r * MOE_PERM_ROWS
            perm = jnp.where(rows == pos_row, 1.0, 0.0).astype(BF16)
            sl = slice(r * MOE_PERM_ROWS, (r + 1) * MOE_PERM_ROWS)
            xs_ref[sl, :] = _dot(perm, xb).astype(BF16)
            gg = _dot(perm, g2)
            gs_ref[sl, :] = gg[:, 0:LANES] + gg[:, LANES:2 * LANES]
        acc_ref[...] = jnp.zeros_like(acc_ref)

    lo = off_ref[grp]
    hi = off_ref[grp + 1]

    def chunk(c, carry):
        r0 = pl.multiple_of(c * MOE_CHUNK, MOE_CHUNK)

        @pl.when((lo < r0 + MOE_CHUNK) & (hi > r0))
        def _():
            rs = pl.ds(r0, MOE_CHUNK)
            xs = xs_ref[rs, :]
            gsc = gs_ref[rs, :]
            hid = _silu(_dot(xs, wg_ref[...])) * _dot(xs, wu_ref[...])
            parts = []
            for e in range(EXPERTS_PER_GROUP):
                ge = jnp.sum(jnp.where(lane == N_GROUPS + EXPERTS_PER_GROUP * grp + e, gsc, 0.0),
                             axis=-1, keepdims=True)
                parts.append((hid[:, e * EXPERT_FF:(e + 1) * EXPERT_FF] * ge).astype(BF16))
            acc_ref[rs, :] += _dot(jnp.concatenate(parts, axis=1), wd_ref[...])

        return carry

    lax.fori_loop(0, n // MOE_CHUNK, chunk, 0)

    @pl.when(grp == N_GROUPS - 1)
    def _unsort_and_norm():
        xs_ref[...] = acc_ref[...].astype(BF16)
        cols = lax.broadcasted_iota(jnp.int32, (MOE_PERM_ROWS, n), 1)
        for r in range(n // MOE_PERM_ROWS):
            sl = slice(r * MOE_PERM_ROWS, (r + 1) * MOE_PERM_ROWS)
            perm_t = jnp.where(cols == pos_ref[sl, 0:1].astype(jnp.int32), 1.0, 0.0).astype(BF16)
            y = _dot(perm_t, xs_ref[...])
            o_ref[sl, :] = _layer_norm(DEEPNORM_ALPHA * x_ref[sl, :] + y, g_ref[...], b_ref[...])


def _moe(x, wr, br, wg, wu, wd, g, bb):
    t = x.shape[0]
    n = min(MOE_TILE, t)
    tri = jnp.asarray(np.tril(np.ones((n, n), np.float32)), dtype=BF16)
    full = lambda a: pl.BlockSpec(a.shape, lambda i, e: (0,) * a.ndim)
    wspec = pl.BlockSpec((None, D_MODEL, D_MODEL), lambda i, e: (e, 0, 0))
    return pl.pallas_call(
        _moe_kernel,
        grid=(t // n, N_GROUPS),
        in_specs=[pl.BlockSpec((n, D_MODEL), lambda i, e: (i, 0)), full(wr), full(br), full(tri),
                  wspec, wspec, wspec, full(g), full(bb)],
        out_specs=pl.BlockSpec((n, D_MODEL), lambda i, e: (i, 0)),
        out_shape=jax.ShapeDtypeStruct((t, D_MODEL), F32),
        scratch_shapes=[pltpu.VMEM((n, D_MODEL), BF16), pltpu.VMEM((n, LANES), F32),
                        pltpu.VMEM((n, LANES), F32), pltpu.VMEM((n, D_MODEL), F32),
                        pltpu.SMEM((8,), jnp.int32)],
        compiler_params=_cparams(("parallel", "arbitrary")),
        name="moe_ln",
    )(x, wr, br, tri, wg, wu, wd, g, bb)


def _rope_tables(seq):
    rows = seq // GRID_W
    row = jnp.repeat(jnp.arange(rows, dtype=F32), GRID_W)
    col = jnp.tile(jnp.arange(GRID_W, dtype=F32), rows)
    half = HEAD_DIM // 2
    freqs = 1.0 / (ROPE_THETA ** (jnp.arange(0, half, 2, dtype=F32) / half))
    ar, ac = row[:, None] * freqs, col[:, None] * freqs
    cosv = jnp.concatenate([jnp.cos(ar), jnp.cos(ar), jnp.cos(ac), jnp.cos(ac)], axis=-1)
    sinv = jnp.concatenate([-jnp.sin(ar), jnp.sin(ar), -jnp.sin(ac), jnp.sin(ac)], axis=-1)
    return jnp.tile(cosv, (1, 2)), jnp.tile(sinv, (1, 2))


def _position_dft(seq):
    blk = 64
    k = jnp.arange(seq, dtype=jnp.int32)
    ang = lambda j: (2.0 * math.pi / seq) * ((j[:, None] * k[None, :]) % seq).astype(F32)
    a_hi = ang(jnp.arange(0, seq, blk, dtype=jnp.int32))
    a_lo = ang(jnp.arange(blk, dtype=jnp.int32))
    ch, sh, cl, sl = jnp.cos(a_hi), jnp.sin(a_hi), jnp.cos(a_lo), jnp.sin(a_lo)
    sc = seq ** -0.5
    cmat = (ch[:, None, :] * cl[None] - sh[:, None, :] * sl[None]) * sc
    smat = (sh[:, None, :] * cl[None] + ch[:, None, :] * sl[None]) * (-sc)
    return cmat.reshape(seq, seq).astype(BF16), smat.reshape(seq, seq).astype(BF16)


def _channel_dft():
    n = np.arange(HEAD_DIM)
    ang = 2.0 * np.pi * ((n[:, None] * n[None, :]) % HEAD_DIM) / HEAD_DIM
    eye = np.eye(4)
    cb = np.kron(eye, np.cos(ang)) * HEAD_DIM ** -0.5
    sb = np.kron(eye, np.sin(ang)) * HEAD_DIM ** -0.5
    return jnp.asarray(np.concatenate([cb, sb], axis=1), dtype=BF16)


def _layer_params(l, w_in, q_norm_g, k_norm_g, w_fnet, w_pool, pool_scale, conv_w, conv_b, dt_bias,
                  a_log, d_skip, ssd_norm_g, w_out, ln1_g, ln1_b, w_group, b_group, w_router, b_router,
                  w_gate, w_up, w_down, ln2_g, ln2_b):
    wi = w_in[l]
    hperm = np.array([0, 2, 1, 3])
    wq = wi[:, 0:256].reshape(D_MODEL, 4, HEAD_DIM)[:, hperm].reshape(D_MODEL, 256)
    w_proj = jnp.concatenate([wq, wi[:, 256:1792], jnp.pad(wi[:, 1792:1800], ((0, 0), (0, 120)))],
                             axis=1).astype(BF16)
    wo = w_out[l]
    wo_att = wo[0:256].reshape(4, HEAD_DIM, D_MODEL)[hperm].reshape(256, D_MODEL)
    w_o = jnp.concatenate([wo_att, wo[256:]], axis=0).astype(BF16)
    row = lambda v, n: jnp.pad(v.reshape(1, -1), ((0, 0), (0, n - v.size)))
    w_pool_bd = jnp.zeros((256, 256), F32)
    for gi in range(4):
        w_pool_bd = w_pool_bd.at[gi * 64:(gi + 1) * 64, gi * 64:(gi + 1) * 64].set(w_pool[l, gi])
    wr = jnp.pad(jnp.concatenate([w_group[l], w_router[l]], axis=1), ((0, 0), (0, LANES - 20)))
    wr_hi = wr.astype(BF16)
    by_group = lambda w: w.reshape(N_GROUPS, EXPERTS_PER_GROUP, D_MODEL, EXPERT_FF).transpose(
        0, 2, 1, 3).reshape(N_GROUPS, D_MODEL, EXPERTS_PER_GROUP * EXPERT_FF).astype(BF16)
    return dict(
        w_proj=w_proj,
        qg=jnp.tile(q_norm_g[l], 2).reshape(1, LANES), kg=jnp.tile(k_norm_g[l], 2).reshape(1, LANES),
        dtb=row(dt_bias[l], LANES), w_fnet=w_fnet[l].astype(BF16),
        w_pool=w_pool_bd.astype(BF16), pool_scale=pool_scale[l].reshape(1, 256),
        conv_w=conv_w[l], conv_b=conv_b[l].reshape(1, 512),
        alog=row(a_log[l], LANES), dskip=jnp.repeat(d_skip[l], 64).reshape(1, 256),
        ssd_ng=ssd_norm_g[l].reshape(1, 256), w_o=w_o,
        ln1_g=ln1_g[l].reshape(1, D_MODEL), ln1_b=ln1_b[l].reshape(1, D_MODEL),
        wr=jnp.concatenate([wr_hi, (wr - wr_hi.astype(F32)).astype(BF16)], axis=1),
        br=row(jnp.concatenate([b_group[l], b_router[l]]), LANES),
        wg=by_group(w_gate[l]), wu=by_group(w_up[l]),
        wd=w_down[l].reshape(N_GROUPS, EXPERTS_PER_GROUP * EXPERT_FF, D_MODEL).astype(BF16),
        ln2_g=ln2_g[l].reshape(1, D_MODEL), ln2_b=ln2_b[l].reshape(1, D_MODEL),
    )


def _trunk(x3, params, tables):
    b, s, _ = x3.shape
    x = x3.reshape(b * s, D_MODEL)
    cosv, sinv, cmat, smat, bd, dftc = tables
    for p in params:
        q, k, v, ab, up, z, xbc, dt = _in_proj(x, p["w_proj"], p["qg"], p["kg"], cosv, sinv, bd, dftc,
                                               p["dtb"], s)
        att = _attention(q, k, v, b, s)
        fno = _fourier(cmat, smat, ab, p["w_fnet"], b, s)
        pol = _pool(up, p["w_pool"], p["pool_scale"], b, s)
        xc = _conv(xbc, p["conv_w"], p["conv_b"], b, s)
        yf, yb = _ssd(xc, dt, p["alog"], p["dskip"], b, s)
        x = _out_proj(x, att, fno, pol, yf, yb, z, p["w_o"], p["ssd_ng"], p["ln1_g"], p["ln1_b"])
        x = _moe(x, p["wr"], p["br"], p["wg"], p["wu"], p["wd"], p["ln2_g"], p["ln2_b"])
    return x.reshape(b, s, D_MODEL)


def _tables(seq):
    cosv, sinv = _rope_tables(seq)
    cmat, smat = _position_dft(seq)
    lane = np.arange(LANES)
    bd = jnp.asarray((lane[:, None] // HEAD_DIM) == (lane[None, :] // HEAD_DIM), dtype=BF16)
    return cosv, sinv, cmat, smat, bd, _channel_dft()


def kernel(x_prompt, x_sample, w_in, q_norm_g, k_norm_g, w_fnet, w_pool, pool_scale, conv_w, conv_b, dt_bias, a_log, d_skip, ssd_norm_g, w_out, ln1_g, ln1_b, w_group, b_group, w_router, b_router, w_gate, w_up, w_down, ln2_g, ln2_b):
    weights = (w_in, q_norm_g, k_norm_g, w_fnet, w_pool, pool_scale, conv_w, conv_b, dt_bias, a_log,
               d_skip, ssd_norm_g, w_out, ln1_g, ln1_b, w_group, b_group, w_router, b_router,
               w_gate, w_up, w_down, ln2_g, ln2_b)
    params = [_layer_params(l, *weights) for l in range(w_in.shape[0])]
    y_prompt = _trunk(x_prompt, params, _tables(x_prompt.shape[1]))
    y_sample = _trunk(x_sample, params, _tables(x_sample.shape[1]))
    return (y_prompt, y_sample)
```

```python
import functools
import math

import numpy as np
import jax
import jax.numpy as jnp
from jax import lax
from jax.experimental import pallas as pl
from jax.experimental.pallas import tpu as pltpu

F32 = jnp.float32
BF16 = jnp.bfloat16

D_MODEL = 1024
DEPTH = 2
GRID_W = 64
HEAD_DIM = 64
ROPE_THETA = 10000.0
ATT_Q_HEADS = 4
ATT_KV_HEADS = 2
POOL_WINDOWS = (2, 4, 8, 16)
SSD_HEADS = 4
SSD_CONV = 4
SSD_CHUNK = 128
N_GROUPS = 4
EXPERTS_PER_GROUP = 4
N_EXPERTS = 16
EXPERT_FF = 256
DEEPNORM_ALPHA = (2 * DEPTH) ** 0.25
LN_EPS = 1e-5
RMS_EPS = 1e-6

LANES = 128
SUBLANES = 8
HALO = SUBLANES
VMEM_LIMIT = 56 * 1024 * 1024

C_Q, C_K, C_V, C_UF, C_UP, C_Z, C_XBC, C_DT, N_PROJ = 0, 256, 384, 512, 768, 1024, 1280, 1792, 1920
HI = lax.Precision.HIGHEST


def _cparams(sem):
    return pltpu.CompilerParams(dimension_semantics=sem, vmem_limit_bytes=VMEM_LIMIT)


def _dot(a, b):
    return jnp.dot(a, b, preferred_element_type=F32)


def _dot_nt(a, b):
    return lax.dot_general(a, b, (((1,), (1,)), ((), ())), preferred_element_type=F32)


def _silu(x):
    return x * (1.0 / (1.0 + jnp.exp(-x)))


def _layer_norm(x, g, b):
    mu = jnp.mean(x, axis=-1, keepdims=True)
    xc = x - mu
    var = jnp.mean(xc * xc, axis=-1, keepdims=True)
    return xc * lax.rsqrt(var + LN_EPS) * g + b


def _head_sumsq(x, ones_bd):
    sq = x * x
    hi = sq.astype(BF16)
    lo = (sq - hi.astype(F32)).astype(BF16)
    return _dot(hi, ones_bd) + _dot(lo, ones_bd)


def _rope(x, cosv, sinv, first_half):
    w = x.shape[-1]
    partner = jnp.where(first_half, pltpu.roll(x, w - 16, axis=1), pltpu.roll(x, 16, axis=1))
    return x * cosv + partner * sinv


def _in_proj_kernel(x_ref, w_ref, qg_ref, kg_ref, cos_ref, sin_ref, bd_ref, dftc_ref, dtb_ref,
                    q_ref, k_ref, v_ref, ab_ref, up_ref, z_ref, xbc_ref, dt_ref):
    xb = x_ref[...].astype(BF16)
    h = _dot(xb, w_ref[...])
    cosv = cos_ref[...]
    sinv = sin_ref[...]
    bd = bd_ref[...]
    lane = lax.broadcasted_iota(jnp.int32, (1, LANES), 1)
    first_half = (lane & 31) < 16
    for c in range(2):
        qc = h[:, C_Q + c * LANES:C_Q + (c + 1) * LANES]
        ss = _head_sumsq(qc, bd)
        qn = qc * lax.rsqrt(ss * (1.0 / HEAD_DIM) + RMS_EPS) * qg_ref[...]
        qr = _rope(qn, cosv, sinv, first_half) * (HEAD_DIM ** -0.5 * math.log2(math.e))
        q_ref[:, c * LANES:(c + 1) * LANES] = qr.astype(BF16)
    kc = h[:, C_K:C_K + LANES]
    ss = _head_sumsq(kc, bd)
    kn = kc * lax.rsqrt(ss * (1.0 / HEAD_DIM) + RMS_EPS) * kg_ref[...]
    k_ref[...] = _rope(kn, cosv, sinv, first_half).astype(BF16)
    v_ref[...] = h[:, C_V:C_V + LANES].astype(BF16)
    uf = h[:, C_UF:C_UF + 256].astype(BF16)
    ab_ref[...] = _dot(uf, dftc_ref[...]).astype(BF16)
    up_ref[...] = h[:, C_UP:C_UP + 256]
    z_ref[...] = h[:, C_Z:C_Z + 256]
    xbc_ref[...] = h[:, C_XBC:C_XBC + 512]
    dr = h[:, C_DT:C_DT + LANES] + dtb_ref[...]
    dt_ref[...] = jnp.maximum(dr, 0.0) + jnp.log(1.0 + jnp.exp(-jnp.abs(dr)))


def _in_proj(x, w, qg, kg, cosv, sinv, bd, dftc, dtb, seq):
    t = x.shape[0]
    tm = min(512, seq)
    nseq = seq // tm
    full = lambda a: pl.BlockSpec(a.shape, lambda i: (0,) * a.ndim)
    row = lambda wdt: pl.BlockSpec((tm, wdt), lambda i: (i, 0))
    pos = pl.BlockSpec((tm, LANES), lambda i: (i % nseq, 0))
    outs = [(256, BF16), (128, BF16), (128, BF16), (512, BF16), (256, F32), (256, F32), (512, F32),
            (128, F32)]
    return pl.pallas_call(
        _in_proj_kernel,
        grid=(t // tm,),
        in_specs=[row(D_MODEL), full(w), full(qg), full(kg), pos, pos, full(bd), full(dftc), full(dtb)],
        out_specs=[row(wd) for wd, _ in outs],
        out_shape=[jax.ShapeDtypeStruct((t, wd), dt) for wd, dt in outs],
        compiler_params=_cparams(("parallel",)),
        name="in_proj",
    )(x, w, qg, kg, cosv, sinv, bd, dftc, dtb)


ATT_KEY_CHUNK = 256


def _attn_kernel(q_ref, k_ref, v_ref, o_ref, vt_ref):
    nch, _, vrows, tk = vt_ref.shape
    tq = q_ref.shape[0]

    @pl.when(pl.program_id(1) == 0)
    def _():
        for c in range(nch):
            vt = v_ref[c * tk:(c + 1) * tk, :].astype(F32).T.astype(BF16)
            for j in range(ATT_KV_HEADS):
                vt_ref[c, j, 0:HEAD_DIM, :] = vt[j * HEAD_DIM:(j + 1) * HEAD_DIM, :]
                vt_ref[c, j, HEAD_DIM:vrows, :] = jnp.ones((vrows - HEAD_DIM, tk), BF16)

    lane = lax.broadcasted_iota(jnp.int32, (1, LANES), 1)
    qms = []
    for c in range(2):
        qc = q_ref[:, c * LANES:(c + 1) * LANES]
        for j in range(ATT_KV_HEADS):
            qms.append(jnp.where((lane >> 6) == j, qc, jnp.zeros_like(qc)))

    def scores(c):
        kc = k_ref[c * tk:(c + 1) * tk, :]
        return [_dot_nt(kc, qm) for qm in qms]

    ms = [jnp.full((1, tq), -jnp.inf, F32)] * ATT_Q_HEADS
    accs = [jnp.zeros((vrows, tq), F32)] * ATT_Q_HEADS
    st_next = scores(0)
    for c in range(nch):
        sts = st_next
        if c + 1 < nch:
            st_next = scores(c + 1)
        for h in range(ATT_Q_HEADS):
            m_new = jnp.maximum(ms[h], jnp.max(sts[h], axis=0, keepdims=True))
            alpha = jnp.exp2(ms[h] - m_new)
            p = jnp.exp2(sts[h] - m_new).astype(BF16)
            ms[h] = m_new
            accs[h] = alpha * accs[h] + _dot(vt_ref[c, h % ATT_KV_HEADS], p)
    for c in range(2):
        ot = jnp.concatenate([accs[2 * c + j][0:HEAD_DIM] / accs[2 * c + j][HEAD_DIM:HEAD_DIM + 1]
                              for j in range(ATT_KV_HEADS)], axis=0)
        o_ref[:, c * LANES:(c + 1) * LANES] = ot.T.astype(BF16)


def _attention(q, k, v, b, s):
    tq = min(256, s)
    tk = min(ATT_KEY_CHUNK, s)
    q3, k3, v3 = q.reshape(b, s, 256), k.reshape(b, s, LANES), v.reshape(b, s, LANES)
    out = pl.pallas_call(
        _attn_kernel,
        grid=(b, s // tq),
        in_specs=[pl.BlockSpec((None, tq, 256), lambda bi, i: (bi, i, 0)),
                  pl.BlockSpec((None, s, LANES), lambda bi, i: (bi, 0, 0)),
                  pl.BlockSpec((None, s, LANES), lambda bi, i: (bi, 0, 0))],
        out_specs=pl.BlockSpec((None, tq, 256), lambda bi, i: (bi, i, 0)),
        out_shape=jax.ShapeDtypeStruct((b, s, 256), BF16),
        scratch_shapes=[pltpu.VMEM((s // tk, ATT_KV_HEADS, HEAD_DIM + 16, tk), BF16)],
        compiler_params=_cparams(("parallel", "arbitrary")),
        name="attention",
    )(q3, k3, v3)
    return out.reshape(b * s, 256)


def _fourier_kernel(c_ref, s_ref, ab_ref, w_ref, o_ref):
    acc = _dot(c_ref[...], ab_ref[:, 0:256]) + _dot(s_ref[...], ab_ref[:, 256:512])
    o_ref[...] = _dot(acc.astype(BF16), w_ref[...]).astype(BF16)


def _fourier(cmat, smat, ab, w_fnet, b, s):
    tm = min(512, s) if s <= 4096 else 256
    ab3 = ab.reshape(b, s, 512)
    out = pl.pallas_call(
        _fourier_kernel,
        grid=(s // tm, b),
        in_specs=[pl.BlockSpec((tm, s), lambda i, bi: (i, 0)),
                  pl.BlockSpec((tm, s), lambda i, bi: (i, 0)),
                  pl.BlockSpec((None, s, 512), lambda i, bi: (bi, 0, 0)),
                  pl.BlockSpec((256, 256), lambda i, bi: (0, 0))],
        out_specs=pl.BlockSpec((None, tm, 256), lambda i, bi: (bi, i, 0)),
        out_shape=jax.ShapeDtypeStruct((b, s, 256), BF16),
        compiler_params=_cparams(("parallel", "parallel")),
        name="fourier",
    )(cmat, smat, ab3, w_fnet)
    return out.reshape(b * s, 256)


def _halo_specs(tt, width, s):
    nb = tt // HALO
    last = s // HALO - 1
    main = pl.BlockSpec((None, tt, width), lambda bi, i: (bi, i, 0))
    prev = pl.BlockSpec((None, HALO, width), lambda bi, i: (bi, jnp.maximum(i * nb - 1, 0), 0))
    nxt = pl.BlockSpec((None, HALO, width), lambda bi, i: (bi, jnp.minimum((i + 1) * nb, last), 0))
    return [main, prev, nxt]


def _with_halo(main_ref, prev_ref, next_ref):
    i = pl.program_id(1)
    n = pl.num_programs(1)
    prev = jnp.where(i > 0, prev_ref[...], 0.0)
    nxt = jnp.where(i < n - 1, next_ref[...], 0.0)
    return jnp.concatenate([prev, main_ref[...], nxt], axis=0)


def _shifted(ext, d, tt):
    n = ext.shape[0]
    r = ext if d == 0 else pltpu.roll(ext, (-d) % n, axis=0)
    return r[HALO:HALO + tt]


def _pool_kernel(u_ref, up_ref, un_ref, w_ref, sc_ref, o_ref, *, seq):
    tt = u_ref.shape[0]
    ext = _with_halo(u_ref, up_ref, un_ref)
    n = ext.shape[0]
    sh = lambda a, d: a if d == 0 else pltpu.roll(a, (-d) % n, axis=0)
    p2 = sh(ext, -1) + ext
    p4 = sh(p2, -1) + sh(p2, 1)
    p8 = sh(p4, -2) + sh(p4, 2)
    p16 = sh(p8, -4) + sh(p8, 4)
    t = pl.program_id(1) * tt + lax.broadcasted_iota(jnp.int32, (tt, 1), 0)
    grp = lax.broadcasted_iota(jnp.int32, (1, 256), 1) >> 6
    u = u_ref[...]
    pooled = None
    for gi, (w, pw) in enumerate(zip(POOL_WINDOWS, (p2, p4, p8, p16))):
        lo = jnp.clip(t - w // 2, 0, seq - 1)
        hi = jnp.clip(t + (w - w // 2) - 1, 0, seq - 1)
        cnt = (hi - lo + 1).astype(F32)
        val = pw[HALO:HALO + tt] / cnt - u
        pooled = val if pooled is None else jnp.where(grp == gi, val, pooled)
    o_ref[...] = (_dot(pooled.astype(BF16), w_ref[...]) * sc_ref[...]).astype(BF16)


def _pool(u, w_bd, scale, b, s):
    tt = min(1024, s)
    u3 = u.reshape(b, s, 256)
    full = lambda a: pl.BlockSpec(a.shape, lambda bi, i: (0,) * a.ndim)
    out = pl.pallas_call(
        functools.partial(_pool_kernel, seq=s),
        grid=(b, s // tt),
        in_specs=_halo_specs(tt, 256, s) + [full(w_bd), full(scale)],
        out_specs=pl.BlockSpec((None, tt, 256), lambda bi, i: (bi, i, 0)),
        out_shape=jax.ShapeDtypeStruct((b, s, 256), BF16),
        compiler_params=_cparams(("parallel", "parallel")),
        name="pool",
    )(u3, u3, u3, w_bd, scale)
    return out.reshape(b * s, 256)


def _conv_kernel(x_ref, xp_ref, xn_ref, w_ref, b_ref, o_ref):
    tt = x_ref.shape[0]
    ext = _with_halo(x_ref, xp_ref, xn_ref)
    acc = b_ref[...] + _shifted(ext, -2, tt) * w_ref[0:1, :]
    for kk in range(1, SSD_CONV):
        acc = acc + _shifted(ext, kk - 2, tt) * w_ref[kk:kk + 1, :]
    o_ref[...] = _silu(acc)


def _conv(xbc, conv_w, conv_b, b, s):
    tt = min(1024, s)
    x3 = xbc.reshape(b, s, 512)
    full = lambda a: pl.BlockSpec(a.shape, lambda bi, i: (0,) * a.ndim)
    return pl.pallas_call(
        _conv_kernel,
        grid=(b, s // tt),
        in_specs=_halo_specs(tt, 512, s) + [full(conv_w), full(conv_b)],
        out_specs=pl.BlockSpec((None, tt, 512), lambda bi, i: (bi, i, 0)),
        out_shape=jax.ShapeDtypeStruct((b, s, 512), F32),
        compiler_params=_cparams(("parallel", "parallel")),
        name="ssd_conv",
    )(x3, x3, x3, conv_w, conv_b)


def _expand_heads(v, off):
    head = lax.broadcasted_iota(jnp.int32, (1, 256), 1) >> 6
    out = v[:, off + 3:off + 4]
    for h in (2, 1, 0):
        out = jnp.where(head == h, v[:, off + h:off + h + 1], out)
    return out


def _ssd_chunk(xc, dtc, a_row, carry, off, backward):
    L = SSD_CHUNK
    xs = xc[:, 0:256]
    bmat = xc[:, 256:384]
    cmat = xc[:, 384:512]
    r = lax.broadcasted_iota(jnp.int32, (L, L), 0)
    c = lax.broadcasted_iota(jnp.int32, (L, L), 1)
    keep = (c >= r) if backward else (c <= r)
    tri = keep.astype(F32)
    adt = dtc * a_row
    cs = jnp.dot(tri, adt, precision=HI, preferred_element_type=F32)
    cs_t = cs.T
    tot = cs[0:1, :] if backward else cs[L - 1:L, :]
    e_cs = jnp.exp(cs)
    ds = jnp.exp(tot - cs)
    xd = xs * _expand_heads(dtc, off)
    xd_b = xd.astype(BF16)
    b_t = bmat.T.astype(BF16)
    grp = lax.broadcasted_iota(jnp.int32, (1, LANES), 1) >> 6
    head = lax.broadcasted_iota(jnp.int32, (1, 256), 1) >> 6
    g_mats = [_dot(jnp.where(grp == g, cmat, 0.0).astype(BF16), b_t) for g in range(2)]
    y = None
    for h in range(SSD_HEADS):
        diff = cs[:, off + h:off + h + 1] - cs_t[off + h:off + h + 1, :]
        decay = jnp.where(keep, jnp.exp(jnp.where(keep, diff, 0.0)), 0.0)
        sc = (g_mats[h // 2] * decay).astype(BF16)
        yd = _dot(sc, xd_b)
        y = yd if y is None else jnp.where(head == h, yd, y)
    y = y + _dot(cmat.astype(BF16), carry.astype(BF16)) * _expand_heads(e_cs, off)
    st = _dot(b_t, (xd * _expand_heads(ds, off)).astype(BF16))
    row_grp = lax.broadcasted_iota(jnp.int32, (LANES, 1), 0) >> 6
    st = jnp.where(row_grp == (head >> 1), st, 0.0)
    new_carry = carry * _expand_heads(jnp.exp(tot), off) + st
    return y, new_carry


def _ssd_kernel(xf_ref, dtf_ref, xb_ref, dtb_ref, alog_ref, dskip_ref, yf_ref, yb_ref, carry_ref):
    @pl.when(pl.program_id(1) == 0)
    def _():
        carry_ref[...] = jnp.zeros_like(carry_ref)

    nch = xf_ref.shape[0] // SSD_CHUNK
    a_row = -jnp.exp(alog_ref[...])
    dskip = dskip_ref[...]

    def body(ci, _):
        rf = pl.ds(pl.multiple_of(ci * SSD_CHUNK, SSD_CHUNK), SSD_CHUNK)
        xc = xf_ref[rf, :]
        y, cf = _ssd_chunk(xc, dtf_ref[rf, :], a_row, carry_ref[0], 0, False)
        carry_ref[0] = cf
        yf_ref[rf, :] = y + xc[:, 0:256] * dskip
        rb = pl.ds(pl.multiple_of((nch - 1 - ci) * SSD_CHUNK, SSD_CHUNK), SSD_CHUNK)
        y, cb = _ssd_chunk(xb_ref[rb, :], dtb_ref[rb, :], a_row, carry_ref[1], SSD_HEADS, True)
        carry_ref[1] = cb
        yb_ref[rb, :] = y
        return 0

    lax.fori_loop(0, nch, body, 0)


def _ssd(xc, dt, alog_row, dskip_row, b, s):
    tt = min(512, s)
    nt = s // tt
    dt3 = dt.reshape(b, s, LANES)
    fwd = lambda w: pl.BlockSpec((None, tt, w), lambda bi, i: (bi, i, 0))
    bwd = lambda w: pl.BlockSpec((None, tt, w), lambda bi, i: (bi, nt - 1 - i, 0))
    full = lambda a: pl.BlockSpec(a.shape, lambda bi, i: (0,) * a.ndim)
    yf, yb = pl.pallas_call(
        _ssd_kernel,
        grid=(b, nt),
        in_specs=[fwd(512), fwd(LANES), bwd(512), bwd(LANES), full(alog_row), full(dskip_row)],
        out_specs=[fwd(256), bwd(256)],
        out_shape=[jax.ShapeDtypeStruct((b, s, 256), F32)] * 2,
        scratch_shapes=[pltpu.VMEM((2, LANES, 256), F32)],
        compiler_params=_cparams(("parallel", "arbitrary")),
        name="ssd_scan",
    )(xc, dt3, xc, dt3, alog_row, dskip_row)
    return yf.reshape(b * s, 256), yb.reshape(b * s, 256)


def _out_proj_kernel(x_ref, att_ref, fno_ref, pol_ref, yf_ref, yb_ref, z_ref, w_ref, ng_ref,
                     g_ref, b_ref, o_ref):
    y = (yf_ref[...] + yb_ref[...]) * _silu(z_ref[...])
    ng = ng_ref[...]
    parts = []
    for gi in range(2):
        yg = y[:, gi * LANES:(gi + 1) * LANES]
        ms = jnp.mean(yg * yg, axis=-1, keepdims=True)
        parts.append((yg * lax.rsqrt(ms + RMS_EPS) * ng[:, gi * LANES:(gi + 1) * LANES]).astype(BF16))
    mix = _dot(att_ref[...], w_ref[0:256, :]) + _dot(fno_ref[...], w_ref[256:512, :])
    mix = mix + _dot(pol_ref[...], w_ref[512:768, :])
    mix = mix + _dot(parts[0], w_ref[768:896, :]) + _dot(parts[1], w_ref[896:1024, :])
    o_ref[...] = _layer_norm(DEEPNORM_ALPHA * x_ref[...] + mix, g_ref[...], b_ref[...])


def _out_proj(x, att, fno, pol, yf, yb, z, w, ng, g, bb):
    t = x.shape[0]
    tm = min(512, t)
    row = lambda wd: pl.BlockSpec((tm, wd), lambda i: (i, 0))
    full = lambda a: pl.BlockSpec(a.shape, lambda i: (0,) * a.ndim)
    return pl.pallas_call(
        _out_proj_kernel,
        grid=(t // tm,),
        in_specs=[row(D_MODEL)] + [row(256)] * 6 + [full(w), full(ng), full(g), full(bb)],
        out_specs=row(D_MODEL),
        out_shape=jax.ShapeDtypeStruct((t, D_MODEL), F32),
        compiler_params=_cparams(("parallel",)),
        name="out_proj_ln",
    )(x, att, fno, pol, yf, yb, z, w, ng, g, bb)


MOE_TILE = 1024
MOE_CHUNK = 128
MOE_PERM_ROWS = 256


def _router(x, w2, br):
    xh = x.astype(BF16)
    xl = (x - xh.astype(F32)).astype(BF16)
    l1 = _dot(xh, w2)
    logits = l1[:, 0:LANES] + l1[:, LANES:2 * LANES] + _dot(xl, w2[:, 0:LANES]) + br
    lane = lax.broadcasted_iota(jnp.int32, (1, LANES), 1)
    ninf = -jnp.inf
    gl = jnp.where(lane < N_GROUPS, logits, ninf)
    gmax = jnp.max(gl, axis=-1, keepdims=True)
    g_p = 1.0 / jnp.sum(jnp.exp(gl - gmax), axis=-1, keepdims=True)
    g_idx = jnp.min(jnp.where(gl == gmax, lane, LANES), axis=-1, keepdims=True)
    e_lane = lane - N_GROUPS
    in_grp = (e_lane >= 0) & (e_lane < N_EXPERTS) & ((e_lane >> 2) == g_idx)
    el = jnp.where(in_grp, logits, ninf)
    m1 = jnp.max(el, axis=-1, keepdims=True)
    i1 = jnp.min(jnp.where(el == m1, lane, LANES), axis=-1, keepdims=True)
    el2 = jnp.where(lane == i1, ninf, el)
    m2 = jnp.max(el2, axis=-1, keepdims=True)
    i2 = jnp.min(jnp.where(el2 == m2, lane, LANES), axis=-1, keepdims=True)
    e2 = jnp.exp(m2 - m1)
    w1 = 1.0 / (1.0 + e2)
    gates = g_p * (jnp.where(lane == i1, w1, 0.0) + jnp.where(lane == i2, e2 * w1, 0.0))
    return gates, g_idx


def _moe_kernel(x_ref, wr_ref, br_ref, tri_ref, wg_ref, wu_ref, wd_ref, g_ref, b_ref, o_ref,
                xs_ref, gs_ref, pos_ref, acc_ref, off_ref):
    grp = pl.program_id(1)
    n = x_ref.shape[0]
    lane = lax.broadcasted_iota(jnp.int32, (1, LANES), 1)

    @pl.when(grp == 0)
    def _route_and_sort():
        x = x_ref[...]
        gates, g_idx = _router(x, wr_ref[...], br_ref[...])
        onehot = lane == g_idx
        csum = _dot(tri_ref[...], jnp.where(onehot, 1.0, 0.0).astype(BF16))
        cnt = csum[n - 1:n, :].astype(jnp.int32)
        c0, c1, c2 = cnt[0, 0], cnt[0, 1], cnt[0, 2]
        off_ref[0] = 0
        off_ref[1] = c0
        off_ref[2] = c0 + c1
        off_ref[3] = c0 + c1 + c2
        off_ref[4] = n
        offv = jnp.where(lane == 1, c0, jnp.where(lane == 2, c0 + c1, jnp.where(lane == 3, c0 + c1 + c2, 0)))
        pos = jnp.sum(jnp.where(onehot, csum + offv.astype(F32), 0.0), axis=-1, keepdims=True) - 1.0
        pos_ref[...] = jnp.broadcast_to(pos, (n, LANES))
        pos_row = pos_ref[...].T[0:1, :].astype(jnp.int32)
        xb = x.astype(BF16)
        g_hi = gates.astype(BF16)
        g2 = jnp.concatenate([g_hi, (gates - g_hi.astype(F32)).astype(BF16)], axis=1)
        for r in range(n // MOE_PERM_ROWS):
            rows = lax.broadcasted_iota(jnp.int32, (MOE_PERM_ROWS, n), 0) + r * MOE_PERM_ROWS
            perm = jnp.where(rows == pos_row, 1.0, 0.0).astype(BF16)
            sl = slice(r * MOE_PERM_ROWS, (r + 1) * MOE_PERM_ROWS)
            xs_ref[sl, :] = _dot(perm, xb).astype(BF16)
            gg = _dot(perm, g2)
            gs_ref[sl, :] = gg[:, 0:LANES] + gg[:, LANES:2 * LANES]
        acc_ref[...] = jnp.zeros_like(acc_ref)

    lo = off_ref[grp]
    hi = off_ref[grp + 1]

    def chunk(c, carry):
        r0 = pl.multiple_of(c * MOE_CHUNK, MOE_CHUNK)

        @pl.when((lo < r0 + MOE_CHUNK) & (hi > r0))
        def _():
            rs = pl.ds(r0, MOE_CHUNK)
            xs = xs_ref[rs, :]
            gsc = gs_ref[rs, :]
            hid = _silu(_dot(xs, wg_ref[...])) * _dot(xs, wu_ref[...])
            parts = []
            for e in range(EXPERTS_PER_GROUP):
                ge = jnp.sum(jnp.where(lane == N_GROUPS + EXPERTS_PER_GROUP * grp + e, gsc, 0.0),
                             axis=-1, keepdims=True)
                parts.append((hid[:, e * EXPERT_FF:(e + 1) * EXPERT_FF] * ge).astype(BF16))
            acc_ref[rs, :] += _dot(jnp.concatenate(parts, axis=1), wd_ref[...])

        return carry

    lax.fori_loop(0, n // MOE_CHUNK, chunk, 0)

    @pl.when(grp == N_GROUPS - 1)
    def _unsort_and_norm():
        xs_ref[...] = acc_ref[...].astype(BF16)
        cols = lax.broadcasted_iota(jnp.int32, (MOE_PERM_ROWS, n), 1)
        for r in range(n // MOE_PERM_ROWS):
            sl = slice(r * MOE_PERM_ROWS, (r + 1) * MOE_PERM_ROWS)
            perm_t = jnp.where(cols == pos_ref[sl, 0:1].astype(jnp.int32), 1.0, 0.0).astype(BF16)
            y = _dot(perm_t, xs_ref[...])
            o_ref[sl, :] = _layer_norm(DEEPNORM_ALPHA * x_ref[sl, :] + y, g_ref[...], b_ref[...])


def _moe(x, wr, br, wg, wu, wd, g, bb):
    t = x.shape[0]
    n = min(MOE_TILE, t)
    tri = jnp.asarray(np.tril(np.ones((n, n), np.float32)), dtype=BF16)
    full = lambda a: pl.BlockSpec(a.shape, lambda i, e: (0,) * a.ndim)
    wspec = pl.BlockSpec((None, D_MODEL, D_MODEL), lambda i, e: (e, 0, 0))
    return pl.pallas_call(
        _moe_kernel,
        grid=(t // n, N_GROUPS),
        in_specs=[pl.BlockSpec((n, D_MODEL), lambda i, e: (i, 0)), full(wr), full(br), full(tri),
                  wspec, wspec, wspec, full(g), full(bb)],
        out_specs=pl.BlockSpec((n, D_MODEL), lambda i, e: (i, 0)),
        out_shape=jax.ShapeDtypeStruct((t, D_MODEL), F32),
        scratch_shapes=[pltpu.VMEM((n, D_MODEL), BF16), pltpu.VMEM((n, LANES), F32),
                        pltpu.VMEM((n, LANES), F32), pltpu.VMEM((n, D_MODEL), F32),
                        pltpu.SMEM((8,), jnp.int32)],
        compiler_params=_cparams(("parallel", "arbitrary")),
        name="moe_ln",
    )(x, wr, br, tri, wg, wu, wd, g, bb)


def _rope_tables(seq):
    rows = seq // GRID_W
    row = jnp.repeat(jnp.arange(rows, dtype=F32), GRID_W)
    col = jnp.tile(jnp.arange(GRID_W, dtype=F32), rows)
    half = HEAD_DIM // 2
    freqs = 1.0 / (ROPE_THETA ** (jnp.arange(0, half, 2, dtype=F32) / half))
    ar, ac = row[:, None] * freqs, col[:, None] * freqs
    cosv = jnp.concatenate([jnp.cos(ar), jnp.cos(ar), jnp.cos(ac), jnp.cos(ac)], axis=-1)
    sinv = jnp.concatenate([-jnp.sin(ar), jnp.sin(ar), -jnp.sin(ac), jnp.sin(ac)], axis=-1)
    return jnp.tile(cosv, (1, 2)), jnp.tile(sinv, (1, 2))


def _position_dft(seq):
    blk = 64
    k = jnp.arange(seq, dtype=jnp.int32)
    ang = lambda j: (2.0 * math.pi / seq) * ((j[:, None] * k[None, :]) % seq).astype(F32)
    a_hi = ang(jnp.arange(0, seq, blk, dtype=jnp.int32))
    a_lo = ang(jnp.arange(blk, dtype=jnp.int32))
    ch, sh, cl, sl = jnp.cos(a_hi), jnp.sin(a_hi), jnp.cos(a_lo), jnp.sin(a_lo)
    sc = seq ** -0.5
    cmat = (ch[:, None, :] * cl[None] - sh[:, None, :] * sl[None]) * sc
    smat = (sh[:, None, :] * cl[None] + ch[:, None, :] * sl[None]) * (-sc)
    return cmat.reshape(seq, seq).astype(BF16), smat.reshape(seq, seq).astype(BF16)


def _channel_dft():
    n = np.arange(HEAD_DIM)
    ang = 2.0 * np.pi * ((n[:, None] * n[None, :]) % HEAD_DIM) / HEAD_DIM
    eye = np.eye(4)
    cb = np.kron(eye, np.cos(ang)) * HEAD_DIM ** -0.5
    sb = np.kron(eye, np.sin(ang)) * HEAD_DIM ** -0.5
    return jnp.asarray(np.concatenate([cb, sb], axis=1), dtype=BF16)


def _layer_params(l, w_in, q_norm_g, k_norm_g, w_fnet, w_pool, pool_scale, conv_w, conv_b, dt_bias,
                  a_log, d_skip, ssd_norm_g, w_out, ln1_g, ln1_b, w_group, b_group, w_router, b_router,
                  w_gate, w_up, w_down, ln2_g, ln2_b):
    wi = w_in[l]
    hperm = np.array([0, 2, 1, 3])
    wq = wi[:, 0:256].reshape(D_MODEL, 4, HEAD_DIM)[:, hperm].reshape(D_MODEL, 256)
    w_proj = jnp.concatenate([wq, wi[:, 256:1792], jnp.pad(wi[:, 1792:1800], ((0, 0), (0, 120)))],
                             axis=1).astype(BF16)
    wo = w_out[l]
    wo_att = wo[0:256].reshape(4, HEAD_DIM, D_MODEL)[hperm].reshape(256, D_MODEL)
    w_o = jnp.concatenate([wo_att, wo[256:]], axis=0).astype(BF16)
    row = lambda v, n: jnp.pad(v.reshape(1, -1), ((0, 0), (0, n - v.size)))
    w_pool_bd = jnp.zeros((256, 256), F32)
    for gi in range(4):
        w_pool_bd = w_pool_bd.at[gi * 64:(gi + 1) * 64, gi * 64:(gi + 1) * 64].set(w_pool[l, gi])
    wr = jnp.pad(jnp.concatenate([w_group[l], w_router[l]], axis=1), ((0, 0), (0, LANES - 20)))
    wr_hi = wr.astype(BF16)
    by_group = lambda w: w.reshape(N_GROUPS, EXPERTS_PER_GROUP, D_MODEL, EXPERT_FF).transpose(
        0, 2, 1, 3).reshape(N_GROUPS, D_MODEL, EXPERTS_PER_GROUP * EXPERT_FF).astype(BF16)
    return dict(
        w_proj=w_proj,
        qg=jnp.tile(q_norm_g[l], 2).reshape(1, LANES), kg=jnp.tile(k_norm_g[l], 2).reshape(1, LANES),
        dtb=row(dt_bias[l], LANES), w_fnet=w_fnet[l].astype(BF16),
        w_pool=w_pool_bd.astype(BF16), pool_scale=pool_scale[l].reshape(1, 256),
        conv_w=conv_w[l], conv_b=conv_b[l].reshape(1, 512),
        alog=row(a_log[l], LANES), dskip=jnp.repeat(d_skip[l], 64).reshape(1, 256),
        ssd_ng=ssd_norm_g[l].reshape(1, 256), w_o=w_o,
        ln1_g=ln1_g[l].reshape(1, D_MODEL), ln1_b=ln1_b[l].reshape(1, D_MODEL),
        wr=jnp.concatenate([wr_hi, (wr - wr_hi.astype(F32)).astype(BF16)], axis=1),
        br=row(jnp.concatenate([b_group[l], b_router[l]]), LANES),
        wg=by_group(w_gate[l]), wu=by_group(w_up[l]),
        wd=w_down[l].reshape(N_GROUPS, EXPERTS_PER_GROUP * EXPERT_FF, D_MODEL).astype(BF16),
        ln2_g=ln2_g[l].reshape(1, D_MODEL), ln2_b=ln2_b[l].reshape(1, D_MODEL),
    )


def _trunk(x3, params, tables):
    b, s, _ = x3.shape
    x = x3.reshape(b * s, D_MODEL)
    cosv, sinv, cmat, smat, bd, dftc = tables
    for p in params:
        q, k, v, ab, up, z, xbc, dt = _in_proj(x, p["w_proj"], p["qg"], p["kg"], cosv, sinv, bd, dftc,
                                               p["dtb"], s)
        att = _attention(q, k, v, b, s)
        fno = _fourier(cmat, smat, ab, p["w_fnet"], b, s)
        pol = _pool(up, p["w_pool"], p["pool_scale"], b, s)
        xc = _conv(xbc, p["conv_w"], p["conv_b"], b, s)
        yf, yb = _ssd(xc, dt, p["alog"], p["dskip"], b, s)
        x = _out_proj(x, att, fno, pol, yf, yb, z, p["w_o"], p["ssd_ng"], p["ln1_g"], p["ln1_b"])
        x = _moe(x, p["wr"], p["br"], p["wg"], p["wu"], p["wd"], p["ln2_g"], p["ln2_b"])
    return x.reshape(b, s, D_MODEL)


def _tables(seq):
    cosv, sinv = _rope_tables(seq)
    cmat, smat = _position_dft(seq)
    lane = np.arange(LANES)
    bd = jnp.asarray((lane[:, None] // HEAD_DIM) == (lane[None, :] // HEAD_DIM), dtype=BF16)
    return cosv, sinv, cmat, smat, bd, _channel_dft()


def kernel(x_prompt, x_sample, w_in, q_norm_g, k_norm_g, w_fnet, w_pool, pool_scale, conv_w, conv_b, dt_bias, a_log, d_skip, ssd_norm_g, w_out, ln1_g, ln1_b, w_group, b_group, w_router, b_router, w_gate, w_up, w_down, ln2_g, ln2_b):
    weights = (w_in, q_norm_g, k_norm_g, w_fnet, w_pool, pool_scale, conv_w, conv_b, dt_bias, a_log,
               d_skip, ssd_norm_g, w_out, ln1_g, ln1_b, w_group, b_group, w_router, b_router,
               w_gate, w_up, w_down, ln2_g, ln2_b)
    params = [_layer_params(l, *weights) for l in range(w_in.shape[0])]
    y_prompt = _trunk(x_prompt, params, _tables(x_prompt.shape[1]))
    y_sample = _trunk(x_sample, params, _tables(x_sample.shape[1]))
    return (y_prompt, y_sample)
```

```python
import functools
import math

import numpy as np
import jax
import jax.numpy as jnp
from jax import lax
from jax.experimental import pallas as pl
from jax.experimental.pallas import tpu as pltpu

F32 = jnp.float32
BF16 = jnp.bfloat16

D_MODEL = 1024
DEPTH = 2
GRID_W = 64
HEAD_DIM = 64
ROPE_THETA = 10000.0
ATT_Q_HEADS = 4
ATT_KV_HEADS = 2
POOL_WINDOWS = (2, 4, 8, 16)
SSD_HEADS = 4
SSD_CONV = 4
SSD_CHUNK = 128
N_GROUPS = 4
EXPERTS_PER_GROUP = 4
N_EXPERTS = 16
EXPERT_FF = 256
DEEPNORM_ALPHA = (2 * DEPTH) ** 0.25
LN_EPS = 1e-5
RMS_EPS = 1e-6

LANES = 128
SUBLANES = 8
HALO = SUBLANES
VMEM_LIMIT = 56 * 1024 * 1024

C_Q, C_K, C_V, C_UF, C_UP, C_Z, C_XBC, C_DT, N_PROJ = 0, 256, 384, 512, 768, 1024, 1280, 1792, 1920
HI = lax.Precision.HIGHEST


def _cparams(sem):
    return pltpu.CompilerParams(dimension_semantics=sem, vmem_limit_bytes=VMEM_LIMIT)


def _dot(a, b):
    return jnp.dot(a, b, preferred_element_type=F32)


def _dot_nt(a, b):
    return lax.dot_general(a, b, (((1,), (1,)), ((), ())), preferred_element_type=F32)


def _silu(x):
    return x * (1.0 / (1.0 + jnp.exp(-x)))


def _layer_norm(x, g, b):
    mu = jnp.mean(x, axis=-1, keepdims=True)
    xc = x - mu
    var = jnp.mean(xc * xc, axis=-1, keepdims=True)
    return xc * lax.rsqrt(var + LN_EPS) * g + b


def _head_sumsq(x, ones_bd):
    sq = x * x
    hi = sq.astype(BF16)
    lo = (sq - hi.astype(F32)).astype(BF16)
    return _dot(hi, ones_bd) + _dot(lo, ones_bd)


def _rope(x, cosv, sinv, first_half):
    w = x.shape[-1]
    partner = jnp.where(first_half, pltpu.roll(x, w - 16, axis=1), pltpu.roll(x, 16, axis=1))
    return x * cosv + partner * sinv


def _in_proj_kernel(x_ref, w_ref, qg_ref, kg_ref, cos_ref, sin_ref, bd_ref, dftc_ref, dtb_ref,
                    q_ref, k_ref, v_ref, ab_ref, up_ref, z_ref, xbc_ref, dt_ref):
    xb = x_ref[...].astype(BF16)
    h = _dot(xb, w_ref[...])
    cosv = cos_ref[...]
    sinv = sin_ref[...]
    bd = bd_ref[...]
    lane = lax.broadcasted_iota(jnp.int32, (1, LANES), 1)
    first_half = (lane & 31) < 16
    for c in range(2):
        qc = h[:, C_Q + c * LANES:C_Q + (c + 1) * LANES]
        ss = _head_sumsq(qc, bd)
        qn = qc * lax.rsqrt(ss * (1.0 / HEAD_DIM) + RMS_EPS) * qg_ref[...]
        qr = _rope(qn, cosv, sinv, first_half) * (HEAD_DIM ** -0.5 * math.log2(math.e))
        q_ref[:, c * LANES:(c + 1) * LANES] = qr.astype(BF16)
    kc = h[:, C_K:C_K + LANES]
    ss = _head_sumsq(kc, bd)
    kn = kc * lax.rsqrt(ss * (1.0 / HEAD_DIM) + RMS_EPS) * kg_ref[...]
    k_ref[...] = _rope(kn, cosv, sinv, first_half).astype(BF16)
    v_ref[...] = h[:, C_V:C_V + LANES].astype(BF16)
    uf = h[:, C_UF:C_UF + 256].astype(BF16)
    ab_ref[...] = _dot(uf, dftc_ref[...]).astype(BF16)
    up_ref[...] = h[:, C_UP:C_UP + 256]
    z_ref[...] = h[:, C_Z:C_Z + 256]
    xbc_ref[...] = h[:, C_XBC:C_XBC + 512]
    dr = h[:, C_DT:C_DT + LANES] + dtb_ref[...]
    dt_ref[...] = jnp.maximum(dr, 0.0) + jnp.log(1.0 + jnp.exp(-jnp.abs(dr)))


def _in_proj(x, w, qg, kg, cosv, sinv, bd, dftc, dtb, seq):
    t = x.shape[0]
    tm = min(512, seq)
    nseq = seq // tm
    full = lambda a: pl.BlockSpec(a.shape, lambda i: (0,) * a.ndim)
    row = lambda wdt: pl.BlockSpec((tm, wdt), lambda i: (i, 0))
    pos = pl.BlockSpec((tm, LANES), lambda i: (i % nseq, 0))
    outs = [(256, BF16), (128, BF16), (128, BF16), (512, BF16), (256, F32), (256, F32), (512, F32),
            (128, F32)]
    return pl.pallas_call(
        _in_proj_kernel,
        grid=(t // tm,),
        in_specs=[row(D_MODEL), full(w), full(qg), full(kg), pos, pos, full(bd), full(dftc), full(dtb)],
        out_specs=[row(wd) for wd, _ in outs],
        out_shape=[jax.ShapeDtypeStruct((t, wd), dt) for wd, dt in outs],
        compiler_params=_cparams(("parallel",)),
        name="in_proj",
    )(x, w, qg, kg, cosv, sinv, bd, dftc, dtb)


ATT_KEY_CHUNK = 256


def _attn_kernel(q_ref, k_ref, v_ref, o_ref, vt_ref):
    nch, _, vrows, tk = vt_ref.shape
    tq = q_ref.shape[0]

    @pl.when(pl.program_id(1) == 0)
    def _():
        for c in range(nch):
            vt = v_ref[c * tk:(c + 1) * tk, :].astype(F32).T.astype(BF16)
            for j in range(ATT_KV_HEADS):
                vt_ref[c, j, 0:HEAD_DIM, :] = vt[j * HEAD_DIM:(j + 1) * HEAD_DIM, :]
                vt_ref[c, j, HEAD_DIM:vrows, :] = jnp.ones((vrows - HEAD_DIM, tk), BF16)

    lane = lax.broadcasted_iota(jnp.int32, (1, LANES), 1)
    qms = []
    for c in range(2):
        qc = q_ref[:, c * LANES:(c + 1) * LANES]
        for j in range(ATT_KV_HEADS):
            qms.append(jnp.where((lane >> 6) == j, qc, jnp.zeros_like(qc)))

    def scores(c):
        kc = k_ref[c * tk:(c + 1) * tk, :]
        return [_dot_nt(kc, qm) for qm in qms]

    ms = [jnp.full((1, tq), -jnp.inf, F32)] * ATT_Q_HEADS
    accs = [jnp.zeros((vrows, tq), F32)] * ATT_Q_HEADS
    st_next = scores(0)
    for c in range(nch):
        sts = st_next
        if c + 1 < nch:
            st_next = scores(c + 1)
        for h in range(ATT_Q_HEADS):
            m_new = jnp.maximum(ms[h], jnp.max(sts[h], axis=0, keepdims=True))
            alpha = jnp.exp2(ms[h] - m_new)
            p = jnp.exp2(sts[h] - m_new).astype(BF16)
            ms[h] = m_new
            accs[h] = alpha * accs[h] + _dot(vt_ref[c, h % ATT_KV_HEADS], p)
    for c in range(2):
        ot = jnp.concatenate([accs[2 * c + j][0:HEAD_DIM] / accs[2 * c + j][HEAD_DIM:HEAD_DIM + 1]
                              for j in range(ATT_KV_HEADS)], axis=0)
        o_ref[:, c * LANES:(c + 1) * LANES] = ot.T.astype(BF16)


def _attention(q, k, v, b, s):
    tq = min(256, s)
    tk = min(ATT_KEY_CHUNK, s)
    q3, k3, v3 = q.reshape(b, s, 256), k.reshape(b, s, LANES), v.reshape(b, s, LANES)
    out = pl.pallas_call(
        _attn_kernel,
        grid=(b, s // tq),
        in_specs=[pl.BlockSpec((None, tq, 256), lambda bi, i: (bi, i, 0)),
                  pl.BlockSpec((None, s, LANES), lambda bi, i: (bi, 0, 0)),
                  pl.BlockSpec((None, s, LANES), lambda bi, i: (bi, 0, 0))],
        out_specs=pl.BlockSpec((None, tq, 256), lambda bi, i: (bi, i, 0)),
        out_shape=jax.ShapeDtypeStruct((b, s, 256), BF16),
        scratch_shapes=[pltpu.VMEM((s // tk, ATT_KV_HEADS, HEAD_DIM + 16, tk), BF16)],
        compiler_params=_cparams(("parallel", "arbitrary")),
        name="attention",
    )(q3, k3, v3)
    return out.reshape(b * s, 256)


FFT_N2 = 64
FFT_A_COLS = 8
FFT_B_BLOCKS = 8


def _fourier_a_kernel(ab_ref, w1_ref, tc_ref, ts_ref, yr_ref, yi_ref):
    n1 = ab_ref.shape[0]
    w1 = w1_ref[...]
    for g in range(FFT_A_COLS):
        r = _dot(w1, ab_ref[:, g * 512:(g + 1) * 512])
        yr = r[0:n1, 0:256] - r[n1:2 * n1, 256:512]
        ym = r[0:n1, 256:512] + r[n1:2 * n1, 0:256]
        tc = jnp.concatenate([tc_ref[g], tc_ref[g]], axis=1)
        ts = jnp.concatenate([ts_ref[g], ts_ref[g]], axis=1)
        yr_ref[:, g * 256:(g + 1) * 256] = (yr * tc - ym * ts).astype(BF16)
        yi_ref[:, g * 256:(g + 1) * 256] = (-(ym * tc) - yr * ts).astype(BF16)


def _fourier_b_kernel(yr_ref, yi_ref, c2_ref, s2_ref, w_ref, o_ref):
    c2 = c2_ref[...]
    s2 = s2_ref[...]
    zs = []
    for g in range(FFT_B_BLOCKS):
        rows = slice(g * FFT_N2, (g + 1) * FFT_N2)
        zs.append((_dot(c2, yr_ref[rows, :]) + _dot(s2, yi_ref[rows, :])).astype(BF16))
    mixed = _dot(jnp.concatenate(zs, axis=0), w_ref[...]).astype(BF16)
    for g in range(FFT_B_BLOCKS):
        o_ref[:, g * 256:(g + 1) * 256] = mixed[g * FFT_N2:(g + 1) * FFT_N2, :]


def _fourier(tabs, ab, w_fnet, b, s):
    w1, tc, ts, c2, s2 = tabs
    n1 = s // FFT_N2
    full = lambda a: pl.BlockSpec(a.shape, lambda bi, i: (0,) * a.ndim)
    tw = pl.BlockSpec((FFT_A_COLS, n1, LANES), lambda bi, i: (i, 0, 0))
    yspec = pl.BlockSpec((None, n1, FFT_A_COLS * 256), lambda bi, i: (bi, 0, i))
    yr, yi = pl.pallas_call(
        _fourier_a_kernel,
        grid=(b, FFT_N2 // FFT_A_COLS),
        in_specs=[pl.BlockSpec((None, n1, FFT_A_COLS * 512), lambda bi, i: (bi, 0, i)), full(w1), tw, tw],
        out_specs=[yspec, yspec],
        out_shape=[jax.ShapeDtypeStruct((b, n1, FFT_N2 * 256), BF16)] * 2,
        compiler_params=_cparams(("parallel", "parallel")),
        name="fourier_a",
    )(ab.reshape(b, n1, FFT_N2 * 512), w1, tc, ts)
    rows = FFT_B_BLOCKS * FFT_N2
    blk = pl.BlockSpec((None, rows, 256), lambda bi, i: (bi, i, 0))
    out = pl.pallas_call(
        _fourier_b_kernel,
        grid=(b, n1 // FFT_B_BLOCKS),
        in_specs=[blk, blk, full(c2), full(s2), full(w_fnet)],
        out_specs=pl.BlockSpec((None, FFT_N2, FFT_B_BLOCKS * 256), lambda bi, i: (bi, 0, i)),
        out_shape=jax.ShapeDtypeStruct((b, FFT_N2, n1 * 256), BF16),
        compiler_params=_cparams(("parallel", "parallel")),
        name="fourier_b",
    )(yr.reshape(b, s, 256), yi.reshape(b, s, 256), c2, s2, w_fnet)
    return out.reshape(b * s, 256)


def _halo_specs(tt, width, s):
    nb = tt // HALO
    last = s // HALO - 1
    main = pl.BlockSpec((None, tt, width), lambda bi, i: (bi, i, 0))
    prev = pl.BlockSpec((None, HALO, width), lambda bi, i: (bi, jnp.maximum(i * nb - 1, 0), 0))
    nxt = pl.BlockSpec((None, HALO, width), lambda bi, i: (bi, jnp.minimum((i + 1) * nb, last), 0))
    return [main, prev, nxt]


def _with_halo(main_ref, prev_ref, next_ref):
    i = pl.program_id(1)
    n = pl.num_programs(1)
    prev = jnp.where(i > 0, prev_ref[...], 0.0)
    nxt = jnp.where(i < n - 1, next_ref[...], 0.0)
    return jnp.concatenate([prev, main_ref[...], nxt], axis=0)


def _shifted(ext, d, tt):
    n = ext.shape[0]
    r = ext if d == 0 else pltpu.roll(ext, (-d) % n, axis=0)
    return r[HALO:HALO + tt]


def _pool_kernel(u_ref, up_ref, un_ref, w_ref, sc_ref, o_ref, *, seq):
    tt = u_ref.shape[0]
    ext = _with_halo(u_ref, up_ref, un_ref)
    n = ext.shape[0]
    sh = lambda a, d: a if d == 0 else pltpu.roll(a, (-d) % n, axis=0)
    p2 = sh(ext, -1) + ext
    p4 = sh(p2, -1) + sh(p2, 1)
    p8 = sh(p4, -2) + sh(p4, 2)
    p16 = sh(p8, -4) + sh(p8, 4)
    t = pl.program_id(1) * tt + lax.broadcasted_iota(jnp.int32, (tt, 1), 0)
    grp = lax.broadcasted_iota(jnp.int32, (1, 256), 1) >> 6
    u = u_ref[...]
    pooled = None
    for gi, (w, pw) in enumerate(zip(POOL_WINDOWS, (p2, p4, p8, p16))):
        lo = jnp.clip(t - w // 2, 0, seq - 1)
        hi = jnp.clip(t + (w - w // 2) - 1, 0, seq - 1)
        cnt = (hi - lo + 1).astype(F32)
        val = pw[HALO:HALO + tt] / cnt - u
        pooled = val if pooled is None else jnp.where(grp == gi, val, pooled)
    o_ref[...] = (_dot(pooled.astype(BF16), w_ref[...]) * sc_ref[...]).astype(BF16)


def _pool(u, w_bd, scale, b, s):
    tt = min(1024, s)
    u3 = u.reshape(b, s, 256)
    full = lambda a: pl.BlockSpec(a.shape, lambda bi, i: (0,) * a.ndim)
    out = pl.pallas_call(
        functools.partial(_pool_kernel, seq=s),
        grid=(b, s // tt),
        in_specs=_halo_specs(tt, 256, s) + [full(w_bd), full(scale)],
        out_specs=pl.BlockSpec((None, tt, 256), lambda bi, i: (bi, i, 0)),
        out_shape=jax.ShapeDtypeStruct((b, s, 256), BF16),
        compiler_params=_cparams(("parallel", "parallel")),
        name="pool",
    )(u3, u3, u3, w_bd, scale)
    return out.reshape(b * s, 256)


def _conv_kernel(x_ref, xp_ref, xn_ref, w_ref, b_ref, o_ref):
    tt = x_ref.shape[0]
    ext = _with_halo(x_ref, xp_ref, xn_ref)
    acc = b_ref[...] + _shifted(ext, -2, tt) * w_ref[0:1, :]
    for kk in range(1, SSD_CONV):
        acc = acc + _shifted(ext, kk - 2, tt) * w_ref[kk:kk + 1, :]
    o_ref[...] = _silu(acc)


def _conv(xbc, conv_w, conv_b, b, s):
    tt = min(1024, s)
    x3 = xbc.reshape(b, s, 512)
    full = lambda a: pl.BlockSpec(a.shape, lambda bi, i: (0,) * a.ndim)
    return pl.pallas_call(
        _conv_kernel,
        grid=(b, s // tt),
        in_specs=_halo_specs(tt, 512, s) + [full(conv_w), full(conv_b)],
        out_specs=pl.BlockSpec((None, tt, 512), lambda bi, i: (bi, i, 0)),
        out_shape=jax.ShapeDtypeStruct((b, s, 512), F32),
        compiler_params=_cparams(("parallel", "parallel")),
        name="ssd_conv",
    )(x3, x3, x3, conv_w, conv_b)


def _ssd_kernel(xf_ref, dtf_ref, xb_ref, dtb_ref, alog_ref, dskip_ref, yf_ref, yb_ref, carry_ref):
    @pl.when(pl.program_id(1) == 0)
    def _():
        carry_ref[...] = jnp.zeros_like(carry_ref)

    L = SSD_CHUNK
    nch = xf_ref.shape[0] // L
    a_row = -jnp.exp(alog_ref[...])
    dskip = dskip_ref[...]
    r = lax.broadcasted_iota(jnp.int32, (L, L), 0)
    c = lax.broadcasted_iota(jnp.int32, (L, L), 1)
    keeps = (c <= r, c >= r)
    tris = [k.astype(F32) for k in keeps]
    grp = lax.broadcasted_iota(jnp.int32, (1, LANES), 1) >> 6
    head = lax.broadcasted_iota(jnp.int32, (1, 256), 1) >> 6
    row_grp = lax.broadcasted_iota(jnp.int32, (LANES, 1), 0) >> 6

    units = []
    for ci in range(nch):
        units.append(dict(d=0, rows=slice(ci * L, (ci + 1) * L)))
        units.append(dict(d=1, rows=slice((nch - 1 - ci) * L, (nch - ci) * L)))

    for u in units:
        d = u["d"]
        u["off"] = d * SSD_HEADS
        u["xc"] = (xf_ref, xb_ref)[d][u["rows"], :]
        u["dtc"] = (dtf_ref, dtb_ref)[d][u["rows"], :]
        u["cs"] = jnp.dot(tris[d], u["dtc"] * a_row, precision=HI, preferred_element_type=F32)

    prow = lax.broadcasted_iota(jnp.int32, (LANES, 1), 0)
    spread = [jnp.where((prow < 16) & ((prow & 7) == d * SSD_HEADS + head), 1.0, 0.0).astype(BF16)
              for d in range(2)]
    low8 = lax.broadcasted_iota(jnp.int32, (1, LANES), 1) < 8

    for u in units:
        cs, xc = u["cs"], u["xc"]
        u["cs_t"] = cs.T
        tot = cs[0:1, :] if u["d"] else cs[L - 1:L, :]
        fac = jnp.concatenate([u["dtc"], jnp.exp(tot - cs), jnp.exp(cs)], axis=0)
        head_part = fac.astype(BF16).astype(F32)
        fac_x = _dot(jnp.where(low8, head_part, fac - head_part).astype(BF16), spread[u["d"]])
        u["ds_x"], u["ecs_x"] = fac_x[L:2 * L], fac_x[2 * L:3 * L]
        u["etot_x"] = u["ecs_x"][0:1, :] if u["d"] else u["ecs_x"][L - 1:L, :]
        u["xd"] = xc[:, 0:256] * fac_x[0:L]
        u["xd_b"] = u["xd"].astype(BF16)
        u["b_t"] = xc[:, 256:384].T.astype(BF16)
        cmat = xc[:, 384:512]
        u["g"] = [_dot(jnp.where(grp == g, cmat, 0.0).astype(BF16), u["b_t"]) for g in range(2)]

    for u in units:
        cs, cs_t, off, keep = u["cs"], u["cs_t"], u["off"], keeps[u["d"]]
        y = None
        for h in range(SSD_HEADS):
            diff = cs[:, off + h:off + h + 1] - cs_t[off + h:off + h + 1, :]
            decay = jnp.where(keep, jnp.exp(jnp.where(keep, diff, 0.0)), 0.0)
            sc = (u["g"][h // 2] * decay).astype(BF16)
            yd = _dot(sc, u["xd_b"])
            y = yd if y is None else jnp.where(head == h, yd, y)
        u["y"] = y

    for u in units:
        st = _dot(u["b_t"], (u["xd"] * u["ds_x"]).astype(BF16))
        u["st"] = jnp.where(row_grp == (head >> 1), st, 0.0)

    carries = [carry_ref[0], carry_ref[1]]
    for u in units:
        d, xc = u["d"], u["xc"]
        y_off = _dot(xc[:, 384:512].astype(BF16), carries[d].astype(BF16))
        y = u["y"] + y_off * u["ecs_x"]
        carries[d] = carries[d] * u["etot_x"] + u["st"]
        if d == 0:
            yf_ref[u["rows"], :] = y + xc[:, 0:256] * dskip
        else:
            yb_ref[u["rows"], :] = y
    carry_ref[0] = carries[0]
    carry_ref[1] = carries[1]


def _ssd(xc, dt, alog_row, dskip_row, b, s):
    tt = min(512, s)
    nt = s // tt
    dt3 = dt.reshape(b, s, LANES)
    fwd = lambda w: pl.BlockSpec((None, tt, w), lambda bi, i: (bi, i, 0))
    bwd = lambda w: pl.BlockSpec((None, tt, w), lambda bi, i: (bi, nt - 1 - i, 0))
    full = lambda a: pl.BlockSpec(a.shape, lambda bi, i: (0,) * a.ndim)
    yf, yb = pl.pallas_call(
        _ssd_kernel,
        grid=(b, nt),
        in_specs=[fwd(512), fwd(LANES), bwd(512), bwd(LANES), full(alog_row), full(dskip_row)],
        out_specs=[fwd(256), bwd(256)],
        out_shape=[jax.ShapeDtypeStruct((b, s, 256), F32)] * 2,
        scratch_shapes=[pltpu.VMEM((2, LANES, 256), F32)],
        compiler_params=_cparams(("parallel", "arbitrary")),
        name="ssd_scan",
    )(xc, dt3, xc, dt3, alog_row, dskip_row)
    return yf.reshape(b * s, 256), yb.reshape(b * s, 256)


def _out_proj_kernel(x_ref, att_ref, fno_ref, pol_ref, yf_ref, yb_ref, z_ref, w_ref, ng_ref,
                     g_ref, b_ref, o_ref):
    y = (yf_ref[...] + yb_ref[...]) * _silu(z_ref[...])
    ng = ng_ref[...]
    parts = []
    for gi in range(2):
        yg = y[:, gi * LANES:(gi + 1) * LANES]
        ms = jnp.mean(yg * yg, axis=-1, keepdims=True)
        parts.append((yg * lax.rsqrt(ms + RMS_EPS) * ng[:, gi * LANES:(gi + 1) * LANES]).astype(BF16))
    mix = _dot(att_ref[...], w_ref[0:256, :]) + _dot(fno_ref[...], w_ref[256:512, :])
    mix = mix + _dot(pol_ref[...], w_ref[512:768, :])
    mix = mix + _dot(parts[0], w_ref[768:896, :]) + _dot(parts[1], w_ref[896:1024, :])
    o_ref[...] = _layer_norm(DEEPNORM_ALPHA * x_ref[...] + mix, g_ref[...], b_ref[...])


def _out_proj(x, att, fno, pol, yf, yb, z, w, ng, g, bb):
    t = x.shape[0]
    tm = min(512, t)
    row = lambda wd: pl.BlockSpec((tm, wd), lambda i: (i, 0))
    full = lambda a: pl.BlockSpec(a.shape, lambda i: (0,) * a.ndim)
    return pl.pallas_call(
        _out_proj_kernel,
        grid=(t // tm,),
        in_specs=[row(D_MODEL)] + [row(256)] * 6 + [full(w), full(ng), full(g), full(bb)],
        out_specs=row(D_MODEL),
        out_shape=jax.ShapeDtypeStruct((t, D_MODEL), F32),
        compiler_params=_cparams(("parallel",)),
        name="out_proj_ln",
    )(x, att, fno, pol, yf, yb, z, w, ng, g, bb)


MOE_TILE = 1024
MOE_CHUNK = 128
MOE_PERM_ROWS = 256


def _router(x, w2, br):
    xh = x.astype(BF16)
    xl = (x - xh.astype(F32)).astype(BF16)
    l1 = _dot(xh, w2)
    logits = l1[:, 0:LANES] + l1[:, LANES:2 * LANES] + _dot(xl, w2[:, 0:LANES]) + br
    lane = lax.broadcasted_iota(jnp.int32, (1, LANES), 1)
    ninf = -jnp.inf
    gl = jnp.where(lane < N_GROUPS, logits, ninf)
    gmax = jnp.max(gl, axis=-1, keepdims=True)
    g_p = 1.0 / jnp.sum(jnp.exp(gl - gmax), axis=-1, keepdims=True)
    g_idx = jnp.min(jnp.where(gl == gmax, lane, LANES), axis=-1, keepdims=True)
    e_lane = lane - N_GROUPS
    in_grp = (e_lane >= 0) & (e_lane < N_EXPERTS) & ((e_lane >> 2) == g_idx)
    el = jnp.where(in_grp, logits, ninf)
    m1 = jnp.max(el, axis=-1, keepdims=True)
    i1 = jnp.min(jnp.where(el == m1, lane, LANES), axis=-1, keepdims=True)
    el2 = jnp.where(lane == i1, ninf, el)
    m2 = jnp.max(el2, axis=-1, keepdims=True)
    i2 = jnp.min(jnp.where(el2 == m2, lane, LANES), axis=-1, keepdims=True)
    e2 = jnp.exp(m2 - m1)
    w1 = 1.0 / (1.0 + e2)
    gates = g_p * (jnp.where(lane == i1, w1, 0.0) + jnp.where(lane == i2, e2 * w1, 0.0))
    return gates, g_idx


def _moe_kernel(x_ref, wr_ref, br_ref, tri_ref, wg_ref, wu_ref, wd_ref, g_ref, b_ref, o_ref,
                xs_ref, gs_ref, pos_ref, acc_ref, off_ref):
    grp = pl.program_id(1)
    n = x_ref.shape[0]
    lane = lax.broadcasted_iota(jnp.int32, (1, LANES), 1)

    @pl.when(grp == 0)
    def _route_and_sort():
        x = x_ref[...]
        gates, g_idx = _router(x, wr_ref[...], br_ref[...])
        onehot = lane == g_idx
        csum = _dot(tri_ref[...], jnp.where(onehot, 1.0, 0.0).astype(BF16))
        cnt = csum[n - 1:n, :].astype(jnp.int32)
        c0, c1, c2 = cnt[0, 0], cnt[0, 1], cnt[0, 2]
        off_ref[0] = 0
        off_ref[1] = c0
        off_ref[2] = c0 + c1
        off_ref[3] = c0 + c1 + c2
        off_ref[4] = n
        offv = jnp.where(lane == 1, c0, jnp.where(lane == 2, c0 + c1, jnp.where(lane == 3, c0 + c1 + c2, 0)))
        pos = jnp.sum(jnp.where(onehot, csum + offv.astype(F32), 0.0), axis=-1, keepdims=True) - 1.0
        pos_ref[...] = jnp.broadcast_to(pos, (n, LANES))
        pos_row = pos_ref[...].T[0:1, :].astype(jnp.int32)
        xb = x.astype(BF16)
        g_hi = gates.astype(BF16)
        g2 = jnp.concatenate([g_hi, (gates - g_hi.astype(F32)).astype(BF16)], axis=1)
        for r in range(n // MOE_PERM_ROWS):
            rows = lax.broadcasted_iota(jnp.int32, (MOE_PERM_ROWS, n), 0) + r * MOE_PERM_ROWS
            perm = jnp.where(rows == pos_row, 1.0, 0.0).astype(BF16)
            sl = slice(r * MOE_PERM_ROWS, (r + 1) * MOE_PERM_ROWS)
            xs_ref[sl, :] = _dot(perm, xb).astype(BF16)
            gg = _dot(perm, g2)
            gs_ref[sl, :] = gg[:, 0:LANES] + gg[:, LANES:2 * LANES]
        acc_ref[...] = jnp.zeros_like(acc_ref)

    lo = off_ref[grp]
    hi = off_ref[grp + 1]

    def chunk(c, carry):
        r0 = pl.multiple_of(c * MOE_CHUNK, MOE_CHUNK)

        @pl.when((lo < r0 + MOE_CHUNK) & (hi > r0))
        def _():
            rs = pl.ds(r0, MOE_CHUNK)
            xs = xs_ref[rs, :]
            gsc = gs_ref[rs, :]
            hid = _silu(_dot(xs, wg_ref[...])) * _dot(xs, wu_ref[...])
            parts = []
            for e in range(EXPERTS_PER_GROUP):
                ge = jnp.sum(jnp.where(lane == N_GROUPS + EXPERTS_PER_GROUP * grp + e, gsc, 0.0),
                             axis=-1, keepdims=True)
                parts.append((hid[:, e * EXPERT_FF:(e + 1) * EXPERT_FF] * ge).astype(BF16))
            acc_ref[rs, :] += _dot(jnp.concatenate(parts, axis=1), wd_ref[...])

        return carry

    lax.fori_loop(0, n // MOE_CHUNK, chunk, 0)

    @pl.when(grp == N_GROUPS - 1)
    def _unsort_and_norm():
        xs_ref[...] = acc_ref[...].astype(BF16)
        cols = lax.broadcasted_iota(jnp.int32, (MOE_PERM_ROWS, n), 1)
        for r in range(n // MOE_PERM_ROWS):
            sl = slice(r * MOE_PERM_ROWS, (r + 1) * MOE_PERM_ROWS)
            perm_t = jnp.where(cols == pos_ref[sl, 0:1].astype(jnp.int32), 1.0, 0.0).astype(BF16)
            y = _dot(perm_t, xs_ref[...])
            o_ref[sl, :] = _layer_norm(DEEPNORM_ALPHA * x_ref[sl, :] + y, g_ref[...], b_ref[...])


def _moe(x, wr, br, wg, wu, wd, g, bb):
    t = x.shape[0]
    n = min(MOE_TILE, t)
    tri = jnp.asarray(np.tril(np.ones((n, n), np.float32)), dtype=BF16)
    full = lambda a: pl.BlockSpec(a.shape, lambda i, e: (0,) * a.ndim)
    wspec = pl.BlockSpec((None, D_MODEL, D_MODEL), lambda i, e: (e, 0, 0))
    return pl.pallas_call(
        _moe_kernel,
        grid=(t // n, N_GROUPS),
        in_specs=[pl.BlockSpec((n, D_MODEL), lambda i, e: (i, 0)), full(wr), full(br), full(tri),
                  wspec, wspec, wspec, full(g), full(bb)],
        out_specs=pl.BlockSpec((n, D_MODEL), lambda i, e: (i, 0)),
        out_shape=jax.ShapeDtypeStruct((t, D_MODEL), F32),
        scratch_shapes=[pltpu.VMEM((n, D_MODEL), BF16), pltpu.VMEM((n, LANES), F32),
                        pltpu.VMEM((n, LANES), F32), pltpu.VMEM((n, D_MODEL), F32),
                        pltpu.SMEM((8,), jnp.int32)],
        compiler_params=_cparams(("parallel", "arbitrary")),
        name="moe_ln",
    )(x, wr, br, tri, wg, wu, wd, g, bb)


def _rope_tables(seq):
    rows = seq // GRID_W
    row = jnp.repeat(jnp.arange(rows, dtype=F32), GRID_W)
    col = jnp.tile(jnp.arange(GRID_W, dtype=F32), rows)
    half = HEAD_DIM // 2
    freqs = 1.0 / (ROPE_THETA ** (jnp.arange(0, half, 2, dtype=F32) / half))
    ar, ac = row[:, None] * freqs, col[:, None] * freqs
    cosv = jnp.concatenate([jnp.cos(ar), jnp.cos(ar), jnp.cos(ac), jnp.cos(ac)], axis=-1)
    sinv = jnp.concatenate([-jnp.sin(ar), jnp.sin(ar), -jnp.sin(ac), jnp.sin(ac)], axis=-1)
    return jnp.tile(cosv, (1, 2)), jnp.tile(sinv, (1, 2))


def _position_dft(seq):
    n1 = seq // FFT_N2
    dft = lambda n: 2.0 * np.pi * ((np.arange(n)[:, None] * np.arange(n)[None, :]) % n) / n
    a1 = dft(n1)
    w1 = np.concatenate([np.cos(a1), np.sin(a1)], axis=0) * seq ** -0.5
    at = 2.0 * np.pi * (np.arange(FFT_N2)[:, None] * np.arange(n1)[None, :]) / seq
    lanes = lambda t: jnp.asarray(np.repeat(t[:, :, None], LANES, axis=2), dtype=F32)
    a2 = dft(FFT_N2)
    return (jnp.asarray(w1, dtype=BF16), lanes(np.cos(at)), lanes(np.sin(at)),
            jnp.asarray(np.cos(a2), dtype=BF16), jnp.asarray(np.sin(a2), dtype=BF16))


def _channel_dft():
    n = np.arange(HEAD_DIM)
    ang = 2.0 * np.pi * ((n[:, None] * n[None, :]) % HEAD_DIM) / HEAD_DIM
    eye = np.eye(4)
    cb = np.kron(eye, np.cos(ang)) * HEAD_DIM ** -0.5
    sb = np.kron(eye, np.sin(ang)) * HEAD_DIM ** -0.5
    return jnp.asarray(np.concatenate([cb, sb], axis=1), dtype=BF16)


def _layer_params(l, w_in, q_norm_g, k_norm_g, w_fnet, w_pool, pool_scale, conv_w, conv_b, dt_bias,
                  a_log, d_skip, ssd_norm_g, w_out, ln1_g, ln1_b, w_group, b_group, w_router, b_router,
                  w_gate, w_up, w_down, ln2_g, ln2_b):
    wi = w_in[l]
    hperm = np.array([0, 2, 1, 3])
    wq = wi[:, 0:256].reshape(D_MODEL, 4, HEAD_DIM)[:, hperm].reshape(D_MODEL, 256)
    twice = lambda v: jnp.concatenate([v, v], axis=-1)
    w_proj = jnp.concatenate([wq, wi[:, 256:1792], jnp.pad(twice(wi[:, 1792:1800]), ((0, 0), (0, 112)))],
                             axis=1).astype(BF16)
    wo = w_out[l]
    wo_att = wo[0:256].reshape(4, HEAD_DIM, D_MODEL)[hperm].reshape(256, D_MODEL)
    w_o = jnp.concatenate([wo_att, wo[256:]], axis=0).astype(BF16)
    row = lambda v, n: jnp.pad(v.reshape(1, -1), ((0, 0), (0, n - v.size)))
    w_pool_bd = jnp.zeros((256, 256), F32)
    for gi in range(4):
        w_pool_bd = w_pool_bd.at[gi * 64:(gi + 1) * 64, gi * 64:(gi + 1) * 64].set(w_pool[l, gi])
    wr = jnp.pad(jnp.concatenate([w_group[l], w_router[l]], axis=1), ((0, 0), (0, LANES - 20)))
    wr_hi = wr.astype(BF16)
    by_group = lambda w: w.reshape(N_GROUPS, EXPERTS_PER_GROUP, D_MODEL, EXPERT_FF).transpose(
        0, 2, 1, 3).reshape(N_GROUPS, D_MODEL, EXPERTS_PER_GROUP * EXPERT_FF).astype(BF16)
    return dict(
        w_proj=w_proj,
        qg=jnp.tile(q_norm_g[l], 2).reshape(1, LANES), kg=jnp.tile(k_norm_g[l], 2).reshape(1, LANES),
        dtb=row(twice(dt_bias[l].reshape(-1)), LANES), w_fnet=w_fnet[l].astype(BF16),
        w_pool=w_pool_bd.astype(BF16), pool_scale=pool_scale[l].reshape(1, 256),
        conv_w=conv_w[l], conv_b=conv_b[l].reshape(1, 512),
        alog=row(twice(a_log[l].reshape(-1)), LANES), dskip=jnp.repeat(d_skip[l], 64).reshape(1, 256),
        ssd_ng=ssd_norm_g[l].reshape(1, 256), w_o=w_o,
        ln1_g=ln1_g[l].reshape(1, D_MODEL), ln1_b=ln1_b[l].reshape(1, D_MODEL),
        wr=jnp.concatenate([wr_hi, (wr - wr_hi.astype(F32)).astype(BF16)], axis=1),
        br=row(jnp.concatenate([b_group[l], b_router[l]]), LANES),
        wg=by_group(w_gate[l]), wu=by_group(w_up[l]),
        wd=w_down[l].reshape(N_GROUPS, EXPERTS_PER_GROUP * EXPERT_FF, D_MODEL).astype(BF16),
        ln2_g=ln2_g[l].reshape(1, D_MODEL), ln2_b=ln2_b[l].reshape(1, D_MODEL),
    )


def _trunk(x3, params, tables):
    b, s, _ = x3.shape
    x = x3.reshape(b * s, D_MODEL)
    cosv, sinv, pos_dft, bd, dftc = tables
    for p in params:
        q, k, v, ab, up, z, xbc, dt = _in_proj(x, p["w_proj"], p["qg"], p["kg"], cosv, sinv, bd, dftc,
                                               p["dtb"], s)
        att = _attention(q, k, v, b, s)
        fno = _fourier(pos_dft, ab, p["w_fnet"], b, s)
        pol = _pool(up, p["w_pool"], p["pool_scale"], b, s)
        xc = _conv(xbc, p["conv_w"], p["conv_b"], b, s)
        yf, yb = _ssd(xc, dt, p["alog"], p["dskip"], b, s)
        x = _out_proj(x, att, fno, pol, yf, yb, z, p["w_o"], p["ssd_ng"], p["ln1_g"], p["ln1_b"])
        x = _moe(x, p["wr"], p["br"], p["wg"], p["wu"], p["wd"], p["ln2_g"], p["ln2_b"])
    return x.reshape(b, s, D_MODEL)


def _tables(seq):
    cosv, sinv = _rope_tables(seq)
    lane = np.arange(LANES)
    bd = jnp.asarray((lane[:, None] // HEAD_DIM) == (lane[None, :] // HEAD_DIM), dtype=BF16)
    return cosv, sinv, _position_dft(seq), bd, _channel_dft()


def kernel(x_prompt, x_sample, w_in, q_norm_g, k_norm_g, w_fnet, w_pool, pool_scale, conv_w, conv_b, dt_bias, a_log, d_skip, ssd_norm_g, w_out, ln1_g, ln1_b, w_group, b_group, w_router, b_router, w_gate, w_up, w_down, ln2_g, ln2_b):
    weights = (w_in, q_norm_g, k_norm_g, w_fnet, w_pool, pool_scale, conv_w, conv_b, dt_bias, a_log,
               d_skip, ssd_norm_g, w_out, ln1_g, ln1_b, w_group, b_group, w_router, b_router,
               w_gate, w_up, w_down, ln2_g, ln2_b)
    params = [_layer_params(l, *weights) for l in range(w_in.shape[0])]
    y_prompt = _trunk(x_prompt, params, _tables(x_prompt.shape[1]))
    y_sample = _trunk(x_sample, params, _tables(x_sample.shape[1]))
    return (y_prompt, y_sample)
```

```python
import functools
import math

import numpy as np
import jax
import jax.numpy as jnp
from jax import lax
from jax.experimental import pallas as pl
from jax.experimental.pallas import tpu as pltpu

F32 = jnp.float32
BF16 = jnp.bfloat16

D_MODEL = 1024
DEPTH = 2
GRID_W = 64
HEAD_DIM = 64
ROPE_THETA = 10000.0
ATT_Q_HEADS = 4
ATT_KV_HEADS = 2
POOL_WINDOWS = (2, 4, 8, 16)
SSD_HEADS = 4
SSD_CONV = 4
SSD_CHUNK = 128
N_GROUPS = 4
EXPERTS_PER_GROUP = 4
N_EXPERTS = 16
EXPERT_FF = 256
DEEPNORM_ALPHA = (2 * DEPTH) ** 0.25
LN_EPS = 1e-5
RMS_EPS = 1e-6

LANES = 128
SUBLANES = 8
HALO = SUBLANES
VMEM_LIMIT = 56 * 1024 * 1024

C_Q, C_K, C_V, C_UF, C_UP, C_Z, C_XBC, C_DT, N_PROJ = 0, 256, 384, 512, 768, 1024, 1280, 1792, 1920
HI = lax.Precision.HIGHEST


def _cparams(sem):
    return pltpu.CompilerParams(dimension_semantics=sem, vmem_limit_bytes=VMEM_LIMIT)


def _dot(a, b):
    return jnp.dot(a, b, preferred_element_type=F32)


def _dot_nt(a, b):
    return lax.dot_general(a, b, (((1,), (1,)), ((), ())), preferred_element_type=F32)


def _silu(x):
    return x * (1.0 / (1.0 + jnp.exp(-x)))


def _layer_norm(x, g, b):
    mu = jnp.mean(x, axis=-1, keepdims=True)
    xc = x - mu
    var = jnp.mean(xc * xc, axis=-1, keepdims=True)
    return xc * lax.rsqrt(var + LN_EPS) * g + b


def _head_sumsq(x, ones_bd):
    sq = x * x
    hi = sq.astype(BF16)
    lo = (sq - hi.astype(F32)).astype(BF16)
    return _dot(hi, ones_bd) + _dot(lo, ones_bd)


def _rope(x, cosv, sinv, first_half):
    w = x.shape[-1]
    partner = jnp.where(first_half, pltpu.roll(x, w - 16, axis=1), pltpu.roll(x, 16, axis=1))
    return x * cosv + partner * sinv


def _in_proj_kernel(x_ref, w_ref, qg_ref, kg_ref, cos_ref, sin_ref, bd_ref, dftc_ref, dtb_ref,
                    q_ref, k_ref, v_ref, ab_ref, up_ref, z_ref, xbc_ref, dt_ref):
    xb = x_ref[...].astype(BF16)
    h = _dot(xb, w_ref[...])
    cosv = cos_ref[...]
    sinv = sin_ref[...]
    bd = bd_ref[...]
    lane = lax.broadcasted_iota(jnp.int32, (1, LANES), 1)
    first_half = (lane & 31) < 16
    for c in range(2):
        qc = h[:, C_Q + c * LANES:C_Q + (c + 1) * LANES]
        ss = _head_sumsq(qc, bd)
        qn = qc * lax.rsqrt(ss * (1.0 / HEAD_DIM) + RMS_EPS) * qg_ref[...]
        qr = _rope(qn, cosv, sinv, first_half) * (HEAD_DIM ** -0.5 * math.log2(math.e))
        q_ref[:, c * LANES:(c + 1) * LANES] = qr.astype(BF16)
    kc = h[:, C_K:C_K + LANES]
    ss = _head_sumsq(kc, bd)
    kn = kc * lax.rsqrt(ss * (1.0 / HEAD_DIM) + RMS_EPS) * kg_ref[...]
    k_ref[...] = _rope(kn, cosv, sinv, first_half).astype(BF16)
    v_ref[...] = h[:, C_V:C_V + LANES].astype(BF16)
    uf = h[:, C_UF:C_UF + 256].astype(BF16)
    ab_ref[...] = _dot(uf, dftc_ref[...])
    up_ref[...] = h[:, C_UP:C_UP + 256]
    z_ref[...] = h[:, C_Z:C_Z + 256]
    xbc_ref[...] = h[:, C_XBC:C_XBC + 512]
    dr = h[:, C_DT:C_DT + LANES] + dtb_ref[...]
    dt_ref[...] = jnp.maximum(dr, 0.0) + jnp.log(1.0 + jnp.exp(-jnp.abs(dr)))


def _in_proj(x, w, qg, kg, cosv, sinv, bd, dftc, dtb, seq):
    t = x.shape[0]
    tm = min(512, seq)
    nseq = seq // tm
    full = lambda a: pl.BlockSpec(a.shape, lambda i: (0,) * a.ndim)
    row = lambda wdt: pl.BlockSpec((tm, wdt), lambda i: (i, 0))
    pos = pl.BlockSpec((tm, LANES), lambda i: (i % nseq, 0))
    outs = [(256, BF16), (128, BF16), (128, BF16), (512, F32), (256, F32), (256, F32), (512, F32),
            (128, F32)]
    return pl.pallas_call(
        _in_proj_kernel,
        grid=(t // tm,),
        in_specs=[row(D_MODEL), full(w), full(qg), full(kg), pos, pos, full(bd), full(dftc), full(dtb)],
        out_specs=[row(wd) for wd, _ in outs],
        out_shape=[jax.ShapeDtypeStruct((t, wd), dt) for wd, dt in outs],
        compiler_params=_cparams(("parallel",)),
        name="in_proj",
    )(x, w, qg, kg, cosv, sinv, bd, dftc, dtb)


ATT_KEY_CHUNK = 256


def _attn_kernel(q_ref, k_ref, v_ref, o_ref, vt_ref):
    nch, _, vrows, tk = vt_ref.shape
    tq = q_ref.shape[0]

    @pl.when(pl.program_id(1) == 0)
    def _():
        for c in range(nch):
            vt = v_ref[c * tk:(c + 1) * tk, :].astype(F32).T.astype(BF16)
            for j in range(ATT_KV_HEADS):
                vt_ref[c, j, 0:HEAD_DIM, :] = vt[j * HEAD_DIM:(j + 1) * HEAD_DIM, :]
                vt_ref[c, j, HEAD_DIM:vrows, :] = jnp.ones((vrows - HEAD_DIM, tk), BF16)

    lane = lax.broadcasted_iota(jnp.int32, (1, LANES), 1)
    qms = []
    for c in range(2):
        qc = q_ref[:, c * LANES:(c + 1) * LANES]
        for j in range(ATT_KV_HEADS):
            qms.append(jnp.where((lane >> 6) == j, qc, jnp.zeros_like(qc)))

    def scores(c):
        kc = k_ref[c * tk:(c + 1) * tk, :]
        return [_dot_nt(kc, qm) for qm in qms]

    ms = [jnp.full((1, tq), -jnp.inf, F32)] * ATT_Q_HEADS
    accs = [jnp.zeros((vrows, tq), F32)] * ATT_Q_HEADS
    st_next = scores(0)
    for c in range(nch):
        sts = st_next
        if c + 1 < nch:
            st_next = scores(c + 1)
        for h in range(ATT_Q_HEADS):
            m_new = jnp.maximum(ms[h], jnp.max(sts[h], axis=0, keepdims=True))
            alpha = jnp.exp2(ms[h] - m_new)
            p = jnp.exp2(sts[h] - m_new).astype(BF16)
            ms[h] = m_new
            accs[h] = alpha * accs[h] + _dot(vt_ref[c, h % ATT_KV_HEADS], p)
    for c in range(2):
        ot = jnp.concatenate([accs[2 * c + j][0:HEAD_DIM] / accs[2 * c + j][HEAD_DIM:HEAD_DIM + 1]
                              for j in range(ATT_KV_HEADS)], axis=0)
        o_ref[:, c * LANES:(c + 1) * LANES] = ot.T.astype(BF16)


def _attention(q, k, v, b, s):
    tq = min(256, s)
    tk = min(ATT_KEY_CHUNK, s)
    q3, k3, v3 = q.reshape(b, s, 256), k.reshape(b, s, LANES), v.reshape(b, s, LANES)
    out = pl.pallas_call(
        _attn_kernel,
        grid=(b, s // tq),
        in_specs=[pl.BlockSpec((None, tq, 256), lambda bi, i: (bi, i, 0)),
                  pl.BlockSpec((None, s, LANES), lambda bi, i: (bi, 0, 0)),
                  pl.BlockSpec((None, s, LANES), lambda bi, i: (bi, 0, 0))],
        out_specs=pl.BlockSpec((None, tq, 256), lambda bi, i: (bi, i, 0)),
        out_shape=jax.ShapeDtypeStruct((b, s, 256), BF16),
        scratch_shapes=[pltpu.VMEM((s // tk, ATT_KV_HEADS, HEAD_DIM + 16, tk), BF16)],
        compiler_params=_cparams(("parallel", "arbitrary")),
        name="attention",
    )(q3, k3, v3)
    return out.reshape(b * s, 256)


FFT_N2 = 64
FFT_K1_GROUP = 8


def _fourier_kernel(a_ref, b_ref, w1_ref, tc_ref, ts_ref, c2_ref, s2_ref, o_ref, yr_ref, yi_ref):
    n1 = a_ref.shape[0]
    w1 = w1_ref[...]
    for n2 in range(FFT_N2):
        v = jnp.concatenate([a_ref[:, n2, :], b_ref[:, n2, :]], axis=1).astype(BF16)
        r = _dot(w1, v)
        yr = r[0:n1, 0:LANES] - r[n1:2 * n1, LANES:2 * LANES]
        ym = r[0:n1, LANES:2 * LANES] + r[n1:2 * n1, 0:LANES]
        tc = tc_ref[n2]
        ts = ts_ref[n2]
        yr_ref[:, n2, :] = yr * tc - ym * ts
        yi_ref[:, n2, :] = -(ym * tc) - yr * ts
    c2 = c2_ref[...]
    s2 = s2_ref[...]
    for k0 in range(0, n1, FFT_K1_GROUP):
        ks = range(k0, k0 + FFT_K1_GROUP)
        yr = jnp.concatenate([yr_ref[k1] for k1 in ks], axis=1).astype(BF16)
        yi = jnp.concatenate([yi_ref[k1] for k1 in ks], axis=1).astype(BF16)
        z = _dot(c2, yr) + _dot(s2, yi)
        for g, k1 in enumerate(ks):
            o_ref[:, k1, :] = z[:, g * LANES:(g + 1) * LANES]


def _fourier(tabs, ab, b, s):
    w1, tc, ts, c2, s2 = tabs
    n1 = s // FFT_N2
    full = lambda a: pl.BlockSpec(a.shape, lambda bi, h: (0,) * a.ndim)
    ab4 = ab.reshape(b, n1, FFT_N2, 512)
    out = pl.pallas_call(
        _fourier_kernel,
        grid=(b, 2),
        in_specs=[pl.BlockSpec((None, n1, FFT_N2, LANES), lambda bi, h: (bi, 0, 0, h)),
                  pl.BlockSpec((None, n1, FFT_N2, LANES), lambda bi, h: (bi, 0, 0, 2 + h)),
                  full(w1), full(tc), full(ts), full(c2), full(s2)],
        out_specs=pl.BlockSpec((None, FFT_N2, n1, LANES), lambda bi, h: (bi, 0, 0, h)),
        out_shape=jax.ShapeDtypeStruct((b, FFT_N2, n1, 256), F32),
        scratch_shapes=[pltpu.VMEM((n1, FFT_N2, LANES), F32)] * 2,
        compiler_params=_cparams(("parallel", "parallel")),
        name="fourier",
    )(ab4, ab4, w1, tc, ts, c2, s2)
    return out.reshape(b * s, 256)


def _halo_specs(tt, width, s):
    nb = tt // HALO
    last = s // HALO - 1
    main = pl.BlockSpec((None, tt, width), lambda bi, i: (bi, i, 0))
    prev = pl.BlockSpec((None, HALO, width), lambda bi, i: (bi, jnp.maximum(i * nb - 1, 0), 0))
    nxt = pl.BlockSpec((None, HALO, width), lambda bi, i: (bi, jnp.minimum((i + 1) * nb, last), 0))
    return [main, prev, nxt]


def _with_halo(main_ref, prev_ref, next_ref):
    i = pl.program_id(1)
    n = pl.num_programs(1)
    prev = jnp.where(i > 0, prev_ref[...], 0.0)
    nxt = jnp.where(i < n - 1, next_ref[...], 0.0)
    return jnp.concatenate([prev, main_ref[...], nxt], axis=0)


def _shifted(ext, d, tt):
    n = ext.shape[0]
    r = ext if d == 0 else pltpu.roll(ext, (-d) % n, axis=0)
    return r[HALO:HALO + tt]


def _pool_kernel(u_ref, up_ref, un_ref, w_ref, sc_ref, o_ref, *, seq):
    tt = u_ref.shape[0]
    ext = _with_halo(u_ref, up_ref, un_ref)
    n = ext.shape[0]
    sh = lambda a, d: a if d == 0 else pltpu.roll(a, (-d) % n, axis=0)
    p2 = sh(ext, -1) + ext
    p4 = sh(p2, -1) + sh(p2, 1)
    p8 = sh(p4, -2) + sh(p4, 2)
    p16 = sh(p8, -4) + sh(p8, 4)
    t = pl.program_id(1) * tt + lax.broadcasted_iota(jnp.int32, (tt, 1), 0)
    grp = lax.broadcasted_iota(jnp.int32, (1, 256), 1) >> 6
    u = u_ref[...]
    pooled = None
    for gi, (w, pw) in enumerate(zip(POOL_WINDOWS, (p2, p4, p8, p16))):
        lo = jnp.clip(t - w // 2, 0, seq - 1)
        hi = jnp.clip(t + (w - w // 2) - 1, 0, seq - 1)
        cnt = (hi - lo + 1).astype(F32)
        val = pw[HALO:HALO + tt] / cnt - u
        pooled = val if pooled is None else jnp.where(grp == gi, val, pooled)
    o_ref[...] = (_dot(pooled.astype(BF16), w_ref[...]) * sc_ref[...]).astype(BF16)


def _pool(u, w_bd, scale, b, s):
    tt = min(1024, s)
    u3 = u.reshape(b, s, 256)
    full = lambda a: pl.BlockSpec(a.shape, lambda bi, i: (0,) * a.ndim)
    out = pl.pallas_call(
        functools.partial(_pool_kernel, seq=s),
        grid=(b, s // tt),
        in_specs=_halo_specs(tt, 256, s) + [full(w_bd), full(scale)],
        out_specs=pl.BlockSpec((None, tt, 256), lambda bi, i: (bi, i, 0)),
        out_shape=jax.ShapeDtypeStruct((b, s, 256), BF16),
        compiler_params=_cparams(("parallel", "parallel")),
        name="pool",
    )(u3, u3, u3, w_bd, scale)
    return out.reshape(b * s, 256)


def _conv_kernel(x_ref, xp_ref, xn_ref, w_ref, b_ref, o_ref):
    tt = x_ref.shape[0]
    ext = _with_halo(x_ref, xp_ref, xn_ref)
    acc = b_ref[...] + _shifted(ext, -2, tt) * w_ref[0:1, :]
    for kk in range(1, SSD_CONV):
        acc = acc + _shifted(ext, kk - 2, tt) * w_ref[kk:kk + 1, :]
    o_ref[...] = _silu(acc)


def _conv(xbc, conv_w, conv_b, b, s):
    tt = min(1024, s)
    x3 = xbc.reshape(b, s, 512)
    full = lambda a: pl.BlockSpec(a.shape, lambda bi, i: (0,) * a.ndim)
    return pl.pallas_call(
        _conv_kernel,
        grid=(b, s // tt),
        in_specs=_halo_specs(tt, 512, s) + [full(conv_w), full(conv_b)],
        out_specs=pl.BlockSpec((None, tt, 512), lambda bi, i: (bi, i, 0)),
        out_shape=jax.ShapeDtypeStruct((b, s, 512), F32),
        compiler_params=_cparams(("parallel", "parallel")),
        name="ssd_conv",
    )(x3, x3, x3, conv_w, conv_b)


def _ssd_kernel(xf_ref, dtf_ref, xb_ref, dtb_ref, alog_ref, dskip_ref, yf_ref, yb_ref, carry_ref):
    @pl.when(pl.program_id(1) == 0)
    def _():
        carry_ref[...] = jnp.zeros_like(carry_ref)

    L = SSD_CHUNK
    nch = xf_ref.shape[0] // L
    a_row = -jnp.exp(alog_ref[...])
    dskip = dskip_ref[...]
    r = lax.broadcasted_iota(jnp.int32, (L, L), 0)
    c = lax.broadcasted_iota(jnp.int32, (L, L), 1)
    keeps = (c <= r, c >= r)
    tris = [k.astype(F32) for k in keeps]
    grp = lax.broadcasted_iota(jnp.int32, (1, LANES), 1) >> 6
    head = lax.broadcasted_iota(jnp.int32, (1, 256), 1) >> 6
    row_grp = lax.broadcasted_iota(jnp.int32, (LANES, 1), 0) >> 6

    units = []
    for ci in range(nch):
        units.append(dict(d=0, rows=slice(ci * L, (ci + 1) * L)))
        units.append(dict(d=1, rows=slice((nch - 1 - ci) * L, (nch - ci) * L)))

    for u in units:
        d = u["d"]
        u["off"] = d * SSD_HEADS
        u["xc"] = (xf_ref, xb_ref)[d][u["rows"], :]
        u["dtc"] = (dtf_ref, dtb_ref)[d][u["rows"], :]
        u["cs"] = jnp.dot(tris[d], u["dtc"] * a_row, precision=HI, preferred_element_type=F32)

    prow = lax.broadcasted_iota(jnp.int32, (LANES, 1), 0)
    spread = [jnp.where((prow < 16) & ((prow & 7) == d * SSD_HEADS + head), 1.0, 0.0).astype(BF16)
              for d in range(2)]
    low8 = lax.broadcasted_iota(jnp.int32, (1, LANES), 1) < 8

    for u in units:
        cs, xc = u["cs"], u["xc"]
        u["cs_t"] = cs.T
        tot = cs[0:1, :] if u["d"] else cs[L - 1:L, :]
        fac = jnp.concatenate([u["dtc"], jnp.exp(tot - cs), jnp.exp(cs)], axis=0)
        head_part = fac.astype(BF16).astype(F32)
        fac_x = _dot(jnp.where(low8, head_part, fac - head_part).astype(BF16), spread[u["d"]])
        u["ds_x"], u["ecs_x"] = fac_x[L:2 * L], fac_x[2 * L:3 * L]
        u["etot_x"] = u["ecs_x"][0:1, :] if u["d"] else u["ecs_x"][L - 1:L, :]
        u["xd"] = xc[:, 0:256] * fac_x[0:L]
        u["xd_b"] = u["xd"].astype(BF16)
        u["b_t"] = xc[:, 256:384].T.astype(BF16)
        cmat = xc[:, 384:512]
        u["g"] = [_dot(jnp.where(grp == g, cmat, 0.0).astype(BF16), u["b_t"]) for g in range(2)]

    for u in units:
        cs, cs_t, off, keep = u["cs"], u["cs_t"], u["off"], keeps[u["d"]]
        y = None
        for h in range(SSD_HEADS):
            diff = cs[:, off + h:off + h + 1] - cs_t[off + h:off + h + 1, :]
            decay = jnp.where(keep, jnp.exp(jnp.where(keep, diff, 0.0)), 0.0)
            sc = (u["g"][h // 2] * decay).astype(BF16)
            yd = _dot(sc, u["xd_b"])
            y = yd if y is None else jnp.where(head == h, yd, y)
        u["y"] = y

    for u in units:
        st = _dot(u["b_t"], (u["xd"] * u["ds_x"]).astype(BF16))
        u["st"] = jnp.where(row_grp == (head >> 1), st, 0.0)

    carries = [carry_ref[0], carry_ref[1]]
    for u in units:
        d, xc = u["d"], u["xc"]
        y_off = _dot(xc[:, 384:512].astype(BF16), carries[d].astype(BF16))
        y = u["y"] + y_off * u["ecs_x"]
        carries[d] = carries[d] * u["etot_x"] + u["st"]
        if d == 0:
            yf_ref[u["rows"], :] = y + xc[:, 0:256] * dskip
        else:
            yb_ref[u["rows"], :] = y
    carry_ref[0] = carries[0]
    carry_ref[1] = carries[1]


def _ssd(xc, dt, alog_row, dskip_row, b, s):
    tt = min(512, s)
    nt = s // tt
    dt3 = dt.reshape(b, s, LANES)
    fwd = lambda w: pl.BlockSpec((None, tt, w), lambda bi, i: (bi, i, 0))
    bwd = lambda w: pl.BlockSpec((None, tt, w), lambda bi, i: (bi, nt - 1 - i, 0))
    full = lambda a: pl.BlockSpec(a.shape, lambda bi, i: (0,) * a.ndim)
    yf, yb = pl.pallas_call(
        _ssd_kernel,
        grid=(b, nt),
        in_specs=[fwd(512), fwd(LANES), bwd(512), bwd(LANES), full(alog_row), full(dskip_row)],
        out_specs=[fwd(256), bwd(256)],
        out_shape=[jax.ShapeDtypeStruct((b, s, 256), F32)] * 2,
        scratch_shapes=[pltpu.VMEM((2, LANES, 256), F32)],
        compiler_params=_cparams(("parallel", "arbitrary")),
        name="ssd_scan",
    )(xc, dt3, xc, dt3, alog_row, dskip_row)
    return yf.reshape(b * s, 256), yb.reshape(b * s, 256)


def _out_proj_kernel(x_ref, att_ref, fmix_ref, pol_ref, yf_ref, yb_ref, z_ref, wf_ref, w_ref, ng_ref,
                     g_ref, b_ref, o_ref):
    fno = _dot(fmix_ref[...].astype(BF16), wf_ref[...]).astype(BF16)
    y = (yf_ref[...] + yb_ref[...]) * _silu(z_ref[...])
    ng = ng_ref[...]
    parts = []
    for gi in range(2):
        yg = y[:, gi * LANES:(gi + 1) * LANES]
        ms = jnp.mean(yg * yg, axis=-1, keepdims=True)
        parts.append((yg * lax.rsqrt(ms + RMS_EPS) * ng[:, gi * LANES:(gi + 1) * LANES]).astype(BF16))
    mix = _dot(att_ref[...], w_ref[0:256, :]) + _dot(fno, w_ref[256:512, :])
    mix = mix + _dot(pol_ref[...], w_ref[512:768, :])
    mix = mix + _dot(parts[0], w_ref[768:896, :]) + _dot(parts[1], w_ref[896:1024, :])
    o_ref[...] = _layer_norm(DEEPNORM_ALPHA * x_ref[...] + mix, g_ref[...], b_ref[...])


def _out_proj(x, att, fmix, pol, yf, yb, z, wf, w, ng, g, bb):
    t = x.shape[0]
    tm = min(512, t)
    row = lambda wd: pl.BlockSpec((tm, wd), lambda i: (i, 0))
    full = lambda a: pl.BlockSpec(a.shape, lambda i: (0,) * a.ndim)
    return pl.pallas_call(
        _out_proj_kernel,
        grid=(t // tm,),
        in_specs=[row(D_MODEL)] + [row(256)] * 6 + [full(wf), full(w), full(ng), full(g), full(bb)],
        out_specs=row(D_MODEL),
        out_shape=jax.ShapeDtypeStruct((t, D_MODEL), F32),
        compiler_params=_cparams(("parallel",)),
        name="out_proj_ln",
    )(x, att, fmix, pol, yf, yb, z, wf, w, ng, g, bb)


MOE_TILE = 1024
MOE_CHUNK = 128
MOE_PERM_ROWS = 256


def _router(x, w2, br):
    xh = x.astype(BF16)
    xl = (x - xh.astype(F32)).astype(BF16)
    l1 = _dot(xh, w2)
    logits = l1[:, 0:LANES] + l1[:, LANES:2 * LANES] + _dot(xl, w2[:, 0:LANES]) + br
    lane = lax.broadcasted_iota(jnp.int32, (1, LANES), 1)
    ninf = -jnp.inf
    gl = jnp.where(lane < N_GROUPS, logits, ninf)
    gmax = jnp.max(gl, axis=-1, keepdims=True)
    g_p = 1.0 / jnp.sum(jnp.exp(gl - gmax), axis=-1, keepdims=True)
    g_idx = jnp.min(jnp.where(gl == gmax, lane, LANES), axis=-1, keepdims=True)
    e_lane = lane - N_GROUPS
    in_grp = (e_lane >= 0) & (e_lane < N_EXPERTS) & ((e_lane >> 2) == g_idx)
    el = jnp.where(in_grp, logits, ninf)
    m1 = jnp.max(el, axis=-1, keepdims=True)
    i1 = jnp.min(jnp.where(el == m1, lane, LANES), axis=-1, keepdims=True)
    el2 = jnp.where(lane == i1, ninf, el)
    m2 = jnp.max(el2, axis=-1, keepdims=True)
    i2 = jnp.min(jnp.where(el2 == m2, lane, LANES), axis=-1, keepdims=True)
    e2 = jnp.exp(m2 - m1)
    w1 = 1.0 / (1.0 + e2)
    gates = g_p * (jnp.where(lane == i1, w1, 0.0) + jnp.where(lane == i2, e2 * w1, 0.0))
    return gates, g_idx


def _moe_kernel(x_ref, wr_ref, br_ref, tri_ref, wg_ref, wu_ref, wd_ref, g_ref, b_ref, o_ref,
                xs_ref, gs_ref, pos_ref, acc_ref, off_ref):
    grp = pl.program_id(1)
    n = x_ref.shape[0]
    lane = lax.broadcasted_iota(jnp.int32, (1, LANES), 1)

    @pl.when(grp == 0)
    def _route_and_sort():
        x = x_ref[...]
        gates, g_idx = _router(x, wr_ref[...], br_ref[...])
        onehot = lane == g_idx
        csum = _dot(tri_ref[...], jnp.where(onehot, 1.0, 0.0).astype(BF16))
        cnt = csum[n - 1:n, :].astype(jnp.int32)
        c0, c1, c2 = cnt[0, 0], cnt[0, 1], cnt[0, 2]
        off_ref[0] = 0
        off_ref[1] = c0
        off_ref[2] = c0 + c1
        off_ref[3] = c0 + c1 + c2
        off_ref[4] = n
        offv = jnp.where(lane == 1, c0, jnp.where(lane == 2, c0 + c1, jnp.where(lane == 3, c0 + c1 + c2, 0)))
        pos = jnp.sum(jnp.where(onehot, csum + offv.astype(F32), 0.0), axis=-1, keepdims=True) - 1.0
        pos_ref[...] = jnp.broadcast_to(pos, (n, LANES))
        pos_row = pos_ref[...].T[0:1, :].astype(jnp.int32)
        xb = x.astype(BF16)
        g_hi = gates.astype(BF16)
        g2 = jnp.concatenate([g_hi, (gates - g_hi.astype(F32)).astype(BF16)], axis=1)
        for r in range(n // MOE_PERM_ROWS):
            rows = lax.broadcasted_iota(jnp.int32, (MOE_PERM_ROWS, n), 0) + r * MOE_PERM_ROWS
            perm = jnp.where(rows == pos_row, 1.0, 0.0).astype(BF16)
            sl = slice(r * MOE_PERM_ROWS, (r + 1) * MOE_PERM_ROWS)
            xs_ref[sl, :] = _dot(perm, xb).astype(BF16)
            gg = _dot(perm, g2)
            gs_ref[sl, :] = gg[:, 0:LANES] + gg[:, LANES:2 * LANES]
        acc_ref[...] = jnp.zeros_like(acc_ref)

    lo = off_ref[grp]
    hi = off_ref[grp + 1]

    def chunk(c, carry):
        r0 = pl.multiple_of(c * MOE_CHUNK, MOE_CHUNK)

        @pl.when((lo < r0 + MOE_CHUNK) & (hi > r0))
        def _():
            rs = pl.ds(r0, MOE_CHUNK)
            xs = xs_ref[rs, :]
            gsc = gs_ref[rs, :]
            hid = _silu(_dot(xs, wg_ref[...])) * _dot(xs, wu_ref[...])
            parts = []
            for e in range(EXPERTS_PER_GROUP):
                ge = jnp.sum(jnp.where(lane == N_GROUPS + EXPERTS_PER_GROUP * grp + e, gsc, 0.0),
                             axis=-1, keepdims=True)
                parts.append((hid[:, e * EXPERT_FF:(e + 1) * EXPERT_FF] * ge).astype(BF16))
            acc_ref[rs, :] += _dot(jnp.concatenate(parts, axis=1), wd_ref[...])

        return carry

    lax.fori_loop(0, n // MOE_CHUNK, chunk, 0)

    @pl.when(grp == N_GROUPS - 1)
    def _unsort_and_norm():
        xs_ref[...] = acc_ref[...].astype(BF16)
        cols = lax.broadcasted_iota(jnp.int32, (MOE_PERM_ROWS, n), 1)
        for r in range(n // MOE_PERM_ROWS):
            sl = slice(r * MOE_PERM_ROWS, (r + 1) * MOE_PERM_ROWS)
            perm_t = jnp.where(cols == pos_ref[sl, 0:1].astype(jnp.int32), 1.0, 0.0).astype(BF16)
            y = _dot(perm_t, xs_ref[...])
            o_ref[sl, :] = _layer_norm(DEEPNORM_ALPHA * x_ref[sl, :] + y, g_ref[...], b_ref[...])


def _moe(x, wr, br, wg, wu, wd, g, bb):
    t = x.shape[0]
    n = min(MOE_TILE, t)
    tri = jnp.asarray(np.tril(np.ones((n, n), np.float32)), dtype=BF16)
    full = lambda a: pl.BlockSpec(a.shape, lambda i, e: (0,) * a.ndim)
    wspec = pl.BlockSpec((None, D_MODEL, D_MODEL), lambda i, e: (e, 0, 0))
    return pl.pallas_call(
        _moe_kernel,
        grid=(t // n, N_GROUPS),
        in_specs=[pl.BlockSpec((n, D_MODEL), lambda i, e: (i, 0)), full(wr), full(br), full(tri),
                  wspec, wspec, wspec, full(g), full(bb)],
        out_specs=pl.BlockSpec((n, D_MODEL), lambda i, e: (i, 0)),
        out_shape=jax.ShapeDtypeStruct((t, D_MODEL), F32),
        scratch_shapes=[pltpu.VMEM((n, D_MODEL), BF16), pltpu.VMEM((n, LANES), F32),
                        pltpu.VMEM((n, LANES), F32), pltpu.VMEM((n, D_MODEL), F32),
                        pltpu.SMEM((8,), jnp.int32)],
        compiler_params=_cparams(("parallel", "arbitrary")),
        name="moe_ln",
    )(x, wr, br, tri, wg, wu, wd, g, bb)


def _rope_tables(seq):
    rows = seq // GRID_W
    row = jnp.repeat(jnp.arange(rows, dtype=F32), GRID_W)
    col = jnp.tile(jnp.arange(GRID_W, dtype=F32), rows)
    half = HEAD_DIM // 2
    freqs = 1.0 / (ROPE_THETA ** (jnp.arange(0, half, 2, dtype=F32) / half))
    ar, ac = row[:, None] * freqs, col[:, None] * freqs
    cosv = jnp.concatenate([jnp.cos(ar), jnp.cos(ar), jnp.cos(ac), jnp.cos(ac)], axis=-1)
    sinv = jnp.concatenate([-jnp.sin(ar), jnp.sin(ar), -jnp.sin(ac), jnp.sin(ac)], axis=-1)
    return jnp.tile(cosv, (1, 2)), jnp.tile(sinv, (1, 2))


def _position_dft(seq):
    n1 = seq // FFT_N2
    dft = lambda n: 2.0 * np.pi * ((np.arange(n)[:, None] * np.arange(n)[None, :]) % n) / n
    a1 = dft(n1)
    w1 = np.concatenate([np.cos(a1), np.sin(a1)], axis=0) * seq ** -0.5
    at = 2.0 * np.pi * (np.arange(FFT_N2)[:, None] * np.arange(n1)[None, :]) / seq
    lanes = lambda t: jnp.asarray(np.repeat(t[:, :, None], LANES, axis=2), dtype=F32)
    a2 = dft(FFT_N2)
    return (jnp.asarray(w1, dtype=BF16), lanes(np.cos(at)), lanes(np.sin(at)),
            jnp.asarray(np.cos(a2), dtype=BF16), jnp.asarray(np.sin(a2), dtype=BF16))


def _channel_dft():
    n = np.arange(HEAD_DIM)
    ang = 2.0 * np.pi * ((n[:, None] * n[None, :]) % HEAD_DIM) / HEAD_DIM
    eye = np.eye(4)
    cb = np.kron(eye, np.cos(ang)) * HEAD_DIM ** -0.5
    sb = np.kron(eye, np.sin(ang)) * HEAD_DIM ** -0.5
    return jnp.asarray(np.concatenate([cb, sb], axis=1), dtype=BF16)


def _layer_params(l, w_in, q_norm_g, k_norm_g, w_fnet, w_pool, pool_scale, conv_w, conv_b, dt_bias,
                  a_log, d_skip, ssd_norm_g, w_out, ln1_g, ln1_b, w_group, b_group, w_router, b_router,
                  w_gate, w_up, w_down, ln2_g, ln2_b):
    wi = w_in[l]
    hperm = np.array([0, 2, 1, 3])
    wq = wi[:, 0:256].reshape(D_MODEL, 4, HEAD_DIM)[:, hperm].reshape(D_MODEL, 256)
    twice = lambda v: jnp.concatenate([v, v], axis=-1)
    w_proj = jnp.concatenate([wq, wi[:, 256:1792], jnp.pad(twice(wi[:, 1792:1800]), ((0, 0), (0, 112)))],
                             axis=1).astype(BF16)
    wo = w_out[l]
    wo_att = wo[0:256].reshape(4, HEAD_DIM, D_MODEL)[hperm].reshape(256, D_MODEL)
    w_o = jnp.concatenate([wo_att, wo[256:]], axis=0).astype(BF16)
    row = lambda v, n: jnp.pad(v.reshape(1, -1), ((0, 0), (0, n - v.size)))
    w_pool_bd = jnp.zeros((256, 256), F32)
    for gi in range(4):
        w_pool_bd = w_pool_bd.at[gi * 64:(gi + 1) * 64, gi * 64:(gi + 1) * 64].set(w_pool[l, gi])
    wr = jnp.pad(jnp.concatenate([w_group[l], w_router[l]], axis=1), ((0, 0), (0, LANES - 20)))
    wr_hi = wr.astype(BF16)
    by_group = lambda w: w.reshape(N_GROUPS, EXPERTS_PER_GROUP, D_MODEL, EXPERT_FF).transpose(
        0, 2, 1, 3).reshape(N_GROUPS, D_MODEL, EXPERTS_PER_GROUP * EXPERT_FF).astype(BF16)
    return dict(
        w_proj=w_proj,
        qg=jnp.tile(q_norm_g[l], 2).reshape(1, LANES), kg=jnp.tile(k_norm_g[l], 2).reshape(1, LANES),
        dtb=row(twice(dt_bias[l].reshape(-1)), LANES), w_fnet=w_fnet[l].astype(BF16),
        w_pool=w_pool_bd.astype(BF16), pool_scale=pool_scale[l].reshape(1, 256),
        conv_w=conv_w[l], conv_b=conv_b[l].reshape(1, 512),
        alog=row(twice(a_log[l].reshape(-1)), LANES), dskip=jnp.repeat(d_skip[l], 64).reshape(1, 256),
        ssd_ng=ssd_norm_g[l].reshape(1, 256), w_o=w_o,
        ln1_g=ln1_g[l].reshape(1, D_MODEL), ln1_b=ln1_b[l].reshape(1, D_MODEL),
        wr=jnp.concatenate([wr_hi, (wr - wr_hi.astype(F32)).astype(BF16)], axis=1),
        br=row(jnp.concatenate([b_group[l], b_router[l]]), LANES),
        wg=by_group(w_gate[l]), wu=by_group(w_up[l]),
        wd=w_down[l].reshape(N_GROUPS, EXPERTS_PER_GROUP * EXPERT_FF, D_MODEL).astype(BF16),
        ln2_g=ln2_g[l].reshape(1, D_MODEL), ln2_b=ln2_b[l].reshape(1, D_MODEL),
    )


def _trunk(x3, params, tables):
    b, s, _ = x3.shape
    x = x3.reshape(b * s, D_MODEL)
    cosv, sinv, pos_dft, bd, dftc = tables
    for p in params:
        q, k, v, ab, up, z, xbc, dt = _in_proj(x, p["w_proj"], p["qg"], p["kg"], cosv, sinv, bd, dftc,
                                               p["dtb"], s)
        att = _attention(q, k, v, b, s)
        fmix = _fourier(pos_dft, ab, b, s)
        pol = _pool(up, p["w_pool"], p["pool_scale"], b, s)
        xc = _conv(xbc, p["conv_w"], p["conv_b"], b, s)
        yf, yb = _ssd(xc, dt, p["alog"], p["dskip"], b, s)
        x = _out_proj(x, att, fmix, pol, yf, yb, z, p["w_fnet"], p["w_o"], p["ssd_ng"], p["ln1_g"],
                      p["ln1_b"])
        x = _moe(x, p["wr"], p["br"], p["wg"], p["wu"], p["wd"], p["ln2_g"], p["ln2_b"])
    return x.reshape(b, s, D_MODEL)


def _tables(seq):
    cosv, sinv = _rope_tables(seq)
    lane = np.arange(LANES)
    bd = jnp.asarray((lane[:, None] // HEAD_DIM) == (lane[None, :] // HEAD_DIM), dtype=BF16)
    return cosv, sinv, _position_dft(seq), bd, _channel_dft()


def kernel(x_prompt, x_sample, w_in, q_norm_g, k_norm_g, w_fnet, w_pool, pool_scale, conv_w, conv_b, dt_bias, a_log, d_skip, ssd_norm_g, w_out, ln1_g, ln1_b, w_group, b_group, w_router, b_router, w_gate, w_up, w_down, ln2_g, ln2_b):
    weights = (w_in, q_norm_g, k_norm_g, w_fnet, w_pool, pool_scale, conv_w, conv_b, dt_bias, a_log,
               d_skip, ssd_norm_g, w_out, ln1_g, ln1_b, w_group, b_group, w_router, b_router,
               w_gate, w_up, w_down, ln2_g, ln2_b)
    params = [_layer_params(l, *weights) for l in range(w_in.shape[0])]
    y_prompt = _trunk(x_prompt, params, _tables(x_prompt.shape[1]))
    y_sample = _trunk(x_sample, params, _tables(x_sample.shape[1]))
    return (y_prompt, y_sample)
```

```python
import functools
import math

import numpy as np
import jax
import jax.numpy as jnp
from jax import lax
from jax.experimental import pallas as pl
from jax.experimental.pallas import tpu as pltpu

F32 = jnp.float32
BF16 = jnp.bfloat16

D_MODEL = 1024
DEPTH = 2
GRID_W = 64
HEAD_DIM = 64
ROPE_THETA = 10000.0
ATT_Q_HEADS = 4
ATT_KV_HEADS = 2
POOL_WINDOWS = (2, 4, 8, 16)
SSD_HEADS = 4
SSD_CONV = 4
SSD_CHUNK = 128
N_GROUPS = 4
EXPERTS_PER_GROUP = 4
N_EXPERTS = 16
EXPERT_FF = 256
DEEPNORM_ALPHA = (2 * DEPTH) ** 0.25
LN_EPS = 1e-5
RMS_EPS = 1e-6

LANES = 128
SUBLANES = 8
HALO = SUBLANES
VMEM_LIMIT = 56 * 1024 * 1024

C_Q, C_K, C_V, C_UF, C_UP, C_Z, C_XBC, C_DT, N_PROJ = 0, 256, 384, 512, 768, 1024, 1280, 1792, 1920
HI = lax.Precision.HIGHEST


def _cparams(sem):
    return pltpu.CompilerParams(dimension_semantics=sem, vmem_limit_bytes=VMEM_LIMIT)


def _dot(a, b):
    return jnp.dot(a, b, preferred_element_type=F32)


def _dot_nt(a, b):
    return lax.dot_general(a, b, (((1,), (1,)), ((), ())), preferred_element_type=F32)


def _silu(x):
    return x * (1.0 / (1.0 + jnp.exp(-x)))


def _layer_norm(x, g, b):
    mu = jnp.mean(x, axis=-1, keepdims=True)
    xc = x - mu
    var = jnp.mean(xc * xc, axis=-1, keepdims=True)
    return xc * lax.rsqrt(var + LN_EPS) * g + b


def _head_sumsq(x, ones_bd):
    sq = x * x
    hi = sq.astype(BF16)
    lo = (sq - hi.astype(F32)).astype(BF16)
    return _dot(jnp.concatenate([hi, lo], axis=1), ones_bd)


def _rope(x, cosv, sinv, first_half):
    w = x.shape[-1]
    partner = jnp.where(first_half, pltpu.roll(x, w - 16, axis=1), pltpu.roll(x, 16, axis=1))
    return x * cosv + partner * sinv


IN_PROJ_SUB = 512


def _in_proj_kernel(x_ref, w_ref, qg_ref, kg_ref, cos_ref, sin_ref, bd_ref, dftc_ref, dtb_ref,
                    q_ref, k_ref, v_ref, ab_ref, up_ref, z_ref, xbc_ref, dt_ref):
    bd = bd_ref[...]
    lane = lax.broadcasted_iota(jnp.int32, (1, LANES), 1)
    first_half = (lane & 31) < 16
    sub = min(IN_PROJ_SUB, x_ref.shape[0])
    nsub = x_ref.shape[0] // sub

    def project(i):
        return _dot(x_ref[i * sub:(i + 1) * sub, :].astype(BF16), w_ref[...])

    h_next = project(0)
    for i in range(nsub):
        h = h_next
        if i + 1 < nsub:
            h_next = project(i + 1)
        rows = slice(i * sub, (i + 1) * sub)
        cosv = cos_ref[rows, :]
        sinv = sin_ref[rows, :]
        for c in range(2):
            qc = h[:, C_Q + c * LANES:C_Q + (c + 1) * LANES]
            ss = _head_sumsq(qc, bd)
            qn = qc * lax.rsqrt(ss * (1.0 / HEAD_DIM) + RMS_EPS) * qg_ref[...]
            qr = _rope(qn, cosv, sinv, first_half) * (HEAD_DIM ** -0.5 * math.log2(math.e))
            q_ref[rows, c * LANES:(c + 1) * LANES] = qr.astype(BF16)
        kc = h[:, C_K:C_K + LANES]
        ss = _head_sumsq(kc, bd)
        kn = kc * lax.rsqrt(ss * (1.0 / HEAD_DIM) + RMS_EPS) * kg_ref[...]
        k_ref[rows, :] = _rope(kn, cosv, sinv, first_half).astype(BF16)
        v_ref[rows, :] = h[:, C_V:C_V + LANES].astype(BF16)
        uf = h[:, C_UF:C_UF + 256].astype(BF16)
        ab_ref[rows, :] = _dot(uf, dftc_ref[...])
        up_ref[rows, :] = h[:, C_UP:C_UP + 256]
        z_ref[rows, :] = h[:, C_Z:C_Z + 256]
        xbc_ref[rows, :] = h[:, C_XBC:C_XBC + 512]
        dr = h[:, C_DT:C_DT + LANES] + dtb_ref[...]
        dt_ref[rows, :] = jnp.maximum(dr, 0.0) + jnp.log(1.0 + jnp.exp(-jnp.abs(dr)))


def _in_proj(x, w, qg, kg, cosv, sinv, bd, dftc, dtb, seq):
    t = x.shape[0]
    tm = min(1024, seq)
    nseq = seq // tm
    full = lambda a: pl.BlockSpec(a.shape, lambda i: (0,) * a.ndim)
    row = lambda wdt: pl.BlockSpec((tm, wdt), lambda i: (i, 0))
    pos = pl.BlockSpec((tm, LANES), lambda i: (i % nseq, 0))
    outs = [(256, BF16), (128, BF16), (128, BF16), (512, F32), (256, F32), (256, F32), (512, F32),
            (128, F32)]
    return pl.pallas_call(
        _in_proj_kernel,
        grid=(t // tm,),
        in_specs=[row(D_MODEL), full(w), full(qg), full(kg), pos, pos, full(bd), full(dftc), full(dtb)],
        out_specs=[row(wd) for wd, _ in outs],
        out_shape=[jax.ShapeDtypeStruct((t, wd), dt) for wd, dt in outs],
        compiler_params=_cparams(("parallel",)),
        name="in_proj",
    )(x, w, qg, kg, cosv, sinv, bd, dftc, dtb)


ATT_KEY_CHUNK = 256


def _attn_kernel(q_ref, k_ref, v_ref, o_ref, vt_ref):
    nch, _, vrows, tk = vt_ref.shape
    tq = q_ref.shape[0]

    @pl.when(pl.program_id(1) == 0)
    def _():
        for c in range(nch):
            vt = v_ref[c * tk:(c + 1) * tk, :].astype(F32).T.astype(BF16)
            for j in range(ATT_KV_HEADS):
                vt_ref[c, j, 0:HEAD_DIM, :] = vt[j * HEAD_DIM:(j + 1) * HEAD_DIM, :]
                vt_ref[c, j, HEAD_DIM:vrows, :] = jnp.ones((vrows - HEAD_DIM, tk), BF16)

    lane = lax.broadcasted_iota(jnp.int32, (1, LANES), 1)
    qms = []
    for c in range(2):
        qc = q_ref[:, c * LANES:(c + 1) * LANES]
        for j in range(ATT_KV_HEADS):
            qms.append(jnp.where((lane >> 6) == j, qc, jnp.zeros_like(qc)))

    def scores(c):
        kc = k_ref[c * tk:(c + 1) * tk, :]
        return [_dot_nt(kc, qm) for qm in qms]

    ms = [jnp.full((1, tq), -jnp.inf, F32)] * ATT_Q_HEADS
    accs = [jnp.zeros((vrows, tq), F32)] * ATT_Q_HEADS
    st_next = scores(0)
    for c in range(nch):
        sts = st_next
        if c + 1 < nch:
            st_next = scores(c + 1)
        for h in range(ATT_Q_HEADS):
            m_new = jnp.maximum(ms[h], jnp.max(sts[h], axis=0, keepdims=True))
            alpha = jnp.exp2(ms[h] - m_new)
            p = jnp.exp2(sts[h] - m_new).astype(BF16)
            ms[h] = m_new
            accs[h] = alpha * accs[h] + _dot(vt_ref[c, h % ATT_KV_HEADS], p)
    for c in range(2):
        ot = jnp.concatenate([accs[2 * c + j][0:HEAD_DIM] / accs[2 * c + j][HEAD_DIM:HEAD_DIM + 1]
                              for j in range(ATT_KV_HEADS)], axis=0)
        o_ref[:, c * LANES:(c + 1) * LANES] = ot.T.astype(BF16)


def _attention(q, k, v, b, s):
    tq = min(256, s)
    tk = min(ATT_KEY_CHUNK, s)
    q3, k3, v3 = q.reshape(b, s, 256), k.reshape(b, s, LANES), v.reshape(b, s, LANES)
    out = pl.pallas_call(
        _attn_kernel,
        grid=(b, s // tq),
        in_specs=[pl.BlockSpec((None, tq, 256), lambda bi, i: (bi, i, 0)),
                  pl.BlockSpec((None, s, LANES), lambda bi, i: (bi, 0, 0)),
                  pl.BlockSpec((None, s, LANES), lambda bi, i: (bi, 0, 0))],
        out_specs=pl.BlockSpec((None, tq, 256), lambda bi, i: (bi, i, 0)),
        out_shape=jax.ShapeDtypeStruct((b, s, 256), BF16),
        scratch_shapes=[pltpu.VMEM((s // tk, ATT_KV_HEADS, HEAD_DIM + 16, tk), BF16)],
        compiler_params=_cparams(("parallel", "arbitrary")),
        name="attention",
    )(q3, k3, v3)
    return out.reshape(b * s, 256)


FFT_N2 = 64
FFT_K1_GROUP = 8


def _fourier_kernel(a_ref, b_ref, w1_ref, tc_ref, ts_ref, cs2_ref, o_ref, yr_ref, yi_ref):
    n1 = a_ref.shape[0]
    w1 = w1_ref[...]
    for n2 in range(FFT_N2):
        v = jnp.concatenate([a_ref[:, n2, :], b_ref[:, n2, :]], axis=1).astype(BF16)
        r = _dot(w1, v)
        yr = r[0:n1, 0:LANES] - r[n1:2 * n1, LANES:2 * LANES]
        ym = r[0:n1, LANES:2 * LANES] + r[n1:2 * n1, 0:LANES]
        tc = tc_ref[n2]
        ts = ts_ref[n2]
        yr_ref[:, n2, :] = yr * tc - ym * ts
        yi_ref[:, n2, :] = -(ym * tc) - yr * ts
    cs2 = cs2_ref[...]
    for k0 in range(0, n1, FFT_K1_GROUP):
        ks = range(k0, k0 + FFT_K1_GROUP)
        yr = jnp.concatenate([yr_ref[k1] for k1 in ks], axis=1)
        yi = jnp.concatenate([yi_ref[k1] for k1 in ks], axis=1)
        z = _dot(cs2, jnp.concatenate([yr, yi], axis=0).astype(BF16))
        for g, k1 in enumerate(ks):
            o_ref[:, k1, :] = z[:, g * LANES:(g + 1) * LANES]


def _fourier(tabs, ab, b, s):
    w1, tc, ts, cs2 = tabs
    n1 = s // FFT_N2
    full = lambda a: pl.BlockSpec(a.shape, lambda bi, h: (0,) * a.ndim)
    ab4 = ab.reshape(b, n1, FFT_N2, 512)
    out = pl.pallas_call(
        _fourier_kernel,
        grid=(b, 2),
        in_specs=[pl.BlockSpec((None, n1, FFT_N2, LANES), lambda bi, h: (bi, 0, 0, h)),
                  pl.BlockSpec((None, n1, FFT_N2, LANES), lambda bi, h: (bi, 0, 0, 2 + h)),
                  full(w1), full(tc), full(ts), full(cs2)],
        out_specs=pl.BlockSpec((None, FFT_N2, n1, LANES), lambda bi, h: (bi, 0, 0, h)),
        out_shape=jax.ShapeDtypeStruct((b, FFT_N2, n1, 256), F32),
        scratch_shapes=[pltpu.VMEM((n1, FFT_N2, LANES), F32)] * 2,
        compiler_params=_cparams(("parallel", "parallel")),
        name="fourier",
    )(ab4, ab4, w1, tc, ts, cs2)
    return out.reshape(b * s, 256)


def _halo_specs(tt, width, s):
    nb = tt // HALO
    last = s // HALO - 1
    main = pl.BlockSpec((None, tt, width), lambda bi, i: (bi, i, 0))
    prev = pl.BlockSpec((None, HALO, width), lambda bi, i: (bi, jnp.maximum(i * nb - 1, 0), 0))
    nxt = pl.BlockSpec((None, HALO, width), lambda bi, i: (bi, jnp.minimum((i + 1) * nb, last), 0))
    return [main, prev, nxt]


def _with_halo(main_ref, prev_ref, next_ref):
    i = pl.program_id(1)
    n = pl.num_programs(1)
    prev = jnp.where(i > 0, prev_ref[...], 0.0)
    nxt = jnp.where(i < n - 1, next_ref[...], 0.0)
    return jnp.concatenate([prev, main_ref[...], nxt], axis=0)


def _shifted(ext, d, tt):
    n = ext.shape[0]
    r = ext if d == 0 else pltpu.roll(ext, (-d) % n, axis=0)
    return r[HALO:HALO + tt]


def _pool_kernel(u_ref, up_ref, un_ref, w_ref, sc_ref, o_ref, *, seq):
    tt = u_ref.shape[0]
    ext = _with_halo(u_ref, up_ref, un_ref)
    n = ext.shape[0]
    sh = lambda a, d: a if d == 0 else pltpu.roll(a, (-d) % n, axis=0)
    p2 = sh(ext, -1) + ext
    p4 = sh(p2, -1) + sh(p2, 1)
    p8 = sh(p4, -2) + sh(p4, 2)
    p16 = sh(p8, -4) + sh(p8, 4)
    t = pl.program_id(1) * tt + lax.broadcasted_iota(jnp.int32, (tt, 1), 0)
    grp = lax.broadcasted_iota(jnp.int32, (1, 256), 1) >> 6
    u = u_ref[...]
    pooled = None
    for gi, (w, pw) in enumerate(zip(POOL_WINDOWS, (p2, p4, p8, p16))):
        lo = jnp.clip(t - w // 2, 0, seq - 1)
        hi = jnp.clip(t + (w - w // 2) - 1, 0, seq - 1)
        cnt = (hi - lo + 1).astype(F32)
        val = pw[HALO:HALO + tt] / cnt - u
        pooled = val if pooled is None else jnp.where(grp == gi, val, pooled)
    o_ref[...] = (_dot(pooled.astype(BF16), w_ref[...]) * sc_ref[...]).astype(BF16)


def _pool(u, w_bd, scale, b, s):
    tt = min(1024, s)
    u3 = u.reshape(b, s, 256)
    full = lambda a: pl.BlockSpec(a.shape, lambda bi, i: (0,) * a.ndim)
    out = pl.pallas_call(
        functools.partial(_pool_kernel, seq=s),
        grid=(b, s // tt),
        in_specs=_halo_specs(tt, 256, s) + [full(w_bd), full(scale)],
        out_specs=pl.BlockSpec((None, tt, 256), lambda bi, i: (bi, i, 0)),
        out_shape=jax.ShapeDtypeStruct((b, s, 256), BF16),
        compiler_params=_cparams(("parallel", "parallel")),
        name="pool",
    )(u3, u3, u3, w_bd, scale)
    return out.reshape(b * s, 256)


def _conv_kernel(x_ref, xp_ref, xn_ref, w_ref, b_ref, o_ref):
    tt = x_ref.shape[0]
    ext = _with_halo(x_ref, xp_ref, xn_ref)
    acc = b_ref[...] + _shifted(ext, -2, tt) * w_ref[0:1, :]
    for kk in range(1, SSD_CONV):
        acc = acc + _shifted(ext, kk - 2, tt) * w_ref[kk:kk + 1, :]
    o_ref[...] = _silu(acc)


def _conv(xbc, conv_w, conv_b, b, s):
    tt = min(1024, s)
    x3 = xbc.reshape(b, s, 512)
    full = lambda a: pl.BlockSpec(a.shape, lambda bi, i: (0,) * a.ndim)
    return pl.pallas_call(
        _conv_kernel,
        grid=(b, s // tt),
        in_specs=_halo_specs(tt, 512, s) + [full(conv_w), full(conv_b)],
        out_specs=pl.BlockSpec((None, tt, 512), lambda bi, i: (bi, i, 0)),
        out_shape=jax.ShapeDtypeStruct((b, s, 512), F32),
        compiler_params=_cparams(("parallel", "parallel")),
        name="ssd_conv",
    )(x3, x3, x3, conv_w, conv_b)


def _ssd_kernel(xf_ref, dtf_ref, xb_ref, dtb_ref, alog_ref, dskip_ref, yf_ref, yb_ref, carry_ref):
    @pl.when(pl.program_id(1) == 0)
    def _():
        carry_ref[...] = jnp.zeros_like(carry_ref)

    L = SSD_CHUNK
    nch = xf_ref.shape[0] // L
    a_row = -jnp.exp(alog_ref[...])
    dskip = dskip_ref[...]
    r = lax.broadcasted_iota(jnp.int32, (L, L), 0)
    c = lax.broadcasted_iota(jnp.int32, (L, L), 1)
    keeps = (c <= r, c >= r)
    tris = [k.astype(F32) for k in keeps]
    grp = lax.broadcasted_iota(jnp.int32, (1, LANES), 1) >> 6
    head = lax.broadcasted_iota(jnp.int32, (1, 256), 1) >> 6
    row_grp = lax.broadcasted_iota(jnp.int32, (LANES, 1), 0) >> 6

    units = []
    for ci in range(nch):
        units.append(dict(d=0, rows=slice(ci * L, (ci + 1) * L)))
        units.append(dict(d=1, rows=slice((nch - 1 - ci) * L, (nch - ci) * L)))

    for u in units:
        d = u["d"]
        u["off"] = d * SSD_HEADS
        u["xc"] = (xf_ref, xb_ref)[d][u["rows"], :]
        u["dtc"] = (dtf_ref, dtb_ref)[d][u["rows"], :]
        u["cs"] = jnp.dot(tris[d], u["dtc"] * a_row, precision=HI, preferred_element_type=F32)

    prow = lax.broadcasted_iota(jnp.int32, (LANES, 1), 0)
    spread = [jnp.where((prow < 16) & ((prow & 7) == d * SSD_HEADS + head), 1.0, 0.0).astype(BF16)
              for d in range(2)]
    low8 = lax.broadcasted_iota(jnp.int32, (1, LANES), 1) < 8

    for u in units:
        cs, xc = u["cs"], u["xc"]
        u["cs_t"] = cs.T
        tot = cs[0:1, :] if u["d"] else cs[L - 1:L, :]
        fac = jnp.concatenate([u["dtc"], jnp.exp(tot - cs), jnp.exp(cs)], axis=0)
        head_part = fac.astype(BF16).astype(F32)
        fac_x = _dot(jnp.where(low8, head_part, fac - head_part).astype(BF16), spread[u["d"]])
        u["ds_x"], u["ecs_x"] = fac_x[L:2 * L], fac_x[2 * L:3 * L]
        u["etot_x"] = u["ecs_x"][0:1, :] if u["d"] else u["ecs_x"][L - 1:L, :]
        u["xd"] = xc[:, 0:256] * fac_x[0:L]
        u["xd_b"] = u["xd"].astype(BF16)
        u["b_t"] = xc[:, 256:384].T.astype(BF16)
        cmat = xc[:, 384:512]
        u["g"] = [_dot(jnp.where(grp == g, cmat, 0.0).astype(BF16), u["b_t"]) for g in range(2)]

    for u in units:
        cs, cs_t, off, keep = u["cs"], u["cs_t"], u["off"], keeps[u["d"]]
        y = None
        for h in range(SSD_HEADS):
            diff = cs[:, off + h:off + h + 1] - cs_t[off + h:off + h + 1, :]
            decay = jnp.where(keep, jnp.exp(jnp.where(keep, diff, 0.0)), 0.0)
            sc = (u["g"][h // 2] * decay).astype(BF16)
            yd = _dot(sc, u["xd_b"])
            y = yd if y is None else jnp.where(head == h, yd, y)
        u["y"] = y

    for u in units:
        st = _dot(u["b_t"], (u["xd"] * u["ds_x"]).astype(BF16))
        u["st"] = jnp.where(row_grp == (head >> 1), st, 0.0)

    carries = [carry_ref[0], carry_ref[1]]
    for u in units:
        d, xc = u["d"], u["xc"]
        y_off = _dot(xc[:, 384:512].astype(BF16), carries[d].astype(BF16))
        y = u["y"] + y_off * u["ecs_x"]
        carries[d] = carries[d] * u["etot_x"] + u["st"]
        if d == 0:
            yf_ref[u["rows"], :] = y + xc[:, 0:256] * dskip
        else:
            yb_ref[u["rows"], :] = y
    carry_ref[0] = carries[0]
    carry_ref[1] = carries[1]


def _ssd(xc, dt, alog_row, dskip_row, b, s):
    tt = min(512, s)
    nt = s // tt
    dt3 = dt.reshape(b, s, LANES)
    fwd = lambda w: pl.BlockSpec((None, tt, w), lambda bi, i: (bi, i, 0))
    bwd = lambda w: pl.BlockSpec((None, tt, w), lambda bi, i: (bi, nt - 1 - i, 0))
    full = lambda a: pl.BlockSpec(a.shape, lambda bi, i: (0,) * a.ndim)
    yf, yb = pl.pallas_call(
        _ssd_kernel,
        grid=(b, nt),
        in_specs=[fwd(512), fwd(LANES), bwd(512), bwd(LANES), full(alog_row), full(dskip_row)],
        out_specs=[fwd(256), bwd(256)],
        out_shape=[jax.ShapeDtypeStruct((b, s, 256), F32)] * 2,
        scratch_shapes=[pltpu.VMEM((2, LANES, 256), F32)],
        compiler_params=_cparams(("parallel", "arbitrary")),
        name="ssd_scan",
    )(xc, dt3, xc, dt3, alog_row, dskip_row)
    return yf.reshape(b * s, 256), yb.reshape(b * s, 256)


OUT_PROJ_SUB = 256


def _out_proj_kernel(x_ref, att_ref, fmix_ref, pol_ref, yf_ref, yb_ref, z_ref, wf_ref, w_ref, ng_ref,
                     g_ref, b_ref, o_ref):
    ng = ng_ref[...]
    sub = min(OUT_PROJ_SUB, x_ref.shape[0])

    def mixed(i):
        rows = slice(i * sub, (i + 1) * sub)
        fno = _dot(fmix_ref[rows, :].astype(BF16), wf_ref[...]).astype(BF16)
        y = (yf_ref[rows, :] + yb_ref[rows, :]) * _silu(z_ref[rows, :])
        parts = []
        for gi in range(2):
            yg = y[:, gi * LANES:(gi + 1) * LANES]
            ms = jnp.mean(yg * yg, axis=-1, keepdims=True)
            parts.append((yg * lax.rsqrt(ms + RMS_EPS) * ng[:, gi * LANES:(gi + 1) * LANES]).astype(BF16))
        return _dot(jnp.concatenate([att_ref[rows, :], fno, pol_ref[rows, :]] + parts, axis=1), w_ref[...])

    nsub = x_ref.shape[0] // sub
    mix_next = mixed(0)
    for i in range(nsub):
        mix = mix_next
        if i + 1 < nsub:
            mix_next = mixed(i + 1)
        rows = slice(i * sub, (i + 1) * sub)
        o_ref[rows, :] = _layer_norm(DEEPNORM_ALPHA * x_ref[rows, :] + mix, g_ref[...], b_ref[...])


def _out_proj(x, att, fmix, pol, yf, yb, z, wf, w, ng, g, bb):
    t = x.shape[0]
    tm = min(1024, t)
    row = lambda wd: pl.BlockSpec((tm, wd), lambda i: (i, 0))
    full = lambda a: pl.BlockSpec(a.shape, lambda i: (0,) * a.ndim)
    return pl.pallas_call(
        _out_proj_kernel,
        grid=(t // tm,),
        in_specs=[row(D_MODEL)] + [row(256)] * 6 + [full(wf), full(w), full(ng), full(g), full(bb)],
        out_specs=row(D_MODEL),
        out_shape=jax.ShapeDtypeStruct((t, D_MODEL), F32),
        compiler_params=_cparams(("parallel",)),
        name="out_proj_ln",
    )(x, att, fmix, pol, yf, yb, z, wf, w, ng, g, bb)


MOE_TILE = 1024
MOE_CHUNK = 128
MOE_PERM_ROWS = 256


def _router(x, w2, br):
    xh = x.astype(BF16)
    xl = (x - xh.astype(F32)).astype(BF16)
    l1 = _dot(xh, w2)
    logits = l1[:, 0:LANES] + l1[:, LANES:2 * LANES] + _dot(xl, w2[:, 0:LANES]) + br
    lane = lax.broadcasted_iota(jnp.int32, (1, LANES), 1)
    ninf = -jnp.inf
    gl = jnp.where(lane < N_GROUPS, logits, ninf)
    gmax = jnp.max(gl, axis=-1, keepdims=True)
    g_p = 1.0 / jnp.sum(jnp.exp(gl - gmax), axis=-1, keepdims=True)
    g_idx = jnp.min(jnp.where(gl == gmax, lane, LANES), axis=-1, keepdims=True)
    e_lane = lane - N_GROUPS
    in_grp = (e_lane >= 0) & (e_lane < N_EXPERTS) & ((e_lane >> 2) == g_idx)
    el = jnp.where(in_grp, logits, ninf)
    m1 = jnp.max(el, axis=-1, keepdims=True)
    i1 = jnp.min(jnp.where(el == m1, lane, LANES), axis=-1, keepdims=True)
    el2 = jnp.where(lane == i1, ninf, el)
    m2 = jnp.max(el2, axis=-1, keepdims=True)
    i2 = jnp.min(jnp.where(el2 == m2, lane, LANES), axis=-1, keepdims=True)
    e2 = jnp.exp(m2 - m1)
    w1 = 1.0 / (1.0 + e2)
    gates = g_p * (jnp.where(lane == i1, w1, 0.0) + jnp.where(lane == i2, e2 * w1, 0.0))
    return gates, g_idx


def _moe_kernel(x_ref, wr_ref, br_ref, tri_ref, wg_ref, wu_ref, wd_ref, g_ref, b_ref, o_ref,
                xs_ref, gs_ref, pos_ref, acc_ref, off_ref):
    grp = pl.program_id(1)
    n = x_ref.shape[0]
    lane = lax.broadcasted_iota(jnp.int32, (1, LANES), 1)

    @pl.when(grp == 0)
    def _route_and_sort():
        x = x_ref[...]
        gates, g_idx = _router(x, wr_ref[...], br_ref[...])
        onehot = lane == g_idx
        csum = _dot(tri_ref[...], jnp.where(onehot, 1.0, 0.0).astype(BF16))
        cnt = csum[n - 1:n, :].astype(jnp.int32)
        c0, c1, c2 = cnt[0, 0], cnt[0, 1], cnt[0, 2]
        off_ref[0] = 0
        off_ref[1] = c0
        off_ref[2] = c0 + c1
        off_ref[3] = c0 + c1 + c2
        off_ref[4] = n
        offv = jnp.where(lane == 1, c0, jnp.where(lane == 2, c0 + c1, jnp.where(lane == 3, c0 + c1 + c2, 0)))
        pos = jnp.sum(jnp.where(onehot, csum + offv.astype(F32), 0.0), axis=-1, keepdims=True) - 1.0
        pos_ref[...] = jnp.broadcast_to(pos, (n, LANES))
        pos_row = pos_ref[...].T[0:1, :].astype(jnp.int32)
        xb = x.astype(BF16)
        g_hi = gates.astype(BF16)
        g2 = jnp.concatenate([g_hi, (gates - g_hi.astype(F32)).astype(BF16)], axis=1)
        for r in range(n // MOE_PERM_ROWS):
            rows = lax.broadcasted_iota(jnp.int32, (MOE_PERM_ROWS, n), 0) + r * MOE_PERM_ROWS
            perm = jnp.where(rows == pos_row, 1.0, 0.0).astype(BF16)
            sl = slice(r * MOE_PERM_ROWS, (r + 1) * MOE_PERM_ROWS)
            xs_ref[sl, :] = _dot(perm, xb).astype(BF16)
            gg = _dot(perm, g2)
            gs_ref[sl, :] = gg[:, 0:LANES] + gg[:, LANES:2 * LANES]
        acc_ref[...] = jnp.zeros_like(acc_ref)

    lo = off_ref[grp]
    hi = off_ref[grp + 1]

    def chunk(c, carry):
        r0 = pl.multiple_of(c * MOE_CHUNK, MOE_CHUNK)

        @pl.when((lo < r0 + MOE_CHUNK) & (hi > r0))
        def _():
            rs = pl.ds(r0, MOE_CHUNK)
            xs = xs_ref[rs, :]
            gsc = gs_ref[rs, :]
            hid = _silu(_dot(xs, wg_ref[...])) * _dot(xs, wu_ref[...])
            parts = []
            for e in range(EXPERTS_PER_GROUP):
                ge = jnp.sum(jnp.where(lane == N_GROUPS + EXPERTS_PER_GROUP * grp + e, gsc, 0.0),
                             axis=-1, keepdims=True)
                parts.append((hid[:, e * EXPERT_FF:(e + 1) * EXPERT_FF] * ge).astype(BF16))
            acc_ref[rs, :] += _dot(jnp.concatenate(parts, axis=1), wd_ref[...])

        return carry

    lax.fori_loop(0, n // MOE_CHUNK, chunk, 0)

    @pl.when(grp == N_GROUPS - 1)
    def _unsort_and_norm():
        xs_ref[...] = acc_ref[...].astype(BF16)
        cols = lax.broadcasted_iota(jnp.int32, (MOE_PERM_ROWS, n), 1)
        for r in range(n // MOE_PERM_ROWS):
            sl = slice(r * MOE_PERM_ROWS, (r + 1) * MOE_PERM_ROWS)
            perm_t = jnp.where(cols == pos_ref[sl, 0:1].astype(jnp.int32), 1.0, 0.0).astype(BF16)
            y = _dot(perm_t, xs_ref[...])
            o_ref[sl, :] = _layer_norm(DEEPNORM_ALPHA * x_ref[sl, :] + y, g_ref[...], b_ref[...])


def _moe(x, wr, br, wg, wu, wd, g, bb):
    t = x.shape[0]
    n = min(MOE_TILE, t)
    tri = jnp.asarray(np.tril(np.ones((n, n), np.float32)), dtype=BF16)
    full = lambda a: pl.BlockSpec(a.shape, lambda i, e: (0,) * a.ndim)
    wspec = pl.BlockSpec((None, D_MODEL, D_MODEL), lambda i, e: (e, 0, 0))
    return pl.pallas_call(
        _moe_kernel,
        grid=(t // n, N_GROUPS),
        in_specs=[pl.BlockSpec((n, D_MODEL), lambda i, e: (i, 0)), full(wr), full(br), full(tri),
                  wspec, wspec, wspec, full(g), full(bb)],
        out_specs=pl.BlockSpec((n, D_MODEL), lambda i, e: (i, 0)),
        out_shape=jax.ShapeDtypeStruct((t, D_MODEL), F32),
        scratch_shapes=[pltpu.VMEM((n, D_MODEL), BF16), pltpu.VMEM((n, LANES), F32),
                        pltpu.VMEM((n, LANES), F32), pltpu.VMEM((n, D_MODEL), F32),
                        pltpu.SMEM((8,), jnp.int32)],
        compiler_params=_cparams(("parallel", "arbitrary")),
        name="moe_ln",
    )(x, wr, br, tri, wg, wu, wd, g, bb)


def _rope_tables(seq):
    rows = seq // GRID_W
    row = jnp.repeat(jnp.arange(rows, dtype=F32), GRID_W)
    col = jnp.tile(jnp.arange(GRID_W, dtype=F32), rows)
    half = HEAD_DIM // 2
    freqs = 1.0 / (ROPE_THETA ** (jnp.arange(0, half, 2, dtype=F32) / half))
    ar, ac = row[:, None] * freqs, col[:, None] * freqs
    cosv = jnp.concatenate([jnp.cos(ar), jnp.cos(ar), jnp.cos(ac), jnp.cos(ac)], axis=-1)
    sinv = jnp.concatenate([-jnp.sin(ar), jnp.sin(ar), -jnp.sin(ac), jnp.sin(ac)], axis=-1)
    return jnp.tile(cosv, (1, 2)), jnp.tile(sinv, (1, 2))


def _position_dft(seq):
    n1 = seq // FFT_N2
    dft = lambda n: 2.0 * np.pi * ((np.arange(n)[:, None] * np.arange(n)[None, :]) % n) / n
    a1 = dft(n1)
    w1 = np.concatenate([np.cos(a1), np.sin(a1)], axis=0) * seq ** -0.5
    at = 2.0 * np.pi * (np.arange(FFT_N2)[:, None] * np.arange(n1)[None, :]) / seq
    lanes = lambda t: jnp.asarray(np.repeat(t[:, :, None], LANES, axis=2), dtype=F32)
    a2 = dft(FFT_N2)
    return (jnp.asarray(w1, dtype=BF16), lanes(np.cos(at)), lanes(np.sin(at)),
            jnp.asarray(np.concatenate([np.cos(a2), np.sin(a2)], axis=1), dtype=BF16))


def _channel_dft():
    n = np.arange(HEAD_DIM)
    ang = 2.0 * np.pi * ((n[:, None] * n[None, :]) % HEAD_DIM) / HEAD_DIM
    eye = np.eye(4)
    cb = np.kron(eye, np.cos(ang)) * HEAD_DIM ** -0.5
    sb = np.kron(eye, np.sin(ang)) * HEAD_DIM ** -0.5
    return jnp.asarray(np.concatenate([cb, sb], axis=1), dtype=BF16)


def _layer_params(l, w_in, q_norm_g, k_norm_g, w_fnet, w_pool, pool_scale, conv_w, conv_b, dt_bias,
                  a_log, d_skip, ssd_norm_g, w_out, ln1_g, ln1_b, w_group, b_group, w_router, b_router,
                  w_gate, w_up, w_down, ln2_g, ln2_b):
    wi = w_in[l]
    hperm = np.array([0, 2, 1, 3])
    wq = wi[:, 0:256].reshape(D_MODEL, 4, HEAD_DIM)[:, hperm].reshape(D_MODEL, 256)
    twice = lambda v: jnp.concatenate([v, v], axis=-1)
    w_proj = jnp.concatenate([wq, wi[:, 256:1792], jnp.pad(twice(wi[:, 1792:1800]), ((0, 0), (0, 112)))],
                             axis=1).astype(BF16)
    wo = w_out[l]
    wo_att = wo[0:256].reshape(4, HEAD_DIM, D_MODEL)[hperm].reshape(256, D_MODEL)
    w_o = jnp.concatenate([wo_att, wo[256:]], axis=0).astype(BF16)
    row = lambda v, n: jnp.pad(v.reshape(1, -1), ((0, 0), (0, n - v.size)))
    w_pool_bd = jnp.zeros((256, 256), F32)
    for gi in range(4):
        w_pool_bd = w_pool_bd.at[gi * 64:(gi + 1) * 64, gi * 64:(gi + 1) * 64].set(w_pool[l, gi])
    wr = jnp.pad(jnp.concatenate([w_group[l], w_router[l]], axis=1), ((0, 0), (0, LANES - 20)))
    wr_hi = wr.astype(BF16)
    by_group = lambda w: w.reshape(N_GROUPS, EXPERTS_PER_GROUP, D_MODEL, EXPERT_FF).transpose(
        0, 2, 1, 3).reshape(N_GROUPS, D_MODEL, EXPERTS_PER_GROUP * EXPERT_FF).astype(BF16)
    return dict(
        w_proj=w_proj,
        qg=jnp.tile(q_norm_g[l], 2).reshape(1, LANES), kg=jnp.tile(k_norm_g[l], 2).reshape(1, LANES),
        dtb=row(twice(dt_bias[l].reshape(-1)), LANES), w_fnet=w_fnet[l].astype(BF16),
        w_pool=w_pool_bd.astype(BF16), pool_scale=pool_scale[l].reshape(1, 256),
        conv_w=conv_w[l], conv_b=conv_b[l].reshape(1, 512),
        alog=row(twice(a_log[l].reshape(-1)), LANES), dskip=jnp.repeat(d_skip[l], 64).reshape(1, 256),
        ssd_ng=ssd_norm_g[l].reshape(1, 256), w_o=w_o,
        ln1_g=ln1_g[l].reshape(1, D_MODEL), ln1_b=ln1_b[l].reshape(1, D_MODEL),
        wr=jnp.concatenate([wr_hi, (wr - wr_hi.astype(F32)).astype(BF16)], axis=1),
        br=row(jnp.concatenate([b_group[l], b_router[l]]), LANES),
        wg=by_group(w_gate[l]), wu=by_group(w_up[l]),
        wd=w_down[l].reshape(N_GROUPS, EXPERTS_PER_GROUP * EXPERT_FF, D_MODEL).astype(BF16),
        ln2_g=ln2_g[l].reshape(1, D_MODEL), ln2_b=ln2_b[l].reshape(1, D_MODEL),
    )


def _trunk(x3, params, tables):
    b, s, _ = x3.shape
    x = x3.reshape(b * s, D_MODEL)
    cosv, sinv, pos_dft, bd, dftc = tables
    for p in params:
        q, k, v, ab, up, z, xbc, dt = _in_proj(x, p["w_proj"], p["qg"], p["kg"], cosv, sinv, bd, dftc,
                                               p["dtb"], s)
        att = _attention(q, k, v, b, s)
        fmix = _fourier(pos_dft, ab, b, s)
        pol = _pool(up, p["w_pool"], p["pool_scale"], b, s)
        xc = _conv(xbc, p["conv_w"], p["conv_b"], b, s)
        yf, yb = _ssd(xc, dt, p["alog"], p["dskip"], b, s)
        x = _out_proj(x, att, fmix, pol, yf, yb, z, p["w_fnet"], p["w_o"], p["ssd_ng"], p["ln1_g"],
                      p["ln1_b"])
        x = _moe(x, p["wr"], p["br"], p["wg"], p["wu"], p["wd"], p["ln2_g"], p["ln2_b"])
    return x.reshape(b, s, D_MODEL)


def _tables(seq):
    cosv, sinv = _rope_tables(seq)
    lane = np.arange(LANES)
    bd = jnp.asarray(np.tile((lane[:, None] // HEAD_DIM) == (lane[None, :] // HEAD_DIM), (2, 1)), dtype=BF16)
    return cosv, sinv, _position_dft(seq), bd, _channel_dft()


def kernel(x_prompt, x_sample, w_in, q_norm_g, k_norm_g, w_fnet, w_pool, pool_scale, conv_w, conv_b, dt_bias, a_log, d_skip, ssd_norm_g, w_out, ln1_g, ln1_b, w_group, b_group, w_router, b_router, w_gate, w_up, w_down, ln2_g, ln2_b):
    weights = (w_in, q_norm_g, k_norm_g, w_fnet, w_pool, pool_scale, conv_w, conv_b, dt_bias, a_log,
               d_skip, ssd_norm_g, w_out, ln1_g, ln1_b, w_group, b_group, w_router, b_router,
               w_gate, w_up, w_down, ln2_g, ln2_b)
    params = [_layer_params(l, *weights) for l in range(w_in.shape[0])]
    y_prompt = _trunk(x_prompt, params, _tables(x_prompt.shape[1]))
    y_sample = _trunk(x_sample, params, _tables(x_sample.shape[1]))
    return (y_prompt, y_sample)
```

```python
import functools
import math

import numpy as np
import jax
import jax.numpy as jnp
from jax import lax
from jax.experimental import pallas as pl
from jax.experimental.pallas import tpu as pltpu

F32 = jnp.float32
BF16 = jnp.bfloat16

D_MODEL = 1024
DEPTH = 2
GRID_W = 64
HEAD_DIM = 64
ROPE_THETA = 10000.0
ATT_Q_HEADS = 4
ATT_KV_HEADS = 2
POOL_WINDOWS = (2, 4, 8, 16)
SSD_HEADS = 4
SSD_CONV = 4
SSD_CHUNK = 128
N_GROUPS = 4
EXPERTS_PER_GROUP = 4
N_EXPERTS = 16
EXPERT_FF = 256
DEEPNORM_ALPHA = (2 * DEPTH) ** 0.25
LN_EPS = 1e-5
RMS_EPS = 1e-6

LANES = 128
SUBLANES = 8
HALO = SUBLANES
VMEM_LIMIT = 56 * 1024 * 1024

C_Q, C_K, C_V, C_UF, C_UP, C_Z, C_XBC, C_DT, N_PROJ = 0, 256, 384, 512, 768, 1024, 1280, 1792, 1920
HI = lax.Precision.HIGHEST


def _cparams(sem):
    return pltpu.CompilerParams(dimension_semantics=sem, vmem_limit_bytes=VMEM_LIMIT)


def _dot(a, b):
    return jnp.dot(a, b, preferred_element_type=F32)


def _dot_nt(a, b):
    return lax.dot_general(a, b, (((1,), (1,)), ((), ())), preferred_element_type=F32)


def _silu(x):
    return x * (1.0 / (1.0 + jnp.exp(-x)))


def _layer_norm(x, g, b):
    mu = jnp.mean(x, axis=-1, keepdims=True)
    xc = x - mu
    var = jnp.mean(xc * xc, axis=-1, keepdims=True)
    return xc * lax.rsqrt(var + LN_EPS) * g + b


def _head_sumsq(x, ones_bd):
    sq = x * x
    hi = sq.astype(BF16)
    lo = (sq - hi.astype(F32)).astype(BF16)
    return _dot(jnp.concatenate([hi, lo], axis=1), ones_bd)


def _rope(x, cosv, sinv, first_half):
    w = x.shape[-1]
    partner = jnp.where(first_half, pltpu.roll(x, w - 16, axis=1), pltpu.roll(x, 16, axis=1))
    return x * cosv + partner * sinv


IN_PROJ_SUB = 512


def _in_proj_kernel(x_ref, w_ref, qg_ref, kg_ref, cos_ref, sin_ref, bd_ref, dftc_ref, dtb_ref,
                    q_ref, k_ref, v_ref, ab_ref, up_ref, z_ref, xbc_ref, dt_ref):
    bd = bd_ref[...]
    lane = lax.broadcasted_iota(jnp.int32, (1, LANES), 1)
    first_half = (lane & 31) < 16
    sub = min(IN_PROJ_SUB, x_ref.shape[0])
    nsub = x_ref.shape[0] // sub

    def project(i):
        return _dot(x_ref[i * sub:(i + 1) * sub, :].astype(BF16), w_ref[...])

    h_next = project(0)
    for i in range(nsub):
        h = h_next
        if i + 1 < nsub:
            h_next = project(i + 1)
        rows = slice(i * sub, (i + 1) * sub)
        cosv = cos_ref[rows, :]
        sinv = sin_ref[rows, :]
        for c in range(2):
            qc = h[:, C_Q + c * LANES:C_Q + (c + 1) * LANES]
            ss = _head_sumsq(qc, bd)
            qn = qc * lax.rsqrt(ss * (1.0 / HEAD_DIM) + RMS_EPS) * qg_ref[...]
            qr = _rope(qn, cosv, sinv, first_half) * (HEAD_DIM ** -0.5 * math.log2(math.e))
            q_ref[rows, c * LANES:(c + 1) * LANES] = qr.astype(BF16)
        kc = h[:, C_K:C_K + LANES]
        ss = _head_sumsq(kc, bd)
        kn = kc * lax.rsqrt(ss * (1.0 / HEAD_DIM) + RMS_EPS) * kg_ref[...]
        k_ref[rows, :] = _rope(kn, cosv, sinv, first_half).astype(BF16)
        v_ref[rows, :] = h[:, C_V:C_V + LANES].astype(BF16)
        uf = h[:, C_UF:C_UF + 256].astype(BF16)
        ab_ref[rows, :] = _dot(uf, dftc_ref[...])
        up_ref[rows, :] = h[:, C_UP:C_UP + 256]
        z_ref[rows, :] = h[:, C_Z:C_Z + 256]
        xbc_ref[rows, :] = h[:, C_XBC:C_XBC + 512]
        dr = h[:, C_DT:C_DT + LANES] + dtb_ref[...]
        dt_ref[rows, :] = jnp.maximum(dr, 0.0) + jnp.log(1.0 + jnp.exp(-jnp.abs(dr)))


def _in_proj(x, w, qg, kg, cosv, sinv, bd, dftc, dtb, seq):
    t = x.shape[0]
    tm = min(1024, seq)
    nseq = seq // tm
    full = lambda a: pl.BlockSpec(a.shape, lambda i: (0,) * a.ndim)
    row = lambda wdt: pl.BlockSpec((tm, wdt), lambda i: (i, 0))
    pos = pl.BlockSpec((tm, LANES), lambda i: (i % nseq, 0))
    outs = [(256, BF16), (128, BF16), (128, BF16), (512, F32), (256, F32), (256, F32), (512, F32),
            (128, F32)]
    return pl.pallas_call(
        _in_proj_kernel,
        grid=(t // tm,),
        in_specs=[row(D_MODEL), full(w), full(qg), full(kg), pos, pos, full(bd), full(dftc), full(dtb)],
        out_specs=[row(wd) for wd, _ in outs],
        out_shape=[jax.ShapeDtypeStruct((t, wd), dt) for wd, dt in outs],
        compiler_params=_cparams(("parallel",)),
        name="in_proj",
    )(x, w, qg, kg, cosv, sinv, bd, dftc, dtb)


ATT_KEY_CHUNK = 256
ATT_FIXED_SHIFT_MAX = 60.0


def _attn_kernel(bound_ref, q_ref, k_ref, v_ref, o_ref, vt_ref):
    nch, _, vrows, tk = vt_ref.shape
    tq = q_ref.shape[0]
    lane = lax.broadcasted_iota(jnp.int32, (1, LANES), 1)
    head_lanes = [(lane >> 6) == j for j in range(ATT_KV_HEADS)]

    @pl.when(pl.program_id(1) == 0)
    def _():
        for c in range(nch):
            vt = v_ref[c * tk:(c + 1) * tk, :].astype(F32).T.astype(BF16)
            for j in range(ATT_KV_HEADS):
                vt_ref[c, j, 0:HEAD_DIM, :] = vt[j * HEAD_DIM:(j + 1) * HEAD_DIM, :]
                vt_ref[c, j, HEAD_DIM:vrows, :] = jnp.ones((vrows - HEAD_DIM, tk), BF16)

    qms = []
    for c in range(2):
        qc = q_ref[:, c * LANES:(c + 1) * LANES]
        for j in range(ATT_KV_HEADS):
            qms.append(jnp.where(head_lanes[j], qc, jnp.zeros_like(qc)))

    bound = bound_ref[0]
    fixed_shift_ok = bound <= ATT_FIXED_SHIFT_MAX

    def scores(c):
        kc = k_ref[c * tk:(c + 1) * tk, :]
        return [_dot_nt(kc, qm) for qm in qms]

    def run(shift):
        ms = [jnp.full((1, tq), -jnp.inf, F32)] * ATT_Q_HEADS
        accs = [jnp.zeros((vrows, tq), F32)] * ATT_Q_HEADS
        st_next = scores(0)
        for c in range(nch):
            sts = st_next
            if c + 1 < nch:
                st_next = scores(c + 1)
            for h in range(ATT_Q_HEADS):
                if shift is None:
                    m_new = jnp.maximum(ms[h], jnp.max(sts[h], axis=0, keepdims=True))
                    accs[h] = jnp.exp2(ms[h] - m_new) * accs[h]
                    ms[h] = m_new
                    p = jnp.exp2(sts[h] - m_new).astype(BF16)
                else:
                    p = jnp.exp2(sts[h] - shift).astype(BF16)
                accs[h] = accs[h] + _dot(vt_ref[c, h % ATT_KV_HEADS], p)
        for c in range(2):
            ot = jnp.concatenate([accs[2 * c + j][0:HEAD_DIM] / accs[2 * c + j][HEAD_DIM:HEAD_DIM + 1]
                                  for j in range(ATT_KV_HEADS)], axis=0)
            o_ref[:, c * LANES:(c + 1) * LANES] = ot.T.astype(BF16)

    @pl.when(fixed_shift_ok)
    def _():
        run(bound)

    @pl.when(jnp.logical_not(fixed_shift_ok))
    def _():
        run(None)


def _attention(bound, q, k, v, b, s):
    tq = min(256, s)
    tk = min(ATT_KEY_CHUNK, s)
    q3, k3, v3 = q.reshape(b, s, 256), k.reshape(b, s, LANES), v.reshape(b, s, LANES)
    out = pl.pallas_call(
        _attn_kernel,
        grid=(b, s // tq),
        in_specs=[pl.BlockSpec(memory_space=pltpu.SMEM),
                  pl.BlockSpec((None, tq, 256), lambda bi, i: (bi, i, 0)),
                  pl.BlockSpec((None, s, LANES), lambda bi, i: (bi, 0, 0)),
                  pl.BlockSpec((None, s, LANES), lambda bi, i: (bi, 0, 0))],
        out_specs=pl.BlockSpec((None, tq, 256), lambda bi, i: (bi, i, 0)),
        out_shape=jax.ShapeDtypeStruct((b, s, 256), BF16),
        scratch_shapes=[pltpu.VMEM((s // tk, ATT_KV_HEADS, HEAD_DIM + 16, tk), BF16)],
        compiler_params=_cparams(("parallel", "arbitrary")),
        name="attention",
    )(bound, q3, k3, v3)
    return out.reshape(b * s, 256)


FFT_N2 = 64
FFT_K1_GROUP = 8


def _fourier_kernel(a_ref, b_ref, w1_ref, tc_ref, ts_ref, cs2_ref, o_ref, yr_ref, yi_ref):
    n1 = a_ref.shape[0]
    w1 = w1_ref[...]
    for n2 in range(FFT_N2):
        v = jnp.concatenate([a_ref[:, n2, :], b_ref[:, n2, :]], axis=1).astype(BF16)
        r = _dot(w1, v)
        yr = r[0:n1, 0:LANES] - r[n1:2 * n1, LANES:2 * LANES]
        ym = r[0:n1, LANES:2 * LANES] + r[n1:2 * n1, 0:LANES]
        tc = tc_ref[n2]
        ts = ts_ref[n2]
        yr_ref[:, n2, :] = yr * tc - ym * ts
        yi_ref[:, n2, :] = -(ym * tc) - yr * ts
    cs2 = cs2_ref[...]
    for k0 in range(0, n1, FFT_K1_GROUP):
        ks = range(k0, k0 + FFT_K1_GROUP)
        yr = jnp.concatenate([yr_ref[k1] for k1 in ks], axis=1)
        yi = jnp.concatenate([yi_ref[k1] for k1 in ks], axis=1)
        z = _dot(cs2, jnp.concatenate([yr, yi], axis=0).astype(BF16))
        for g, k1 in enumerate(ks):
            o_ref[:, k1, :] = z[:, g * LANES:(g + 1) * LANES]


def _fourier(tabs, ab, b, s):
    w1, tc, ts, cs2 = tabs
    n1 = s // FFT_N2
    full = lambda a: pl.BlockSpec(a.shape, lambda bi, h: (0,) * a.ndim)
    ab4 = ab.reshape(b, n1, FFT_N2, 512)
    out = pl.pallas_call(
        _fourier_kernel,
        grid=(b, 2),
        in_specs=[pl.BlockSpec((None, n1, FFT_N2, LANES), lambda bi, h: (bi, 0, 0, h)),
                  pl.BlockSpec((None, n1, FFT_N2, LANES), lambda bi, h: (bi, 0, 0, 2 + h)),
                  full(w1), full(tc), full(ts), full(cs2)],
        out_specs=pl.BlockSpec((None, FFT_N2, n1, LANES), lambda bi, h: (bi, 0, 0, h)),
        out_shape=jax.ShapeDtypeStruct((b, FFT_N2, n1, 256), F32),
        scratch_shapes=[pltpu.VMEM((n1, FFT_N2, LANES), F32)] * 2,
        compiler_params=_cparams(("parallel", "parallel")),
        name="fourier",
    )(ab4, ab4, w1, tc, ts, cs2)
    return out.reshape(b * s, 256)


def _halo_specs(tt, width, s):
    nb = tt // HALO
    last = s // HALO - 1
    main = pl.BlockSpec((None, tt, width), lambda bi, i: (bi, i, 0))
    prev = pl.BlockSpec((None, HALO, width), lambda bi, i: (bi, jnp.maximum(i * nb - 1, 0), 0))
    nxt = pl.BlockSpec((None, HALO, width), lambda bi, i: (bi, jnp.minimum((i + 1) * nb, last), 0))
    return [main, prev, nxt]


def _with_halo(main_ref, prev_ref, next_ref):
    i = pl.program_id(1)
    n = pl.num_programs(1)
    prev = jnp.where(i > 0, prev_ref[...], 0.0)
    nxt = jnp.where(i < n - 1, next_ref[...], 0.0)
    return jnp.concatenate([prev, main_ref[...], nxt], axis=0)


def _shifted(ext, d, tt):
    n = ext.shape[0]
    r = ext if d == 0 else pltpu.roll(ext, (-d) % n, axis=0)
    return r[HALO:HALO + tt]


def _pool_kernel(u_ref, up_ref, un_ref, w_ref, sc_ref, o_ref, *, seq):
    tt = u_ref.shape[0]
    ext = _with_halo(u_ref, up_ref, un_ref)
    n = ext.shape[0]
    sh = lambda a, d: a if d == 0 else pltpu.roll(a, (-d) % n, axis=0)
    t = pl.program_id(1) * tt + lax.broadcasted_iota(jnp.int32, (tt, 1), 0)

    def inv_count(w):
        lo = jnp.clip(t - w // 2, 0, seq - 1)
        hi = jnp.clip(t + (w - w // 2) - 1, 0, seq - 1)
        return 1.0 / (hi - lo + 1).astype(F32)

    left = (lax.broadcasted_iota(jnp.int32, (1, LANES), 1) >> 6) == 0
    cols = []
    for c in range(2):
        e = ext[:, c * LANES:(c + 1) * LANES]
        u = u_ref[:, c * LANES:(c + 1) * LANES]
        p2 = sh(e, -1) + e
        p4 = sh(p2, -1) + sh(p2, 1)
        if c == 0:
            small, big, ws, wb = p2, p4, POOL_WINDOWS[0], POOL_WINDOWS[1]
        else:
            p8 = sh(p4, -2) + sh(p4, 2)
            p16 = sh(p8, -4) + sh(p8, 4)
            small, big, ws, wb = p8, p16, POOL_WINDOWS[2], POOL_WINDOWS[3]
        mean = jnp.where(left, small[HALO:HALO + tt] * inv_count(ws), big[HALO:HALO + tt] * inv_count(wb))
        cols.append((mean - u).astype(BF16))
    o_ref[...] = (_dot(jnp.concatenate(cols, axis=1), w_ref[...]) * sc_ref[...]).astype(BF16)


def _pool(u, w_bd, scale, b, s):
    tt = min(1024, s)
    u3 = u.reshape(b, s, 256)
    full = lambda a: pl.BlockSpec(a.shape, lambda bi, i: (0,) * a.ndim)
    out = pl.pallas_call(
        functools.partial(_pool_kernel, seq=s),
        grid=(b, s // tt),
        in_specs=_halo_specs(tt, 256, s) + [full(w_bd), full(scale)],
        out_specs=pl.BlockSpec((None, tt, 256), lambda bi, i: (bi, i, 0)),
        out_shape=jax.ShapeDtypeStruct((b, s, 256), BF16),
        compiler_params=_cparams(("parallel", "parallel")),
        name="pool",
    )(u3, u3, u3, w_bd, scale)
    return out.reshape(b * s, 256)


def _conv_kernel(x_ref, xp_ref, xn_ref, w_ref, b_ref, o_ref):
    tt = x_ref.shape[0]
    ext = _with_halo(x_ref, xp_ref, xn_ref)
    acc = b_ref[...] + _shifted(ext, -2, tt) * w_ref[0:1, :]
    for kk in range(1, SSD_CONV):
        acc = acc + _shifted(ext, kk - 2, tt) * w_ref[kk:kk + 1, :]
    o_ref[...] = _silu(acc)


def _conv(xbc, conv_w, conv_b, b, s):
    tt = min(1024, s)
    x3 = xbc.reshape(b, s, 512)
    full = lambda a: pl.BlockSpec(a.shape, lambda bi, i: (0,) * a.ndim)
    return pl.pallas_call(
        _conv_kernel,
        grid=(b, s // tt),
        in_specs=_halo_specs(tt, 512, s) + [full(conv_w), full(conv_b)],
        out_specs=pl.BlockSpec((None, tt, 512), lambda bi, i: (bi, i, 0)),
        out_shape=jax.ShapeDtypeStruct((b, s, 512), F32),
        compiler_params=_cparams(("parallel", "parallel")),
        name="ssd_conv",
    )(x3, x3, x3, conv_w, conv_b)


def _ssd_kernel(xf_ref, dtf_ref, xb_ref, dtb_ref, alog_ref, dskip_ref, yf_ref, yb_ref, carry_ref):
    @pl.when(pl.program_id(1) == 0)
    def _():
        carry_ref[...] = jnp.zeros_like(carry_ref)

    L = SSD_CHUNK
    nch = xf_ref.shape[0] // L
    a_row = -jnp.exp(alog_ref[...])
    dskip = dskip_ref[...]
    r = lax.broadcasted_iota(jnp.int32, (L, L), 0)
    c = lax.broadcasted_iota(jnp.int32, (L, L), 1)
    keeps = (c <= r, c >= r)
    tris = [k.astype(F32) for k in keeps]
    grp = lax.broadcasted_iota(jnp.int32, (1, LANES), 1) >> 6
    head = lax.broadcasted_iota(jnp.int32, (1, 256), 1) >> 6
    row_grp = lax.broadcasted_iota(jnp.int32, (LANES, 1), 0) >> 6

    units = []
    for ci in range(nch):
        units.append(dict(d=0, rows=slice(ci * L, (ci + 1) * L)))
        units.append(dict(d=1, rows=slice((nch - 1 - ci) * L, (nch - ci) * L)))

    for u in units:
        d = u["d"]
        u["off"] = d * SSD_HEADS
        u["xc"] = (xf_ref, xb_ref)[d][u["rows"], :]
        u["dtc"] = (dtf_ref, dtb_ref)[d][u["rows"], :]
        u["cs"] = jnp.dot(tris[d], u["dtc"] * a_row, precision=HI, preferred_element_type=F32)

    prow = lax.broadcasted_iota(jnp.int32, (LANES, 1), 0)
    spread = [jnp.where((prow < 16) & ((prow & 7) == d * SSD_HEADS + head), 1.0, 0.0).astype(BF16)
              for d in range(2)]
    low8 = lax.broadcasted_iota(jnp.int32, (1, LANES), 1) < 8

    for u in units:
        cs, xc = u["cs"], u["xc"]
        u["cs_t"] = cs.T
        tot = cs[0:1, :] if u["d"] else cs[L - 1:L, :]
        fac = jnp.concatenate([u["dtc"], jnp.exp(tot - cs), jnp.exp(cs)], axis=0)
        head_part = fac.astype(BF16).astype(F32)
        fac_x = _dot(jnp.where(low8, head_part, fac - head_part).astype(BF16), spread[u["d"]])
        u["ds_x"], u["ecs_x"] = fac_x[L:2 * L], fac_x[2 * L:3 * L]
        u["etot_x"] = u["ecs_x"][0:1, :] if u["d"] else u["ecs_x"][L - 1:L, :]
        u["xd"] = xc[:, 0:256] * fac_x[0:L]
        u["xd_b"] = u["xd"].astype(BF16)
        u["b_t"] = xc[:, 256:384].T.astype(BF16)
        cmat = xc[:, 384:512]
        u["g"] = [_dot(jnp.where(grp == g, cmat, 0.0).astype(BF16), u["b_t"]) for g in range(2)]

    for u in units:
        cs, cs_t, off, keep = u["cs"], u["cs_t"], u["off"], keeps[u["d"]]
        y = None
        for h in range(SSD_HEADS):
            diff = cs[:, off + h:off + h + 1] - cs_t[off + h:off + h + 1, :]
            decay = jnp.where(keep, jnp.exp(jnp.where(keep, diff, 0.0)), 0.0)
            sc = (u["g"][h // 2] * decay).astype(BF16)
            yd = _dot(sc, u["xd_b"])
            y = yd if y is None else jnp.where(head == h, yd, y)
        u["y"] = y

    for u in units:
        st = _dot(u["b_t"], (u["xd"] * u["ds_x"]).astype(BF16))
        u["st"] = jnp.where(row_grp == (head >> 1), st, 0.0)

    carries = [carry_ref[0], carry_ref[1]]
    for u in units:
        d, xc = u["d"], u["xc"]
        y_off = _dot(xc[:, 384:512].astype(BF16), carries[d].astype(BF16))
        y = u["y"] + y_off * u["ecs_x"]
        carries[d] = carries[d] * u["etot_x"] + u["st"]
        if d == 0:
            yf_ref[u["rows"], :] = y + xc[:, 0:256] * dskip
        else:
            yb_ref[u["rows"], :] = y
    carry_ref[0] = carries[0]
    carry_ref[1] = carries[1]


def _ssd(xc, dt, alog_row, dskip_row, b, s):
    tt = min(512, s)
    nt = s // tt
    dt3 = dt.reshape(b, s, LANES)
    fwd = lambda w: pl.BlockSpec((None, tt, w), lambda bi, i: (bi, i, 0))
    bwd = lambda w: pl.BlockSpec((None, tt, w), lambda bi, i: (bi, nt - 1 - i, 0))
    full = lambda a: pl.BlockSpec(a.shape, lambda bi, i: (0,) * a.ndim)
    yf, yb = pl.pallas_call(
        _ssd_kernel,
        grid=(b, nt),
        in_specs=[fwd(512), fwd(LANES), bwd(512), bwd(LANES), full(alog_row), full(dskip_row)],
        out_specs=[fwd(256), bwd(256)],
        out_shape=[jax.ShapeDtypeStruct((b, s, 256), F32)] * 2,
        scratch_shapes=[pltpu.VMEM((2, LANES, 256), F32)],
        compiler_params=_cparams(("parallel", "arbitrary")),
        name="ssd_scan",
    )(xc, dt3, xc, dt3, alog_row, dskip_row)
    return yf.reshape(b * s, 256), yb.reshape(b * s, 256)


OUT_PROJ_SUB = 256


def _out_proj_kernel(x_ref, att_ref, fmix_ref, pol_ref, yf_ref, yb_ref, z_ref, wf_ref, w_ref, ng_ref,
                     g_ref, b_ref, o_ref):
    ng = ng_ref[...]
    sub = min(OUT_PROJ_SUB, x_ref.shape[0])

    def mixed(i):
        rows = slice(i * sub, (i + 1) * sub)
        fno = _dot(fmix_ref[rows, :].astype(BF16), wf_ref[...]).astype(BF16)
        y = (yf_ref[rows, :] + yb_ref[rows, :]) * _silu(z_ref[rows, :])
        parts = []
        for gi in range(2):
            yg = y[:, gi * LANES:(gi + 1) * LANES]
            ms = jnp.mean(yg * yg, axis=-1, keepdims=True)
            parts.append((yg * lax.rsqrt(ms + RMS_EPS) * ng[:, gi * LANES:(gi + 1) * LANES]).astype(BF16))
        return _dot(jnp.concatenate([att_ref[rows, :], fno, pol_ref[rows, :]] + parts, axis=1), w_ref[...])

    nsub = x_ref.shape[0] // sub
    mix_next = mixed(0)
    for i in range(nsub):
        mix = mix_next
        if i + 1 < nsub:
            mix_next = mixed(i + 1)
        rows = slice(i * sub, (i + 1) * sub)
        o_ref[rows, :] = _layer_norm(DEEPNORM_ALPHA * x_ref[rows, :] + mix, g_ref[...], b_ref[...])


def _out_proj(x, att, fmix, pol, yf, yb, z, wf, w, ng, g, bb):
    t = x.shape[0]
    tm = min(1024, t)
    row = lambda wd: pl.BlockSpec((tm, wd), lambda i: (i, 0))
    full = lambda a: pl.BlockSpec(a.shape, lambda i: (0,) * a.ndim)
    return pl.pallas_call(
        _out_proj_kernel,
        grid=(t // tm,),
        in_specs=[row(D_MODEL)] + [row(256)] * 6 + [full(wf), full(w), full(ng), full(g), full(bb)],
        out_specs=row(D_MODEL),
        out_shape=jax.ShapeDtypeStruct((t, D_MODEL), F32),
        compiler_params=_cparams(("parallel",)),
        name="out_proj_ln",
    )(x, att, fmix, pol, yf, yb, z, wf, w, ng, g, bb)


MOE_TILE = 1024
MOE_CHUNK = 128
MOE_PERM_ROWS = 256


def _router(x, w2, br):
    xh = x.astype(BF16)
    xl = (x - xh.astype(F32)).astype(BF16)
    l1 = _dot(xh, w2)
    logits = l1[:, 0:LANES] + l1[:, LANES:2 * LANES] + _dot(xl, w2[:, 0:LANES]) + br
    lane = lax.broadcasted_iota(jnp.int32, (1, LANES), 1)
    ninf = -jnp.inf
    gl = jnp.where(lane < N_GROUPS, logits, ninf)
    gmax = jnp.max(gl, axis=-1, keepdims=True)
    g_p = 1.0 / jnp.sum(jnp.exp(gl - gmax), axis=-1, keepdims=True)
    g_idx = jnp.min(jnp.where(gl == gmax, lane, LANES), axis=-1, keepdims=True)
    e_lane = lane - N_GROUPS
    in_grp = (e_lane >= 0) & (e_lane < N_EXPERTS) & ((e_lane >> 2) == g_idx)
    el = jnp.where(in_grp, logits, ninf)
    m1 = jnp.max(el, axis=-1, keepdims=True)
    i1 = jnp.min(jnp.where(el == m1, lane, LANES), axis=-1, keepdims=True)
    el2 = jnp.where(lane == i1, ninf, el)
    m2 = jnp.max(el2, axis=-1, keepdims=True)
    i2 = jnp.min(jnp.where(el2 == m2, lane, LANES), axis=-1, keepdims=True)
    e2 = jnp.exp(m2 - m1)
    w1 = 1.0 / (1.0 + e2)
    gates = g_p * (jnp.where(lane == i1, w1, 0.0) + jnp.where(lane == i2, e2 * w1, 0.0))
    return gates, g_idx


def _moe_kernel(x_ref, wr_ref, br_ref, tri_ref, wg_ref, wu_ref, wd_ref, g_ref, b_ref, o_ref,
                xs_ref, gs_ref, pos_ref, acc_ref, off_ref):
    grp = pl.program_id(1)
    n = x_ref.shape[0]
    lane = lax.broadcasted_iota(jnp.int32, (1, LANES), 1)

    @pl.when(grp == 0)
    def _route_and_sort():
        x = x_ref[...]
        gates, g_idx = _router(x, wr_ref[...], br_ref[...])
        onehot = lane == g_idx
        csum = _dot(tri_ref[...], jnp.where(onehot, 1.0, 0.0).astype(BF16))
        cnt = csum[n - 1:n, :].astype(jnp.int32)
        c0, c1, c2 = cnt[0, 0], cnt[0, 1], cnt[0, 2]
        off_ref[0] = 0
        off_ref[1] = c0
        off_ref[2] = c0 + c1
        off_ref[3] = c0 + c1 + c2
        off_ref[4] = n
        offv = jnp.where(lane == 1, c0, jnp.where(lane == 2, c0 + c1, jnp.where(lane == 3, c0 + c1 + c2, 0)))
        pos = jnp.sum(jnp.where(onehot, csum + offv.astype(F32), 0.0), axis=-1, keepdims=True) - 1.0
        pos_ref[...] = jnp.broadcast_to(pos, (n, LANES))
        pos_row = pos_ref[...].T[0:1, :].astype(jnp.int32)
        xb = x.astype(BF16)
        g_hi = gates.astype(BF16)
        g2 = jnp.concatenate([g_hi, (gates - g_hi.astype(F32)).astype(BF16)], axis=1)
        for r in range(n // MOE_PERM_ROWS):
            rows = lax.broadcasted_iota(jnp.int32, (MOE_PERM_ROWS, n), 0) + r * MOE_PERM_ROWS
            perm = jnp.where(rows == pos_row, 1.0, 0.0).astype(BF16)
            sl = slice(r * MOE_PERM_ROWS, (r + 1) * MOE_PERM_ROWS)
            xs_ref[sl, :] = _dot(perm, xb).astype(BF16)
            gg = _dot(perm, g2)
            gs_ref[sl, :] = gg[:, 0:LANES] + gg[:, LANES:2 * LANES]
        acc_ref[...] = jnp.zeros_like(acc_ref)

    lo = off_ref[grp]
    hi = off_ref[grp + 1]

    def chunk(c, carry):
        r0 = pl.multiple_of(c * MOE_CHUNK, MOE_CHUNK)

        @pl.when((lo < r0 + MOE_CHUNK) & (hi > r0))
        def _():
            rs = pl.ds(r0, MOE_CHUNK)
            xs = xs_ref[rs, :]
            gsc = gs_ref[rs, :]
            hid = _silu(_dot(xs, wg_ref[...])) * _dot(xs, wu_ref[...])
            parts = []
            for e in range(EXPERTS_PER_GROUP):
                ge = jnp.sum(jnp.where(lane == N_GROUPS + EXPERTS_PER_GROUP * grp + e, gsc, 0.0),
                             axis=-1, keepdims=True)
                parts.append((hid[:, e * EXPERT_FF:(e + 1) * EXPERT_FF] * ge).astype(BF16))
            acc_ref[rs, :] += _dot(jnp.concatenate(parts, axis=1), wd_ref[...])

        return carry

    lax.fori_loop(0, n // MOE_CHUNK, chunk, 0)

    @pl.when(grp == N_GROUPS - 1)
    def _unsort_and_norm():
        xs_ref[...] = acc_ref[...].astype(BF16)
        cols = lax.broadcasted_iota(jnp.int32, (MOE_PERM_ROWS, n), 1)
        for r in range(n // MOE_PERM_ROWS):
            sl = slice(r * MOE_PERM_ROWS, (r + 1) * MOE_PERM_ROWS)
            perm_t = jnp.where(cols == pos_ref[sl, 0:1].astype(jnp.int32), 1.0, 0.0).astype(BF16)
            y = _dot(perm_t, xs_ref[...])
            o_ref[sl, :] = _layer_norm(DEEPNORM_ALPHA * x_ref[sl, :] + y, g_ref[...], b_ref[...])


def _moe(x, wr, br, wg, wu, wd, g, bb):
    t = x.shape[0]
    n = min(MOE_TILE, t)
    tri = jnp.asarray(np.tril(np.ones((n, n), np.float32)), dtype=BF16)
    full = lambda a: pl.BlockSpec(a.shape, lambda i, e: (0,) * a.ndim)
    wspec = pl.BlockSpec((None, D_MODEL, D_MODEL), lambda i, e: (e, 0, 0))
    return pl.pallas_call(
        _moe_kernel,
        grid=(t // n, N_GROUPS),
        in_specs=[pl.BlockSpec((n, D_MODEL), lambda i, e: (i, 0)), full(wr), full(br), full(tri),
                  wspec, wspec, wspec, full(g), full(bb)],
        out_specs=pl.BlockSpec((n, D_MODEL), lambda i, e: (i, 0)),
        out_shape=jax.ShapeDtypeStruct((t, D_MODEL), F32),
        scratch_shapes=[pltpu.VMEM((n, D_MODEL), BF16), pltpu.VMEM((n, LANES), F32),
                        pltpu.VMEM((n, LANES), F32), pltpu.VMEM((n, D_MODEL), F32),
                        pltpu.SMEM((8,), jnp.int32)],
        compiler_params=_cparams(("parallel", "arbitrary")),
        name="moe_ln",
    )(x, wr, br, tri, wg, wu, wd, g, bb)


def _rope_tables(seq):
    rows = seq // GRID_W
    row = jnp.repeat(jnp.arange(rows, dtype=F32), GRID_W)
    col = jnp.tile(jnp.arange(GRID_W, dtype=F32), rows)
    half = HEAD_DIM // 2
    freqs = 1.0 / (ROPE_THETA ** (jnp.arange(0, half, 2, dtype=F32) / half))
    ar, ac = row[:, None] * freqs, col[:, None] * freqs
    cosv = jnp.concatenate([jnp.cos(ar), jnp.cos(ar), jnp.cos(ac), jnp.cos(ac)], axis=-1)
    sinv = jnp.concatenate([-jnp.sin(ar), jnp.sin(ar), -jnp.sin(ac), jnp.sin(ac)], axis=-1)
    return jnp.tile(cosv, (1, 2)), jnp.tile(sinv, (1, 2))


def _position_dft(seq):
    n1 = seq // FFT_N2
    dft = lambda n: 2.0 * np.pi * ((np.arange(n)[:, None] * np.arange(n)[None, :]) % n) / n
    a1 = dft(n1)
    w1 = np.concatenate([np.cos(a1), np.sin(a1)], axis=0) * seq ** -0.5
    at = 2.0 * np.pi * (np.arange(FFT_N2)[:, None] * np.arange(n1)[None, :]) / seq
    lanes = lambda t: jnp.asarray(np.repeat(t[:, :, None], LANES, axis=2), dtype=F32)
    a2 = dft(FFT_N2)
    return (jnp.asarray(w1, dtype=BF16), lanes(np.cos(at)), lanes(np.sin(at)),
            jnp.asarray(np.concatenate([np.cos(a2), np.sin(a2)], axis=1), dtype=BF16))


def _channel_dft():
    n = np.arange(HEAD_DIM)
    ang = 2.0 * np.pi * ((n[:, None] * n[None, :]) % HEAD_DIM) / HEAD_DIM
    eye = np.eye(4)
    cb = np.kron(eye, np.cos(ang)) * HEAD_DIM ** -0.5
    sb = np.kron(eye, np.sin(ang)) * HEAD_DIM ** -0.5
    return jnp.asarray(np.concatenate([cb, sb], axis=1), dtype=BF16)


def _layer_params(l, w_in, q_norm_g, k_norm_g, w_fnet, w_pool, pool_scale, conv_w, conv_b, dt_bias,
                  a_log, d_skip, ssd_norm_g, w_out, ln1_g, ln1_b, w_group, b_group, w_router, b_router,
                  w_gate, w_up, w_down, ln2_g, ln2_b):
    wi = w_in[l]
    hperm = np.array([0, 2, 1, 3])
    wq = wi[:, 0:256].reshape(D_MODEL, 4, HEAD_DIM)[:, hperm].reshape(D_MODEL, 256)
    twice = lambda v: jnp.concatenate([v, v], axis=-1)
    w_proj = jnp.concatenate([wq, wi[:, 256:1792], jnp.pad(twice(wi[:, 1792:1800]), ((0, 0), (0, 112)))],
                             axis=1).astype(BF16)
    wo = w_out[l]
    wo_att = wo[0:256].reshape(4, HEAD_DIM, D_MODEL)[hperm].reshape(256, D_MODEL)
    w_o = jnp.concatenate([wo_att, wo[256:]], axis=0).astype(BF16)
    row = lambda v, n: jnp.pad(v.reshape(1, -1), ((0, 0), (0, n - v.size)))
    w_pool_bd = jnp.zeros((256, 256), F32)
    for gi in range(4):
        w_pool_bd = w_pool_bd.at[gi * 64:(gi + 1) * 64, gi * 64:(gi + 1) * 64].set(w_pool[l, gi])
    wr = jnp.pad(jnp.concatenate([w_group[l], w_router[l]], axis=1), ((0, 0), (0, LANES - 20)))
    wr_hi = wr.astype(BF16)
    by_group = lambda w: w.reshape(N_GROUPS, EXPERTS_PER_GROUP, D_MODEL, EXPERT_FF).transpose(
        0, 2, 1, 3).reshape(N_GROUPS, D_MODEL, EXPERTS_PER_GROUP * EXPERT_FF).astype(BF16)
    att_bound = (1.01 * HEAD_DIM ** -0.5 * math.log2(math.e) * HEAD_DIM
                 * jnp.max(jnp.abs(q_norm_g[l])) * jnp.max(jnp.abs(k_norm_g[l]))).reshape(1).astype(F32)
    return dict(
        w_proj=w_proj, att_bound=att_bound,
        qg=jnp.tile(q_norm_g[l], 2).reshape(1, LANES), kg=jnp.tile(k_norm_g[l], 2).reshape(1, LANES),
        dtb=row(twice(dt_bias[l].reshape(-1)), LANES), w_fnet=w_fnet[l].astype(BF16),
        w_pool=w_pool_bd.astype(BF16), pool_scale=pool_scale[l].reshape(1, 256),
        conv_w=conv_w[l], conv_b=conv_b[l].reshape(1, 512),
        alog=row(twice(a_log[l].reshape(-1)), LANES), dskip=jnp.repeat(d_skip[l], 64).reshape(1, 256),
        ssd_ng=ssd_norm_g[l].reshape(1, 256), w_o=w_o,
        ln1_g=ln1_g[l].reshape(1, D_MODEL), ln1_b=ln1_b[l].reshape(1, D_MODEL),
        wr=jnp.concatenate([wr_hi, (wr - wr_hi.astype(F32)).astype(BF16)], axis=1),
        br=row(jnp.concatenate([b_group[l], b_router[l]]), LANES),
        wg=by_group(w_gate[l]), wu=by_group(w_up[l]),
        wd=w_down[l].reshape(N_GROUPS, EXPERTS_PER_GROUP * EXPERT_FF, D_MODEL).astype(BF16),
        ln2_g=ln2_g[l].reshape(1, D_MODEL), ln2_b=ln2_b[l].reshape(1, D_MODEL),
    )


def _trunk(x3, params, tables):
    b, s, _ = x3.shape
    x = x3.reshape(b * s, D_MODEL)
    cosv, sinv, pos_dft, bd, dftc = tables
    for p in params:
        q, k, v, ab, up, z, xbc, dt = _in_proj(x, p["w_proj"], p["qg"], p["kg"], cosv, sinv, bd, dftc,
                                               p["dtb"], s)
        att = _attention(p["att_bound"], q, k, v, b, s)
        fmix = _fourier(pos_dft, ab, b, s)
        pol = _pool(up, p["w_pool"], p["pool_scale"], b, s)
        xc = _conv(xbc, p["conv_w"], p["conv_b"], b, s)
        yf, yb = _ssd(xc, dt, p["alog"], p["dskip"], b, s)
        x = _out_proj(x, att, fmix, pol, yf, yb, z, p["w_fnet"], p["w_o"], p["ssd_ng"], p["ln1_g"],
                      p["ln1_b"])
        x = _moe(x, p["wr"], p["br"], p["wg"], p["wu"], p["wd"], p["ln2_g"], p["ln2_b"])
    return x.reshape(b, s, D_MODEL)


def _tables(seq):
    cosv, sinv = _rope_tables(seq)
    lane = np.arange(LANES)
    bd = jnp.asarray(np.tile((lane[:, None] // HEAD_DIM) == (lane[None, :] // HEAD_DIM), (2, 1)), dtype=BF16)
    return cosv, sinv, _position_dft(seq), bd, _channel_dft()


def kernel(x_prompt, x_sample, w_in, q_norm_g, k_norm_g, w_fnet, w_pool, pool_scale, conv_w, conv_b, dt_bias, a_log, d_skip, ssd_norm_g, w_out, ln1_g, ln1_b, w_group, b_group, w_router, b_router, w_gate, w_up, w_down, ln2_g, ln2_b):
    weights = (w_in, q_norm_g, k_norm_g, w_fnet, w_pool, pool_scale, conv_w, conv_b, dt_bias, a_log,
               d_skip, ssd_norm_g, w_out, ln1_g, ln1_b, w_group, b_group, w_router, b_router,
               w_gate, w_up, w_down, ln2_g, ln2_b)
    params = [_layer_params(l, *weights) for l in range(w_in.shape[0])]
    y_prompt = _trunk(x_prompt, params, _tables(x_prompt.shape[1]))
    y_sample = _trunk(x_sample, params, _tables(x_sample.shape[1]))
    return (y_prompt, y_sample)
```

```python
import functools
import math

import numpy as np
import jax
import jax.numpy as jnp
from jax import lax
from jax.experimental import pallas as pl
from jax.experimental.pallas import tpu as pltpu

F32 = jnp.float32
BF16 = jnp.bfloat16

D_MODEL = 1024
DEPTH = 2
GRID_W = 64
HEAD_DIM = 64
ROPE_THETA = 10000.0
ATT_Q_HEADS = 4
ATT_KV_HEADS = 2
POOL_WINDOWS = (2, 4, 8, 16)
SSD_HEADS = 4
SSD_CONV = 4
SSD_CHUNK = 128
N_GROUPS = 4
EXPERTS_PER_GROUP = 4
N_EXPERTS = 16
EXPERT_FF = 256
DEEPNORM_ALPHA = (2 * DEPTH) ** 0.25
LN_EPS = 1e-5
RMS_EPS = 1e-6

LANES = 128
SUBLANES = 8
HALO = SUBLANES
VMEM_LIMIT = 56 * 1024 * 1024

C_Q, C_K, C_V, C_UF, C_UP, C_Z, C_XBC, C_DT, N_PROJ = 0, 256, 384, 512, 768, 1024, 1280, 1792, 1920
HI = lax.Precision.HIGHEST


def _cparams(sem):
    return pltpu.CompilerParams(dimension_semantics=sem, vmem_limit_bytes=VMEM_LIMIT)


def _dot(a, b):
    return jnp.dot(a, b, preferred_element_type=F32)


def _dot_nt(a, b):
    return lax.dot_general(a, b, (((1,), (1,)), ((), ())), preferred_element_type=F32)


def _silu(x):
    return x * (1.0 / (1.0 + jnp.exp(-x)))


def _layer_norm(x, g, b):
    mu = jnp.mean(x, axis=-1, keepdims=True)
    xc = x - mu
    var = jnp.mean(xc * xc, axis=-1, keepdims=True)
    return xc * lax.rsqrt(var + LN_EPS) * g + b


def _head_sumsq(x, ones_bd):
    sq = x * x
    hi = sq.astype(BF16)
    lo = (sq - hi.astype(F32)).astype(BF16)
    return _dot(jnp.concatenate([hi, lo], axis=1), ones_bd)


def _rope(x, cosv, sinv, first_half):
    w = x.shape[-1]
    partner = jnp.where(first_half, pltpu.roll(x, w - 16, axis=1), pltpu.roll(x, 16, axis=1))
    return x * cosv + partner * sinv


IN_PROJ_SUB = 512


def _in_proj_kernel(x_ref, w_ref, qg_ref, kg_ref, cos_ref, sin_ref, bd_ref, dftc_ref, dtb_ref,
                    q_ref, k_ref, v_ref, ab_ref, up_ref, z_ref, xbc_ref, dt_ref):
    bd = bd_ref[...]
    lane = lax.broadcasted_iota(jnp.int32, (1, LANES), 1)
    first_half = (lane & 31) < 16
    sub = min(IN_PROJ_SUB, x_ref.shape[0])
    nsub = x_ref.shape[0] // sub

    def project(i):
        return _dot(x_ref[i * sub:(i + 1) * sub, :].astype(BF16), w_ref[...])

    h_next = project(0)
    for i in range(nsub):
        h = h_next
        if i + 1 < nsub:
            h_next = project(i + 1)
        rows = slice(i * sub, (i + 1) * sub)
        cosv = cos_ref[rows, :]
        sinv = sin_ref[rows, :]
        for c in range(2):
            qc = h[:, C_Q + c * LANES:C_Q + (c + 1) * LANES]
            ss = _head_sumsq(qc, bd)
            qn = qc * lax.rsqrt(ss * (1.0 / HEAD_DIM) + RMS_EPS) * qg_ref[...]
            qr = _rope(qn, cosv, sinv, first_half) * (HEAD_DIM ** -0.5 * math.log2(math.e))
            q_ref[rows, c * LANES:(c + 1) * LANES] = qr.astype(BF16)
        kc = h[:, C_K:C_K + LANES]
        ss = _head_sumsq(kc, bd)
        kn = kc * lax.rsqrt(ss * (1.0 / HEAD_DIM) + RMS_EPS) * kg_ref[...]
        k_ref[rows, :] = _rope(kn, cosv, sinv, first_half).astype(BF16)
        v_ref[rows, :] = h[:, C_V:C_V + LANES].astype(BF16)
        uf = h[:, C_UF:C_UF + 256].astype(BF16)
        ab_ref[rows, :] = _dot(uf, dftc_ref[...])
        up_ref[rows, :] = h[:, C_UP:C_UP + 256]
        z_ref[rows, :] = h[:, C_Z:C_Z + 256]
        xbc_ref[rows, :] = h[:, C_XBC:C_XBC + 512]
        dr = h[:, C_DT:C_DT + LANES] + dtb_ref[...]
        dt_ref[rows, :] = jnp.maximum(dr, 0.0) + jnp.log(1.0 + jnp.exp(-jnp.abs(dr)))


def _in_proj(x, w, qg, kg, cosv, sinv, bd, dftc, dtb, seq):
    t = x.shape[0]
    tm = min(1024, seq)
    nseq = seq // tm
    full = lambda a: pl.BlockSpec(a.shape, lambda i: (0,) * a.ndim)
    row = lambda wdt: pl.BlockSpec((tm, wdt), lambda i: (i, 0))
    pos = pl.BlockSpec((tm, LANES), lambda i: (i % nseq, 0))
    outs = [(256, BF16), (128, BF16), (128, BF16), (512, F32), (256, F32), (256, F32), (512, F32),
            (128, F32)]
    return pl.pallas_call(
        _in_proj_kernel,
        grid=(t // tm,),
        in_specs=[row(D_MODEL), full(w), full(qg), full(kg), pos, pos, full(bd), full(dftc), full(dtb)],
        out_specs=[row(wd) for wd, _ in outs],
        out_shape=[jax.ShapeDtypeStruct((t, wd), dt) for wd, dt in outs],
        compiler_params=_cparams(("parallel",)),
        name="in_proj",
    )(x, w, qg, kg, cosv, sinv, bd, dftc, dtb)


ATT_KEY_CHUNK = 256
ATT_FIXED_SHIFT_MAX = 60.0


def _attn_kernel(bound_ref, q_ref, k_ref, v_ref, o_ref, vt_ref):
    nch, _, vrows, tk = vt_ref.shape
    tq = q_ref.shape[0]
    lane = lax.broadcasted_iota(jnp.int32, (1, LANES), 1)
    head_lanes = [(lane >> 6) == j for j in range(ATT_KV_HEADS)]

    @pl.when(pl.program_id(1) == 0)
    def _():
        for c in range(nch):
            vt = v_ref[c * tk:(c + 1) * tk, :].astype(F32).T.astype(BF16)
            for j in range(ATT_KV_HEADS):
                vt_ref[c, j, 0:HEAD_DIM, :] = vt[j * HEAD_DIM:(j + 1) * HEAD_DIM, :]
                vt_ref[c, j, HEAD_DIM:vrows, :] = jnp.ones((vrows - HEAD_DIM, tk), BF16)

    qms = []
    for c in range(2):
        qc = q_ref[:, c * LANES:(c + 1) * LANES]
        for j in range(ATT_KV_HEADS):
            qms.append(jnp.where(head_lanes[j], qc, jnp.zeros_like(qc)))

    bound = bound_ref[0]
    fixed_shift_ok = bound <= ATT_FIXED_SHIFT_MAX

    def scores(c):
        kc = k_ref[c * tk:(c + 1) * tk, :]
        return [_dot_nt(kc, qm) for qm in qms]

    def run(shift):
        ms = [jnp.full((1, tq), -jnp.inf, F32)] * ATT_Q_HEADS
        accs = [jnp.zeros((vrows, tq), F32)] * ATT_Q_HEADS
        st_next = scores(0)
        for c in range(nch):
            sts = st_next
            if c + 1 < nch:
                st_next = scores(c + 1)
            for h in range(ATT_Q_HEADS):
                if shift is None:
                    m_new = jnp.maximum(ms[h], jnp.max(sts[h], axis=0, keepdims=True))
                    accs[h] = jnp.exp2(ms[h] - m_new) * accs[h]
                    ms[h] = m_new
                    p = jnp.exp2(sts[h] - m_new).astype(BF16)
                else:
                    p = jnp.exp2(sts[h] - shift).astype(BF16)
                accs[h] = accs[h] + _dot(vt_ref[c, h % ATT_KV_HEADS], p)
        for c in range(2):
            ot = jnp.concatenate([accs[2 * c + j][0:HEAD_DIM] / accs[2 * c + j][HEAD_DIM:HEAD_DIM + 1]
                                  for j in range(ATT_KV_HEADS)], axis=0)
            o_ref[:, c * LANES:(c + 1) * LANES] = ot.T.astype(BF16)

    @pl.when(fixed_shift_ok)
    def _():
        run(bound)

    @pl.when(jnp.logical_not(fixed_shift_ok))
    def _():
        run(None)


def _attention(bound, q, k, v, b, s):
    tq = min(512, s)
    tk = min(ATT_KEY_CHUNK, s)
    q3, k3, v3 = q.reshape(b, s, 256), k.reshape(b, s, LANES), v.reshape(b, s, LANES)
    out = pl.pallas_call(
        _attn_kernel,
        grid=(b, s // tq),
        in_specs=[pl.BlockSpec(memory_space=pltpu.SMEM),
                  pl.BlockSpec((None, tq, 256), lambda bi, i: (bi, i, 0)),
                  pl.BlockSpec((None, s, LANES), lambda bi, i: (bi, 0, 0)),
                  pl.BlockSpec((None, s, LANES), lambda bi, i: (bi, 0, 0))],
        out_specs=pl.BlockSpec((None, tq, 256), lambda bi, i: (bi, i, 0)),
        out_shape=jax.ShapeDtypeStruct((b, s, 256), BF16),
        scratch_shapes=[pltpu.VMEM((s // tk, ATT_KV_HEADS, HEAD_DIM + 16, tk), BF16)],
        compiler_params=_cparams(("parallel", "arbitrary")),
        name="attention",
    )(bound, q3, k3, v3)
    return out.reshape(b * s, 256)


FFT_N2 = 64


def _fourier_kernel(a_ref, b_ref, w1_ref, tc_ref, ts_ref, cs2_ref, o_ref, yr_ref, yi_ref):
    n1 = a_ref.shape[0]
    w1 = w1_ref[...]
    for j in range(FFT_N2 // SUBLANES):
        rows = slice(j * SUBLANES, (j + 1) * SUBLANES)
        at = jnp.swapaxes(a_ref[:, rows, :], 0, 1)
        bt = jnp.swapaxes(b_ref[:, rows, :], 0, 1)
        yrs, yis = [], []
        for i in range(SUBLANES):
            n2 = j * SUBLANES + i
            r = _dot(w1, jnp.concatenate([at[i], bt[i]], axis=1).astype(BF16))
            yr = r[0:n1, 0:LANES] - r[n1:2 * n1, LANES:2 * LANES]
            ym = r[0:n1, LANES:2 * LANES] + r[n1:2 * n1, 0:LANES]
            tc = tc_ref[n2]
            ts = ts_ref[n2]
            yrs.append(yr * tc - ym * ts)
            yis.append(-(ym * tc) - yr * ts)
        yr_ref[:, rows, :] = jnp.swapaxes(jnp.stack(yrs, axis=0), 0, 1)
        yi_ref[:, rows, :] = jnp.swapaxes(jnp.stack(yis, axis=0), 0, 1)
    cs2 = cs2_ref[...]
    for k0 in range(0, n1, SUBLANES):
        ks = range(k0, k0 + SUBLANES)
        yr = jnp.concatenate([yr_ref[k1] for k1 in ks], axis=1)
        yi = jnp.concatenate([yi_ref[k1] for k1 in ks], axis=1)
        z = _dot(cs2, jnp.concatenate([yr, yi], axis=0).astype(BF16))
        zt = jnp.stack([z[:, g * LANES:(g + 1) * LANES] for g in range(SUBLANES)], axis=0)
        o_ref[:, k0:k0 + SUBLANES, :] = jnp.swapaxes(zt, 0, 1)


def _fourier(tabs, ab, b, s):
    w1, tc, ts, cs2 = tabs
    n1 = s // FFT_N2
    full = lambda a: pl.BlockSpec(a.shape, lambda bi, h: (0,) * a.ndim)
    ab4 = ab.reshape(b, n1, FFT_N2, 512)
    out = pl.pallas_call(
        _fourier_kernel,
        grid=(b, 2),
        in_specs=[pl.BlockSpec((None, n1, FFT_N2, LANES), lambda bi, h: (bi, 0, 0, h)),
                  pl.BlockSpec((None, n1, FFT_N2, LANES), lambda bi, h: (bi, 0, 0, 2 + h)),
                  full(w1), full(tc), full(ts), full(cs2)],
        out_specs=pl.BlockSpec((None, FFT_N2, n1, LANES), lambda bi, h: (bi, 0, 0, h)),
        out_shape=jax.ShapeDtypeStruct((b, FFT_N2, n1, 256), F32),
        scratch_shapes=[pltpu.VMEM((n1, FFT_N2, LANES), F32)] * 2,
        compiler_params=_cparams(("parallel", "parallel")),
        name="fourier",
    )(ab4, ab4, w1, tc, ts, cs2)
    return out.reshape(b * s, 256)


def _halo_specs(tt, width, s):
    nb = tt // HALO
    last = s // HALO - 1
    main = pl.BlockSpec((None, tt, width), lambda bi, i: (bi, i, 0))
    prev = pl.BlockSpec((None, HALO, width), lambda bi, i: (bi, jnp.maximum(i * nb - 1, 0), 0))
    nxt = pl.BlockSpec((None, HALO, width), lambda bi, i: (bi, jnp.minimum((i + 1) * nb, last), 0))
    return [main, prev, nxt]


def _with_halo(main_ref, prev_ref, next_ref):
    i = pl.program_id(1)
    n = pl.num_programs(1)
    prev = jnp.where(i > 0, prev_ref[...], 0.0)
    nxt = jnp.where(i < n - 1, next_ref[...], 0.0)
    return jnp.concatenate([prev, main_ref[...], nxt], axis=0)


def _shifted(ext, d, tt):
    n = ext.shape[0]
    r = ext if d == 0 else pltpu.roll(ext, (-d) % n, axis=0)
    return r[HALO:HALO + tt]


def _pool_kernel(u_ref, up_ref, un_ref, w_ref, sc_ref, o_ref, *, seq):
    tt = u_ref.shape[0]
    ext = _with_halo(u_ref, up_ref, un_ref)
    n = ext.shape[0]
    sh = lambda a, d: a if d == 0 else pltpu.roll(a, (-d) % n, axis=0)
    t = pl.program_id(1) * tt + lax.broadcasted_iota(jnp.int32, (tt, 1), 0)

    def inv_count(w):
        lo = jnp.clip(t - w // 2, 0, seq - 1)
        hi = jnp.clip(t + (w - w // 2) - 1, 0, seq - 1)
        return 1.0 / (hi - lo + 1).astype(F32)

    left = (lax.broadcasted_iota(jnp.int32, (1, LANES), 1) >> 6) == 0
    cols = []
    for c in range(2):
        e = ext[:, c * LANES:(c + 1) * LANES]
        u = u_ref[:, c * LANES:(c + 1) * LANES]
        p2 = sh(e, -1) + e
        p4 = sh(p2, -1) + sh(p2, 1)
        if c == 0:
            small, big, ws, wb = p2, p4, POOL_WINDOWS[0], POOL_WINDOWS[1]
        else:
            p8 = sh(p4, -2) + sh(p4, 2)
            p16 = sh(p8, -4) + sh(p8, 4)
            small, big, ws, wb = p8, p16, POOL_WINDOWS[2], POOL_WINDOWS[3]
        mean = jnp.where(left, small[HALO:HALO + tt] * inv_count(ws), big[HALO:HALO + tt] * inv_count(wb))
        cols.append((mean - u).astype(BF16))
    o_ref[...] = (_dot(jnp.concatenate(cols, axis=1), w_ref[...]) * sc_ref[...]).astype(BF16)


def _pool(u, w_bd, scale, b, s):
    tt = min(1024, s)
    u3 = u.reshape(b, s, 256)
    full = lambda a: pl.BlockSpec(a.shape, lambda bi, i: (0,) * a.ndim)
    out = pl.pallas_call(
        functools.partial(_pool_kernel, seq=s),
        grid=(b, s // tt),
        in_specs=_halo_specs(tt, 256, s) + [full(w_bd), full(scale)],
        out_specs=pl.BlockSpec((None, tt, 256), lambda bi, i: (bi, i, 0)),
        out_shape=jax.ShapeDtypeStruct((b, s, 256), BF16),
        compiler_params=_cparams(("parallel", "parallel")),
        name="pool",
    )(u3, u3, u3, w_bd, scale)
    return out.reshape(b * s, 256)


def _conv_kernel(x_ref, xp_ref, xn_ref, w_ref, b_ref, o_ref):
    tt = x_ref.shape[0]
    ext = _with_halo(x_ref, xp_ref, xn_ref)
    acc = b_ref[...] + _shifted(ext, -2, tt) * w_ref[0:1, :]
    for kk in range(1, SSD_CONV):
        acc = acc + _shifted(ext, kk - 2, tt) * w_ref[kk:kk + 1, :]
    o_ref[...] = _silu(acc)


def _conv(xbc, conv_w, conv_b, b, s):
    tt = min(1024, s)
    x3 = xbc.reshape(b, s, 512)
    full = lambda a: pl.BlockSpec(a.shape, lambda bi, i: (0,) * a.ndim)
    return pl.pallas_call(
        _conv_kernel,
        grid=(b, s // tt),
        in_specs=_halo_specs(tt, 512, s) + [full(conv_w), full(conv_b)],
        out_specs=pl.BlockSpec((None, tt, 512), lambda bi, i: (bi, i, 0)),
        out_shape=jax.ShapeDtypeStruct((b, s, 512), F32),
        compiler_params=_cparams(("parallel", "parallel")),
        name="ssd_conv",
    )(x3, x3, x3, conv_w, conv_b)


def _ssd_kernel(xf_ref, dtf_ref, xb_ref, dtb_ref, alog_ref, dskip_ref, yf_ref, yb_ref, carry_ref):
    @pl.when(pl.program_id(1) == 0)
    def _():
        carry_ref[...] = jnp.zeros_like(carry_ref)

    L = SSD_CHUNK
    nch = xf_ref.shape[0] // L
    a_row = -jnp.exp(alog_ref[...])
    dskip = dskip_ref[...]
    r = lax.broadcasted_iota(jnp.int32, (L, L), 0)
    c = lax.broadcasted_iota(jnp.int32, (L, L), 1)
    keeps = (c <= r, c >= r)
    tris = [k.astype(F32) for k in keeps]
    grp = lax.broadcasted_iota(jnp.int32, (1, LANES), 1) >> 6
    head = lax.broadcasted_iota(jnp.int32, (1, 256), 1) >> 6
    row_grp = lax.broadcasted_iota(jnp.int32, (LANES, 1), 0) >> 6

    units = []
    for ci in range(nch):
        units.append(dict(d=0, rows=slice(ci * L, (ci + 1) * L)))
        units.append(dict(d=1, rows=slice((nch - 1 - ci) * L, (nch - ci) * L)))

    for u in units:
        d = u["d"]
        u["off"] = d * SSD_HEADS
        u["xc"] = (xf_ref, xb_ref)[d][u["rows"], :]
        u["dtc"] = (dtf_ref, dtb_ref)[d][u["rows"], :]
        u["cs"] = jnp.dot(tris[d], u["dtc"] * a_row, precision=HI, preferred_element_type=F32)

    prow = lax.broadcasted_iota(jnp.int32, (LANES, 1), 0)
    spread = [jnp.where((prow < 16) & ((prow & 7) == d * SSD_HEADS + head), 1.0, 0.0).astype(BF16)
              for d in range(2)]
    low8 = lax.broadcasted_iota(jnp.int32, (1, LANES), 1) < 8

    for u in units:
        cs, xc = u["cs"], u["xc"]
        u["cs_t"] = cs.T
        tot = cs[0:1, :] if u["d"] else cs[L - 1:L, :]
        fac = jnp.concatenate([u["dtc"], jnp.exp(tot - cs), jnp.exp(cs)], axis=0)
        head_part = fac.astype(BF16).astype(F32)
        fac_x = _dot(jnp.where(low8, head_part, fac - head_part).astype(BF16), spread[u["d"]])
        u["ds_x"], u["ecs_x"] = fac_x[L:2 * L], fac_x[2 * L:3 * L]
        u["etot_x"] = u["ecs_x"][0:1, :] if u["d"] else u["ecs_x"][L - 1:L, :]
        u["xd"] = xc[:, 0:256] * fac_x[0:L]
        u["xd_b"] = u["xd"].astype(BF16)
        u["b_t"] = xc[:, 256:384].T.astype(BF16)
        cmat = xc[:, 384:512]
        u["g"] = [_dot(jnp.where(grp == g, cmat, 0.0).astype(BF16), u["b_t"]) for g in range(2)]

    for u in units:
        cs, cs_t, off, keep = u["cs"], u["cs_t"], u["off"], keeps[u["d"]]
        y = None
        for h in range(SSD_HEADS):
            diff = cs[:, off + h:off + h + 1] - cs_t[off + h:off + h + 1, :]
            decay = jnp.where(keep, jnp.exp(jnp.where(keep, diff, 0.0)), 0.0)
            sc = (u["g"][h // 2] * decay).astype(BF16)
            yd = _dot(sc, u["xd_b"])
            y = yd if y is None else jnp.where(head == h, yd, y)
        u["y"] = y

    for u in units:
        st = _dot(u["b_t"], (u["xd"] * u["ds_x"]).astype(BF16))
        u["st"] = jnp.where(row_grp == (head >> 1), st, 0.0)

    carries = [carry_ref[0], carry_ref[1]]
    for u in units:
        d, xc = u["d"], u["xc"]
        y_off = _dot(xc[:, 384:512].astype(BF16), carries[d].astype(BF16))
        y = u["y"] + y_off * u["ecs_x"]
        carries[d] = carries[d] * u["etot_x"] + u["st"]
        if d == 0:
            yf_ref[u["rows"], :] = y + xc[:, 0:256] * dskip
        else:
            yb_ref[u["rows"], :] = y
    carry_ref[0] = carries[0]
    carry_ref[1] = carries[1]


def _ssd(xc, dt, alog_row, dskip_row, b, s):
    tt = min(512, s)
    nt = s // tt
    dt3 = dt.reshape(b, s, LANES)
    fwd = lambda w: pl.BlockSpec((None, tt, w), lambda bi, i: (bi, i, 0))
    bwd = lambda w: pl.BlockSpec((None, tt, w), lambda bi, i: (bi, nt - 1 - i, 0))
    full = lambda a: pl.BlockSpec(a.shape, lambda bi, i: (0,) * a.ndim)
    yf, yb = pl.pallas_call(
        _ssd_kernel,
        grid=(b, nt),
        in_specs=[fwd(512), fwd(LANES), bwd(512), bwd(LANES), full(alog_row), full(dskip_row)],
        out_specs=[fwd(256), bwd(256)],
        out_shape=[jax.ShapeDtypeStruct((b, s, 256), F32)] * 2,
        scratch_shapes=[pltpu.VMEM((2, LANES, 256), F32)],
        compiler_params=_cparams(("parallel", "arbitrary")),
        name="ssd_scan",
    )(xc, dt3, xc, dt3, alog_row, dskip_row)
    return yf.reshape(b * s, 256), yb.reshape(b * s, 256)


OUT_PROJ_SUB = 256


def _out_proj_kernel(x_ref, att_ref, fmix_ref, pol_ref, yf_ref, yb_ref, z_ref, wf_ref, w_ref, ng_ref,
                     g_ref, b_ref, o_ref):
    ng = ng_ref[...]
    sub = min(OUT_PROJ_SUB, x_ref.shape[0])

    def mixed(i):
        rows = slice(i * sub, (i + 1) * sub)
        fno = _dot(fmix_ref[rows, :].astype(BF16), wf_ref[...]).astype(BF16)
        y = (yf_ref[rows, :] + yb_ref[rows, :]) * _silu(z_ref[rows, :])
        parts = []
        for gi in range(2):
            yg = y[:, gi * LANES:(gi + 1) * LANES]
            ms = jnp.mean(yg * yg, axis=-1, keepdims=True)
            parts.append((yg * lax.rsqrt(ms + RMS_EPS) * ng[:, gi * LANES:(gi + 1) * LANES]).astype(BF16))
        return _dot(jnp.concatenate([att_ref[rows, :], fno, pol_ref[rows, :]] + parts, axis=1), w_ref[...])

    nsub = x_ref.shape[0] // sub
    mix_next = mixed(0)
    for i in range(nsub):
        mix = mix_next
        if i + 1 < nsub:
            mix_next = mixed(i + 1)
        rows = slice(i * sub, (i + 1) * sub)
        o_ref[rows, :] = _layer_norm(DEEPNORM_ALPHA * x_ref[rows, :] + mix, g_ref[...], b_ref[...])


def _out_proj(x, att, fmix, pol, yf, yb, z, wf, w, ng, g, bb):
    t = x.shape[0]
    tm = min(1024, t)
    row = lambda wd: pl.BlockSpec((tm, wd), lambda i: (i, 0))
    full = lambda a: pl.BlockSpec(a.shape, lambda i: (0,) * a.ndim)
    return pl.pallas_call(
        _out_proj_kernel,
        grid=(t // tm,),
        in_specs=[row(D_MODEL)] + [row(256)] * 6 + [full(wf), full(w), full(ng), full(g), full(bb)],
        out_specs=row(D_MODEL),
        out_shape=jax.ShapeDtypeStruct((t, D_MODEL), F32),
        compiler_params=_cparams(("parallel",)),
        name="out_proj_ln",
    )(x, att, fmix, pol, yf, yb, z, wf, w, ng, g, bb)


MOE_TILE = 1024
MOE_CHUNK = 128
MOE_PERM_ROWS = 256


def _router(x, w2, br):
    xh = x.astype(BF16)
    xl = (x - xh.astype(F32)).astype(BF16)
    l1 = _dot(xh, w2)
    logits = l1[:, 0:LANES] + l1[:, LANES:2 * LANES] + _dot(xl, w2[:, 0:LANES]) + br
    lane = lax.broadcasted_iota(jnp.int32, (1, LANES), 1)
    ninf = -jnp.inf
    gl = jnp.where(lane < N_GROUPS, logits, ninf)
    gmax = jnp.max(gl, axis=-1, keepdims=True)
    g_p = 1.0 / jnp.sum(jnp.exp(gl - gmax), axis=-1, keepdims=True)
    g_idx = jnp.min(jnp.where(gl == gmax, lane, LANES), axis=-1, keepdims=True)
    e_lane = lane - N_GROUPS
    in_grp = (e_lane >= 0) & (e_lane < N_EXPERTS) & ((e_lane >> 2) == g_idx)
    el = jnp.where(in_grp, logits, ninf)
    m1 = jnp.max(el, axis=-1, keepdims=True)
    i1 = jnp.min(jnp.where(el == m1, lane, LANES), axis=-1, keepdims=True)
    el2 = jnp.where(lane == i1, ninf, el)
    m2 = jnp.max(el2, axis=-1, keepdims=True)
    i2 = jnp.min(jnp.where(el2 == m2, lane, LANES), axis=-1, keepdims=True)
    e2 = jnp.exp(m2 - m1)
    w1 = 1.0 / (1.0 + e2)
    gates = g_p * (jnp.where(lane == i1, w1, 0.0) + jnp.where(lane == i2, e2 * w1, 0.0))
    return gates, g_idx


def _moe_kernel(x_ref, wr_ref, br_ref, tri_ref, wg_ref, wu_ref, wd_ref, g_ref, b_ref, o_ref,
                xs_ref, gs_ref, pos_ref, acc_ref, off_ref):
    grp = pl.program_id(1)
    n = x_ref.shape[0]
    lane = lax.broadcasted_iota(jnp.int32, (1, LANES), 1)

    @pl.when(grp == 0)
    def _route_and_sort():
        x = x_ref[...]
        gates, g_idx = _router(x, wr_ref[...], br_ref[...])
        onehot = lane == g_idx
        csum = _dot(tri_ref[...], jnp.where(onehot, 1.0, 0.0).astype(BF16))
        cnt = csum[n - 1:n, :].astype(jnp.int32)
        c0, c1, c2 = cnt[0, 0], cnt[0, 1], cnt[0, 2]
        off_ref[0] = 0
        off_ref[1] = c0
        off_ref[2] = c0 + c1
        off_ref[3] = c0 + c1 + c2
        off_ref[4] = n
        offv = jnp.where(lane == 1, c0, jnp.where(lane == 2, c0 + c1, jnp.where(lane == 3, c0 + c1 + c2, 0)))
        pos = jnp.sum(jnp.where(onehot, csum + offv.astype(F32), 0.0), axis=-1, keepdims=True) - 1.0
        pos_ref[...] = jnp.broadcast_to(pos, (n, LANES))
        pos_row = pos_ref[...].T[0:1, :].astype(jnp.int32)
        xb = x.astype(BF16)
        g_hi = gates.astype(BF16)
        g2 = jnp.concatenate([g_hi, (gates - g_hi.astype(F32)).astype(BF16)], axis=1)
        for r in range(n // MOE_PERM_ROWS):
            rows = lax.broadcasted_iota(jnp.int32, (MOE_PERM_ROWS, n), 0) + r * MOE_PERM_ROWS
            perm = jnp.where(rows == pos_row, 1.0, 0.0).astype(BF16)
            sl = slice(r * MOE_PERM_ROWS, (r + 1) * MOE_PERM_ROWS)
            xs_ref[sl, :] = _dot(perm, xb).astype(BF16)
            gg = _dot(perm, g2)
            gs_ref[sl, :] = gg[:, 0:LANES] + gg[:, LANES:2 * LANES]
        acc_ref[...] = jnp.zeros_like(acc_ref)

    lo = off_ref[grp]
    hi = off_ref[grp + 1]

    def chunk(c, carry):
        r0 = pl.multiple_of(c * MOE_CHUNK, MOE_CHUNK)

        @pl.when((lo < r0 + MOE_CHUNK) & (hi > r0))
        def _():
            rs = pl.ds(r0, MOE_CHUNK)
            xs = xs_ref[rs, :]
            gsc = gs_ref[rs, :]
            hid = _silu(_dot(xs, wg_ref[...])) * _dot(xs, wu_ref[...])
            parts = []
            for e in range(EXPERTS_PER_GROUP):
                ge = jnp.sum(jnp.where(lane == N_GROUPS + EXPERTS_PER_GROUP * grp + e, gsc, 0.0),
                             axis=-1, keepdims=True)
                parts.append((hid[:, e * EXPERT_FF:(e + 1) * EXPERT_FF] * ge).astype(BF16))
            acc_ref[rs, :] += _dot(jnp.concatenate(parts, axis=1), wd_ref[...])

        return carry

    lax.fori_loop(0, n // MOE_CHUNK, chunk, 0)

    @pl.when(grp == N_GROUPS - 1)
    def _unsort_and_norm():
        xs_ref[...] = acc_ref[...].astype(BF16)
        cols = lax.broadcasted_iota(jnp.int32, (MOE_PERM_ROWS, n), 1)
        for r in range(n // MOE_PERM_ROWS):
            sl = slice(r * MOE_PERM_ROWS, (r + 1) * MOE_PERM_ROWS)
            perm_t = jnp.where(cols == pos_ref[sl, 0:1].astype(jnp.int32), 1.0, 0.0).astype(BF16)
            y = _dot(perm_t, xs_ref[...])
            o_ref[sl, :] = _layer_norm(DEEPNORM_ALPHA * x_ref[sl, :] + y, g_ref[...], b_ref[...])


def _moe(x, wr, br, wg, wu, wd, g, bb):
    t = x.shape[0]
    n = min(MOE_TILE, t)
    tri = jnp.asarray(np.tril(np.ones((n, n), np.float32)), dtype=BF16)
    full = lambda a: pl.BlockSpec(a.shape, lambda i, e: (0,) * a.ndim)
    wspec = pl.BlockSpec((None, D_MODEL, D_MODEL), lambda i, e: (e, 0, 0))
    return pl.pallas_call(
        _moe_kernel,
        grid=(t // n, N_GROUPS),
        in_specs=[pl.BlockSpec((n, D_MODEL), lambda i, e: (i, 0)), full(wr), full(br), full(tri),
                  wspec, wspec, wspec, full(g), full(bb)],
        out_specs=pl.BlockSpec((n, D_MODEL), lambda i, e: (i, 0)),
        out_shape=jax.ShapeDtypeStruct((t, D_MODEL), F32),
        scratch_shapes=[pltpu.VMEM((n, D_MODEL), BF16), pltpu.VMEM((n, LANES), F32),
                        pltpu.VMEM((n, LANES), F32), pltpu.VMEM((n, D_MODEL), F32),
                        pltpu.SMEM((8,), jnp.int32)],
        compiler_params=_cparams(("parallel", "arbitrary")),
        name="moe_ln",
    )(x, wr, br, tri, wg, wu, wd, g, bb)


def _rope_tables(seq):
    rows = seq // GRID_W
    row = jnp.repeat(jnp.arange(rows, dtype=F32), GRID_W)
    col = jnp.tile(jnp.arange(GRID_W, dtype=F32), rows)
    half = HEAD_DIM // 2
    freqs = 1.0 / (ROPE_THETA ** (jnp.arange(0, half, 2, dtype=F32) / half))
    ar, ac = row[:, None] * freqs, col[:, None] * freqs
    cosv = jnp.concatenate([jnp.cos(ar), jnp.cos(ar), jnp.cos(ac), jnp.cos(ac)], axis=-1)
    sinv = jnp.concatenate([-jnp.sin(ar), jnp.sin(ar), -jnp.sin(ac), jnp.sin(ac)], axis=-1)
    return jnp.tile(cosv, (1, 2)), jnp.tile(sinv, (1, 2))


def _position_dft(seq):
    n1 = seq // FFT_N2
    dft = lambda n: 2.0 * np.pi * ((np.arange(n)[:, None] * np.arange(n)[None, :]) % n) / n
    a1 = dft(n1)
    w1 = np.concatenate([np.cos(a1), np.sin(a1)], axis=0) * seq ** -0.5
    at = 2.0 * np.pi * (np.arange(FFT_N2)[:, None] * np.arange(n1)[None, :]) / seq
    lanes = lambda t: jnp.asarray(np.repeat(t[:, :, None], LANES, axis=2), dtype=F32)
    a2 = dft(FFT_N2)
    return (jnp.asarray(w1, dtype=BF16), lanes(np.cos(at)), lanes(np.sin(at)),
            jnp.asarray(np.concatenate([np.cos(a2), np.sin(a2)], axis=1), dtype=BF16))


def _channel_dft():
    n = np.arange(HEAD_DIM)
    ang = 2.0 * np.pi * ((n[:, None] * n[None, :]) % HEAD_DIM) / HEAD_DIM
    eye = np.eye(4)
    cb = np.kron(eye, np.cos(ang)) * HEAD_DIM ** -0.5
    sb = np.kron(eye, np.sin(ang)) * HEAD_DIM ** -0.5
    return jnp.asarray(np.concatenate([cb, sb], axis=1), dtype=BF16)


def _layer_params(l, w_in, q_norm_g, k_norm_g, w_fnet, w_pool, pool_scale, conv_w, conv_b, dt_bias,
                  a_log, d_skip, ssd_norm_g, w_out, ln1_g, ln1_b, w_group, b_group, w_router, b_router,
                  w_gate, w_up, w_down, ln2_g, ln2_b):
    wi = w_in[l]
    hperm = np.array([0, 2, 1, 3])
    wq = wi[:, 0:256].reshape(D_MODEL, 4, HEAD_DIM)[:, hperm].reshape(D_MODEL, 256)
    twice = lambda v: jnp.concatenate([v, v], axis=-1)
    w_proj = jnp.concatenate([wq, wi[:, 256:1792], jnp.pad(twice(wi[:, 1792:1800]), ((0, 0), (0, 112)))],
                             axis=1).astype(BF16)
    wo = w_out[l]
    wo_att = wo[0:256].reshape(4, HEAD_DIM, D_MODEL)[hperm].reshape(256, D_MODEL)
    w_o = jnp.concatenate([wo_att, wo[256:]], axis=0).astype(BF16)
    row = lambda v, n: jnp.pad(v.reshape(1, -1), ((0, 0), (0, n - v.size)))
    w_pool_bd = jnp.zeros((256, 256), F32)
    for gi in range(4):
        w_pool_bd = w_pool_bd.at[gi * 64:(gi + 1) * 64, gi * 64:(gi + 1) * 64].set(w_pool[l, gi])
    wr = jnp.pad(jnp.concatenate([w_group[l], w_router[l]], axis=1), ((0, 0), (0, LANES - 20)))
    wr_hi = wr.astype(BF16)
    by_group = lambda w: w.reshape(N_GROUPS, EXPERTS_PER_GROUP, D_MODEL, EXPERT_FF).transpose(
        0, 2, 1, 3).reshape(N_GROUPS, D_MODEL, EXPERTS_PER_GROUP * EXPERT_FF).astype(BF16)
    att_bound = (1.01 * HEAD_DIM ** -0.5 * math.log2(math.e) * HEAD_DIM
                 * jnp.max(jnp.abs(q_norm_g[l])) * jnp.max(jnp.abs(k_norm_g[l]))).reshape(1).astype(F32)
    return dict(
        w_proj=w_proj, att_bound=att_bound,
        qg=jnp.tile(q_norm_g[l], 2).reshape(1, LANES), kg=jnp.tile(k_norm_g[l], 2).reshape(1, LANES),
        dtb=row(twice(dt_bias[l].reshape(-1)), LANES), w_fnet=w_fnet[l].astype(BF16),
        w_pool=w_pool_bd.astype(BF16), pool_scale=pool_scale[l].reshape(1, 256),
        conv_w=conv_w[l], conv_b=conv_b[l].reshape(1, 512),
        alog=row(twice(a_log[l].reshape(-1)), LANES), dskip=jnp.repeat(d_skip[l], 64).reshape(1, 256),
        ssd_ng=ssd_norm_g[l].reshape(1, 256), w_o=w_o,
        ln1_g=ln1_g[l].reshape(1, D_MODEL), ln1_b=ln1_b[l].reshape(1, D_MODEL),
        wr=jnp.concatenate([wr_hi, (wr - wr_hi.astype(F32)).astype(BF16)], axis=1),
        br=row(jnp.concatenate([b_group[l], b_router[l]]), LANES),
        wg=by_group(w_gate[l]), wu=by_group(w_up[l]),
        wd=w_down[l].reshape(N_GROUPS, EXPERTS_PER_GROUP * EXPERT_FF, D_MODEL).astype(BF16),
        ln2_g=ln2_g[l].reshape(1, D_MODEL), ln2_b=ln2_b[l].reshape(1, D_MODEL),
    )


def _trunk(x3, params, tables):
    b, s, _ = x3.shape
    x = x3.reshape(b * s, D_MODEL)
    cosv, sinv, pos_dft, bd, dftc = tables
    for p in params:
        q, k, v, ab, up, z, xbc, dt = _in_proj(x, p["w_proj"], p["qg"], p["kg"], cosv, sinv, bd, dftc,
                                               p["dtb"], s)
        att = _attention(p["att_bound"], q, k, v, b, s)
        fmix = _fourier(pos_dft, ab, b, s)
        pol = _pool(up, p["w_pool"], p["pool_scale"], b, s)
        xc = _conv(xbc, p["conv_w"], p["conv_b"], b, s)
        yf, yb = _ssd(xc, dt, p["alog"], p["dskip"], b, s)
        x = _out_proj(x, att, fmix, pol, yf, yb, z, p["w_fnet"], p["w_o"], p["ssd_ng"], p["ln1_g"],
                      p["ln1_b"])
        x = _moe(x, p["wr"], p["br"], p["wg"], p["wu"], p["wd"], p["ln2_g"], p["ln2_b"])
    return x.reshape(b, s, D_MODEL)


def _tables(seq):
    cosv, sinv = _rope_tables(seq)
    lane = np.arange(LANES)
    bd = jnp.asarray(np.tile((lane[:, None] // HEAD_DIM) == (lane[None, :] // HEAD_DIM), (2, 1)), dtype=BF16)
    return cosv, sinv, _position_dft(seq), bd, _channel_dft()


def kernel(x_prompt, x_sample, w_in, q_norm_g, k_norm_g, w_fnet, w_pool, pool_scale, conv_w, conv_b, dt_bias, a_log, d_skip, ssd_norm_g, w_out, ln1_g, ln1_b, w_group, b_group, w_router, b_router, w_gate, w_up, w_down, ln2_g, ln2_b):
    weights = (w_in, q_norm_g, k_norm_g, w_fnet, w_pool, pool_scale, conv_w, conv_b, dt_bias, a_log,
               d_skip, ssd_norm_g, w_out, ln1_g, ln1_b, w_group, b_group, w_router, b_router,
               w_gate, w_up, w_down, ln2_g, ln2_b)
    params = [_layer_params(l, *weights) for l in range(w_in.shape[0])]
    y_prompt = _trunk(x_prompt, params, _tables(x_prompt.shape[1]))
    y_sample = _trunk(x_sample, params, _tables(x_sample.shape[1]))
    return (y_prompt, y_sample)
```

```python
import functools
import math

import numpy as np
import jax
import jax.numpy as jnp
from jax import lax
from jax.experimental import pallas as pl
from jax.experimental.pallas import tpu as pltpu

F32 = jnp.float32
BF16 = jnp.bfloat16

D_MODEL = 1024
DEPTH = 2
GRID_W = 64
HEAD_DIM = 64
ROPE_THETA = 10000.0
ATT_Q_HEADS = 4
ATT_KV_HEADS = 2
POOL_WINDOWS = (2, 4, 8, 16)
SSD_HEADS = 4
SSD_CONV = 4
SSD_CHUNK = 128
N_GROUPS = 4
EXPERTS_PER_GROUP = 4
N_EXPERTS = 16
EXPERT_FF = 256
DEEPNORM_ALPHA = (2 * DEPTH) ** 0.25
LN_EPS = 1e-5
RMS_EPS = 1e-6

LANES = 128
SUBLANES = 8
HALO = SUBLANES
VMEM_LIMIT = 56 * 1024 * 1024

C_Q, C_K, C_V, C_UF, C_UP, C_Z, C_XBC, C_DT, N_PROJ = 0, 256, 384, 512, 768, 1024, 1280, 1792, 1920
HI = lax.Precision.HIGHEST


def _cparams(sem):
    return pltpu.CompilerParams(dimension_semantics=sem, vmem_limit_bytes=VMEM_LIMIT)


def _dot(a, b):
    return jnp.dot(a, b, preferred_element_type=F32)


def _dot_nt(a, b):
    return lax.dot_general(a, b, (((1,), (1,)), ((), ())), preferred_element_type=F32)


def _silu(x):
    return x * (1.0 / (1.0 + jnp.exp(-x)))


def _layer_norm(x, g, b):
    mu = jnp.mean(x, axis=-1, keepdims=True)
    xc = x - mu
    var = jnp.mean(xc * xc, axis=-1, keepdims=True)
    return xc * lax.rsqrt(var + LN_EPS) * g + b


def _head_sumsq(x, ones_bd):
    sq = x * x
    hi = sq.astype(BF16)
    lo = (sq - hi.astype(F32)).astype(BF16)
    return _dot(jnp.concatenate([hi, lo], axis=1), ones_bd)


def _rope(x, cosv, sinv, first_half):
    w = x.shape[-1]
    partner = jnp.where(first_half, pltpu.roll(x, w - 16, axis=1), pltpu.roll(x, 16, axis=1))
    return x * cosv + partner * sinv


IN_PROJ_SUB = 512


def _in_proj_kernel(x_ref, w_ref, qg_ref, kg_ref, cos_ref, sin_ref, bd_ref, dftc_ref, dtb_ref,
                    q_ref, k_ref, v_ref, ab_ref, up_ref, z_ref, xbc_ref, dt_ref):
    bd = bd_ref[...]
    lane = lax.broadcasted_iota(jnp.int32, (1, LANES), 1)
    first_half = (lane & 31) < 16
    sub = min(IN_PROJ_SUB, x_ref.shape[0])
    nsub = x_ref.shape[0] // sub

    def project(i):
        return _dot(x_ref[i * sub:(i + 1) * sub, :].astype(BF16), w_ref[...])

    h_next = project(0)
    for i in range(nsub):
        h = h_next
        if i + 1 < nsub:
            h_next = project(i + 1)
        rows = slice(i * sub, (i + 1) * sub)
        cosv = cos_ref[rows, :]
        sinv = sin_ref[rows, :]
        for c in range(2):
            qc = h[:, C_Q + c * LANES:C_Q + (c + 1) * LANES]
            ss = _head_sumsq(qc, bd)
            qn = qc * lax.rsqrt(ss * (1.0 / HEAD_DIM) + RMS_EPS) * qg_ref[...]
            qr = _rope(qn, cosv, sinv, first_half) * (HEAD_DIM ** -0.5 * math.log2(math.e))
            q_ref[rows, c * LANES:(c + 1) * LANES] = qr.astype(BF16)
        kc = h[:, C_K:C_K + LANES]
        ss = _head_sumsq(kc, bd)
        kn = kc * lax.rsqrt(ss * (1.0 / HEAD_DIM) + RMS_EPS) * kg_ref[...]
        k_ref[rows, :] = _rope(kn, cosv, sinv, first_half).astype(BF16)
        v_ref[rows, :] = h[:, C_V:C_V + LANES].astype(BF16)
        uf = h[:, C_UF:C_UF + 256].astype(BF16)
        ab_ref[rows, :] = _dot(uf, dftc_ref[...])
        up_ref[rows, :] = h[:, C_UP:C_UP + 256]
        z_ref[rows, :] = h[:, C_Z:C_Z + 256]
        xbc_ref[rows, :] = h[:, C_XBC:C_XBC + 512]
        dr = h[:, C_DT:C_DT + LANES] + dtb_ref[...]
        dt_ref[rows, :] = jnp.maximum(dr, 0.0) + jnp.log(1.0 + jnp.exp(-jnp.abs(dr)))


def _in_proj(x, w, qg, kg, cosv, sinv, bd, dftc, dtb, seq):
    t = x.shape[0]
    tm = min(1024, seq)
    nseq = seq // tm
    full = lambda a: pl.BlockSpec(a.shape, lambda i: (0,) * a.ndim)
    row = lambda wdt: pl.BlockSpec((tm, wdt), lambda i: (i, 0))
    pos = pl.BlockSpec((tm, LANES), lambda i: (i % nseq, 0))
    outs = [(256, BF16), (128, BF16), (128, BF16), (512, F32), (256, F32), (256, F32), (512, F32),
            (128, F32)]
    return pl.pallas_call(
        _in_proj_kernel,
        grid=(t // tm,),
        in_specs=[row(D_MODEL), full(w), full(qg), full(kg), pos, pos, full(bd), full(dftc), full(dtb)],
        out_specs=[row(wd) for wd, _ in outs],
        out_shape=[jax.ShapeDtypeStruct((t, wd), dt) for wd, dt in outs],
        compiler_params=_cparams(("parallel",)),
        name="in_proj",
    )(x, w, qg, kg, cosv, sinv, bd, dftc, dtb)


ATT_KEY_CHUNK = 256
ATT_FIXED_SHIFT_MAX = 60.0


def _attn_kernel(bound_ref, q_ref, k_ref, v_ref, o_ref, vt_ref):
    nch, _, vrows, tk = vt_ref.shape
    tq = q_ref.shape[0]
    lane = lax.broadcasted_iota(jnp.int32, (1, LANES), 1)
    head_lanes = [(lane >> 6) == j for j in range(ATT_KV_HEADS)]

    @pl.when(pl.program_id(1) == 0)
    def _():
        for c in range(nch):
            vt = v_ref[c * tk:(c + 1) * tk, :].astype(F32).T.astype(BF16)
            for j in range(ATT_KV_HEADS):
                vt_ref[c, j, 0:HEAD_DIM, :] = vt[j * HEAD_DIM:(j + 1) * HEAD_DIM, :]
                vt_ref[c, j, HEAD_DIM:vrows, :] = jnp.ones((vrows - HEAD_DIM, tk), BF16)

    qms = []
    for c in range(2):
        qc = q_ref[:, c * LANES:(c + 1) * LANES]
        for j in range(ATT_KV_HEADS):
            qms.append(jnp.where(head_lanes[j], qc, jnp.zeros_like(qc)))

    bound = bound_ref[0]
    fixed_shift_ok = bound <= ATT_FIXED_SHIFT_MAX

    def scores(c):
        kc = k_ref[c * tk:(c + 1) * tk, :]
        return [_dot_nt(kc, qm) for qm in qms]

    def run(shift):
        ms = [jnp.full((1, tq), -jnp.inf, F32)] * ATT_Q_HEADS
        accs = [jnp.zeros((vrows, tq), F32)] * ATT_Q_HEADS
        st_next = scores(0)
        for c in range(nch):
            sts = st_next
            if c + 1 < nch:
                st_next = scores(c + 1)
            for h in range(ATT_Q_HEADS):
                if shift is None:
                    m_new = jnp.maximum(ms[h], jnp.max(sts[h], axis=0, keepdims=True))
                    accs[h] = jnp.exp2(ms[h] - m_new) * accs[h]
                    ms[h] = m_new
                    p = jnp.exp2(sts[h] - m_new).astype(BF16)
                else:
                    p = jnp.exp2(sts[h] - shift).astype(BF16)
                accs[h] = accs[h] + _dot(vt_ref[c, h % ATT_KV_HEADS], p)
        for c in range(2):
            ot = jnp.concatenate([accs[2 * c + j][0:HEAD_DIM] / accs[2 * c + j][HEAD_DIM:HEAD_DIM + 1]
                                  for j in range(ATT_KV_HEADS)], axis=0)
            o_ref[:, c * LANES:(c + 1) * LANES] = ot.T.astype(BF16)

    @pl.when(fixed_shift_ok)
    def _():
        run(bound)

    @pl.when(jnp.logical_not(fixed_shift_ok))
    def _():
        run(None)


def _attention(bound, q, k, v, b, s):
    tq = min(512, s)
    tk = min(ATT_KEY_CHUNK, s)
    q3, k3, v3 = q.reshape(b, s, 256), k.reshape(b, s, LANES), v.reshape(b, s, LANES)
    out = pl.pallas_call(
        _attn_kernel,
        grid=(b, s // tq),
        in_specs=[pl.BlockSpec(memory_space=pltpu.SMEM),
                  pl.BlockSpec((None, tq, 256), lambda bi, i: (bi, i, 0)),
                  pl.BlockSpec((None, s, LANES), lambda bi, i: (bi, 0, 0)),
                  pl.BlockSpec((None, s, LANES), lambda bi, i: (bi, 0, 0))],
        out_specs=pl.BlockSpec((None, tq, 256), lambda bi, i: (bi, i, 0)),
        out_shape=jax.ShapeDtypeStruct((b, s, 256), BF16),
        scratch_shapes=[pltpu.VMEM((s // tk, ATT_KV_HEADS, HEAD_DIM + 16, tk), BF16)],
        compiler_params=_cparams(("parallel", "arbitrary")),
        name="attention",
    )(bound, q3, k3, v3)
    return out.reshape(b * s, 256)


FFT_N2 = 64


def _fourier_kernel(a_ref, b_ref, w1_ref, tc_ref, ts_ref, cs2_ref, o_ref, yr_ref, yi_ref):
    n1 = a_ref.shape[0]
    w1 = w1_ref[...]
    for j in range(FFT_N2 // SUBLANES):
        rows = slice(j * SUBLANES, (j + 1) * SUBLANES)
        at = jnp.swapaxes(a_ref[:, rows, :], 0, 1)
        bt = jnp.swapaxes(b_ref[:, rows, :], 0, 1)
        yrs, yis = [], []
        for i in range(SUBLANES):
            n2 = j * SUBLANES + i
            r = _dot(w1, jnp.concatenate([at[i], bt[i]], axis=1).astype(BF16))
            yr = r[0:n1, 0:LANES] - r[n1:2 * n1, LANES:2 * LANES]
            ym = r[0:n1, LANES:2 * LANES] + r[n1:2 * n1, 0:LANES]
            tc = tc_ref[n2]
            ts = ts_ref[n2]
            yrs.append(yr * tc - ym * ts)
            yis.append(-(ym * tc) - yr * ts)
        yr_ref[:, rows, :] = jnp.swapaxes(jnp.stack(yrs, axis=0), 0, 1)
        yi_ref[:, rows, :] = jnp.swapaxes(jnp.stack(yis, axis=0), 0, 1)
    cs2 = cs2_ref[...]
    for k0 in range(0, n1, SUBLANES):
        ks = range(k0, k0 + SUBLANES)
        yr = jnp.concatenate([yr_ref[k1] for k1 in ks], axis=1)
        yi = jnp.concatenate([yi_ref[k1] for k1 in ks], axis=1)
        z = _dot(cs2, jnp.concatenate([yr, yi], axis=0).astype(BF16))
        zt = jnp.stack([z[:, g * LANES:(g + 1) * LANES] for g in range(SUBLANES)], axis=0)
        o_ref[:, k0:k0 + SUBLANES, :] = jnp.swapaxes(zt, 0, 1)


def _fourier(tabs, ab, b, s):
    w1, tc, ts, cs2 = tabs
    n1 = s // FFT_N2
    full = lambda a: pl.BlockSpec(a.shape, lambda bi, h: (0,) * a.ndim)
    ab4 = ab.reshape(b, n1, FFT_N2, 512)
    out = pl.pallas_call(
        _fourier_kernel,
        grid=(b, 2),
        in_specs=[pl.BlockSpec((None, n1, FFT_N2, LANES), lambda bi, h: (bi, 0, 0, h)),
                  pl.BlockSpec((None, n1, FFT_N2, LANES), lambda bi, h: (bi, 0, 0, 2 + h)),
                  full(w1), full(tc), full(ts), full(cs2)],
        out_specs=pl.BlockSpec((None, FFT_N2, n1, LANES), lambda bi, h: (bi, 0, 0, h)),
        out_shape=jax.ShapeDtypeStruct((b, FFT_N2, n1, 256), F32),
        scratch_shapes=[pltpu.VMEM((n1, FFT_N2, LANES), F32)] * 2,
        compiler_params=_cparams(("parallel", "parallel")),
        name="fourier",
    )(ab4, ab4, w1, tc, ts, cs2)
    return out.reshape(b * s, 256)


def _halo_specs(tt, width, s):
    nb = tt // HALO
    last = s // HALO - 1
    main = pl.BlockSpec((None, tt, width), lambda bi, i: (bi, i, 0))
    prev = pl.BlockSpec((None, HALO, width), lambda bi, i: (bi, jnp.maximum(i * nb - 1, 0), 0))
    nxt = pl.BlockSpec((None, HALO, width), lambda bi, i: (bi, jnp.minimum((i + 1) * nb, last), 0))
    return [main, prev, nxt]


def _with_halo(main_ref, prev_ref, next_ref):
    i = pl.program_id(1)
    n = pl.num_programs(1)
    prev = jnp.where(i > 0, prev_ref[...], 0.0)
    nxt = jnp.where(i < n - 1, next_ref[...], 0.0)
    return jnp.concatenate([prev, main_ref[...], nxt], axis=0)


def _shifted(ext, d, tt):
    n = ext.shape[0]
    r = ext if d == 0 else pltpu.roll(ext, (-d) % n, axis=0)
    return r[HALO:HALO + tt]


def _pool_kernel(u_ref, up_ref, un_ref, w_ref, sc_ref, o_ref, *, seq):
    tt = u_ref.shape[0]
    ext = _with_halo(u_ref, up_ref, un_ref)
    n = ext.shape[0]
    sh = lambda a, d: a if d == 0 else pltpu.roll(a, (-d) % n, axis=0)
    t = pl.program_id(1) * tt + lax.broadcasted_iota(jnp.int32, (tt, 1), 0)

    def inv_count(w):
        lo = jnp.clip(t - w // 2, 0, seq - 1)
        hi = jnp.clip(t + (w - w // 2) - 1, 0, seq - 1)
        return 1.0 / (hi - lo + 1).astype(F32)

    left = (lax.broadcasted_iota(jnp.int32, (1, LANES), 1) >> 6) == 0
    cols = []
    for c in range(2):
        e = ext[:, c * LANES:(c + 1) * LANES]
        u = u_ref[:, c * LANES:(c + 1) * LANES]
        p2 = sh(e, -1) + e
        p4 = sh(p2, -1) + sh(p2, 1)
        if c == 0:
            small, big, ws, wb = p2, p4, POOL_WINDOWS[0], POOL_WINDOWS[1]
        else:
            p8 = sh(p4, -2) + sh(p4, 2)
            p16 = sh(p8, -4) + sh(p8, 4)
            small, big, ws, wb = p8, p16, POOL_WINDOWS[2], POOL_WINDOWS[3]
        mean = jnp.where(left, small[HALO:HALO + tt] * inv_count(ws), big[HALO:HALO + tt] * inv_count(wb))
        cols.append((mean - u).astype(BF16))
    o_ref[...] = (_dot(jnp.concatenate(cols, axis=1), w_ref[...]) * sc_ref[...]).astype(BF16)


def _pool(u, w_bd, scale, b, s):
    tt = min(1024, s)
    u3 = u.reshape(b, s, 256)
    full = lambda a: pl.BlockSpec(a.shape, lambda bi, i: (0,) * a.ndim)
    out = pl.pallas_call(
        functools.partial(_pool_kernel, seq=s),
        grid=(b, s // tt),
        in_specs=_halo_specs(tt, 256, s) + [full(w_bd), full(scale)],
        out_specs=pl.BlockSpec((None, tt, 256), lambda bi, i: (bi, i, 0)),
        out_shape=jax.ShapeDtypeStruct((b, s, 256), BF16),
        compiler_params=_cparams(("parallel", "parallel")),
        name="pool",
    )(u3, u3, u3, w_bd, scale)
    return out.reshape(b * s, 256)


def _conv_kernel(x_ref, xp_ref, xn_ref, w_ref, b_ref, o_ref):
    tt = x_ref.shape[0]
    ext = _with_halo(x_ref, xp_ref, xn_ref)
    acc = b_ref[...] + _shifted(ext, -2, tt) * w_ref[0:1, :]
    for kk in range(1, SSD_CONV):
        acc = acc + _shifted(ext, kk - 2, tt) * w_ref[kk:kk + 1, :]
    o_ref[...] = _silu(acc)


def _conv(xbc, conv_w, conv_b, b, s):
    tt = min(1024, s)
    x3 = xbc.reshape(b, s, 512)
    full = lambda a: pl.BlockSpec(a.shape, lambda bi, i: (0,) * a.ndim)
    return pl.pallas_call(
        _conv_kernel,
        grid=(b, s // tt),
        in_specs=_halo_specs(tt, 512, s) + [full(conv_w), full(conv_b)],
        out_specs=pl.BlockSpec((None, tt, 512), lambda bi, i: (bi, i, 0)),
        out_shape=jax.ShapeDtypeStruct((b, s, 512), F32),
        compiler_params=_cparams(("parallel", "parallel")),
        name="ssd_conv",
    )(x3, x3, x3, conv_w, conv_b)


def _ssd_kernel(xf_ref, dtf_ref, xb_ref, dtb_ref, alog_ref, dskip_ref, yf_ref, yb_ref, carry_ref):
    @pl.when(pl.program_id(1) == 0)
    def _():
        carry_ref[...] = jnp.zeros_like(carry_ref)

    L = SSD_CHUNK
    nch = xf_ref.shape[0] // L
    a_row = -jnp.exp(alog_ref[...])
    dskip = dskip_ref[...]
    r = lax.broadcasted_iota(jnp.int32, (L, L), 0)
    c = lax.broadcasted_iota(jnp.int32, (L, L), 1)
    keeps = (c <= r, c >= r)
    tris = [k.astype(F32) for k in keeps]
    grp = lax.broadcasted_iota(jnp.int32, (1, LANES), 1) >> 6
    head = lax.broadcasted_iota(jnp.int32, (1, 256), 1) >> 6
    row_grp = lax.broadcasted_iota(jnp.int32, (LANES, 1), 0) >> 6

    units = []
    for ci in range(nch):
        units.append(dict(d=0, rows=slice(ci * L, (ci + 1) * L)))
        units.append(dict(d=1, rows=slice((nch - 1 - ci) * L, (nch - ci) * L)))

    for u in units:
        d = u["d"]
        u["off"] = d * SSD_HEADS
        u["xc"] = (xf_ref, xb_ref)[d][u["rows"], :]
        u["dtc"] = (dtf_ref, dtb_ref)[d][u["rows"], :]
        u["cs"] = jnp.dot(tris[d], u["dtc"] * a_row, precision=HI, preferred_element_type=F32)

    prow = lax.broadcasted_iota(jnp.int32, (LANES, 1), 0)
    spread = [jnp.where((prow < 16) & ((prow & 7) == d * SSD_HEADS + head), 1.0, 0.0).astype(BF16)
              for d in range(2)]
    low8 = lax.broadcasted_iota(jnp.int32, (1, LANES), 1) < 8

    for u in units:
        cs, xc = u["cs"], u["xc"]
        u["cs_t"] = cs.T
        tot = cs[0:1, :] if u["d"] else cs[L - 1:L, :]
        fac = jnp.concatenate([u["dtc"], jnp.exp(tot - cs), jnp.exp(cs)], axis=0)
        head_part = fac.astype(BF16).astype(F32)
        fac_x = _dot(jnp.where(low8, head_part, fac - head_part).astype(BF16), spread[u["d"]])
        u["ds_x"], u["ecs_x"] = fac_x[L:2 * L], fac_x[2 * L:3 * L]
        u["etot_x"] = u["ecs_x"][0:1, :] if u["d"] else u["ecs_x"][L - 1:L, :]
        u["xd"] = xc[:, 0:256] * fac_x[0:L]
        u["xd_b"] = u["xd"].astype(BF16)
        u["b_t"] = xc[:, 256:384].T.astype(BF16)
        cmat = xc[:, 384:512]
        u["g"] = [_dot(jnp.where(grp == g, cmat, 0.0).astype(BF16), u["b_t"]) for g in range(2)]

    for u in units:
        cs, cs_t, off, keep = u["cs"], u["cs_t"], u["off"], keeps[u["d"]]
        y = None
        for h in range(SSD_HEADS):
            diff = cs[:, off + h:off + h + 1] - cs_t[off + h:off + h + 1, :]
            decay = jnp.where(keep, jnp.exp(jnp.where(keep, diff, 0.0)), 0.0)
            sc = (u["g"][h // 2] * decay).astype(BF16)
            yd = _dot(sc, u["xd_b"])
            y = yd if y is None else jnp.where(head == h, yd, y)
        u["y"] = y

    for u in units:
        st = _dot(u["b_t"], (u["xd"] * u["ds_x"]).astype(BF16))
        u["st"] = jnp.where(row_grp == (head >> 1), st, 0.0)

    carries = [carry_ref[0], carry_ref[1]]
    for u in units:
        d, xc = u["d"], u["xc"]
        y_off = _dot(xc[:, 384:512].astype(BF16), carries[d].astype(BF16))
        y = u["y"] + y_off * u["ecs_x"]
        carries[d] = carries[d] * u["etot_x"] + u["st"]
        if d == 0:
            yf_ref[u["rows"], :] = y + xc[:, 0:256] * dskip
        else:
            yb_ref[u["rows"], :] = y
    carry_ref[0] = carries[0]
    carry_ref[1] = carries[1]


def _ssd(xc, dt, alog_row, dskip_row, b, s):
    tt = min(512, s)
    nt = s // tt
    dt3 = dt.reshape(b, s, LANES)
    fwd = lambda w: pl.BlockSpec((None, tt, w), lambda bi, i: (bi, i, 0))
    bwd = lambda w: pl.BlockSpec((None, tt, w), lambda bi, i: (bi, nt - 1 - i, 0))
    full = lambda a: pl.BlockSpec(a.shape, lambda bi, i: (0,) * a.ndim)
    yf, yb = pl.pallas_call(
        _ssd_kernel,
        grid=(b, nt),
        in_specs=[fwd(512), fwd(LANES), bwd(512), bwd(LANES), full(alog_row), full(dskip_row)],
        out_specs=[fwd(256), bwd(256)],
        out_shape=[jax.ShapeDtypeStruct((b, s, 256), F32)] * 2,
        scratch_shapes=[pltpu.VMEM((2, LANES, 256), F32)],
        compiler_params=_cparams(("parallel", "arbitrary")),
        name="ssd_scan",
    )(xc, dt3, xc, dt3, alog_row, dskip_row)
    return yf.reshape(b * s, 256), yb.reshape(b * s, 256)


OUT_PROJ_SUB = 256


def _out_proj_kernel(x_ref, att_ref, fmix_ref, pol_ref, yf_ref, yb_ref, z_ref, wf_ref, w_ref, ng_ref,
                     g_ref, b_ref, o_ref):
    ng = ng_ref[...]
    sub = min(OUT_PROJ_SUB, x_ref.shape[0])

    def mixed(i):
        rows = slice(i * sub, (i + 1) * sub)
        fno = _dot(fmix_ref[rows, :].astype(BF16), wf_ref[...]).astype(BF16)
        y = (yf_ref[rows, :] + yb_ref[rows, :]) * _silu(z_ref[rows, :])
        parts = []
        for gi in range(2):
            yg = y[:, gi * LANES:(gi + 1) * LANES]
            ms = jnp.mean(yg * yg, axis=-1, keepdims=True)
            parts.append((yg * lax.rsqrt(ms + RMS_EPS) * ng[:, gi * LANES:(gi + 1) * LANES]).astype(BF16))
        return _dot(jnp.concatenate([att_ref[rows, :], fno, pol_ref[rows, :]] + parts, axis=1), w_ref[...])

    nsub = x_ref.shape[0] // sub
    mix_next = mixed(0)
    for i in range(nsub):
        mix = mix_next
        if i + 1 < nsub:
            mix_next = mixed(i + 1)
        rows = slice(i * sub, (i + 1) * sub)
        o_ref[rows, :] = _layer_norm(DEEPNORM_ALPHA * x_ref[rows, :] + mix, g_ref[...], b_ref[...])


def _out_proj(x, att, fmix, pol, yf, yb, z, wf, w, ng, g, bb):
    t = x.shape[0]
    tm = min(1024, t)
    row = lambda wd: pl.BlockSpec((tm, wd), lambda i: (i, 0))
    full = lambda a: pl.BlockSpec(a.shape, lambda i: (0,) * a.ndim)
    return pl.pallas_call(
        _out_proj_kernel,
        grid=(t // tm,),
        in_specs=[row(D_MODEL)] + [row(256)] * 6 + [full(wf), full(w), full(ng), full(g), full(bb)],
        out_specs=row(D_MODEL),
        out_shape=jax.ShapeDtypeStruct((t, D_MODEL), F32),
        compiler_params=_cparams(("parallel",)),
        name="out_proj_ln",
    )(x, att, fmix, pol, yf, yb, z, wf, w, ng, g, bb)


MOE_TILE = 1024
MOE_CHUNK = 128
MOE_PERM_ROWS = 256


MOE_LOGIT_ROWS = 24


def _router(x, w2, br_col):
    xh = x.astype(BF16)
    xl = (x - xh.astype(F32)).astype(BF16)
    l1 = _dot(xh, w2)
    logits = l1[:, 0:LANES] + l1[:, LANES:2 * LANES] + _dot(xl, w2[:, 0:LANES])
    lt = logits.T[0:MOE_LOGIT_ROWS, :] + br_col[0:MOE_LOGIT_ROWS, :]
    row = lax.broadcasted_iota(jnp.int32, (MOE_LOGIT_ROWS, 1), 0)
    ninf = -jnp.inf
    row8 = row[0:SUBLANES]
    gl = jnp.where(row8 < N_GROUPS, lt[0:SUBLANES], ninf)
    gmax = jnp.max(gl, axis=0, keepdims=True)
    g_p = 1.0 / jnp.sum(jnp.exp(gl - gmax), axis=0, keepdims=True)
    g_idx = jnp.min(jnp.where(gl == gmax, row8, LANES), axis=0, keepdims=True)
    e_row = row - N_GROUPS
    in_grp = (e_row >= 0) & (e_row < N_EXPERTS) & ((e_row >> 2) == g_idx)
    el = jnp.where(in_grp, lt, ninf)
    m1 = jnp.max(el, axis=0, keepdims=True)
    i1 = jnp.min(jnp.where(el == m1, row, LANES), axis=0, keepdims=True)
    el2 = jnp.where(row == i1, ninf, el)
    m2 = jnp.max(el2, axis=0, keepdims=True)
    i2 = jnp.min(jnp.where(el2 == m2, row, LANES), axis=0, keepdims=True)
    e2 = jnp.exp(m2 - m1)
    w1 = 1.0 / (1.0 + e2)
    gates = g_p * (jnp.where(row == i1, w1, 0.0) + jnp.where(row == i2, e2 * w1, 0.0))
    return gates, g_idx


def _moe_kernel(x_ref, wr_ref, br_ref, tri_ref, wg_ref, wu_ref, wd_ref, g_ref, b_ref, o_ref,
                xs_ref, gs_ref, pos_ref, acc_ref, off_ref):
    grp = pl.program_id(1)
    n = x_ref.shape[0]
    lane = lax.broadcasted_iota(jnp.int32, (1, LANES), 1)

    @pl.when(grp == 0)
    def _route_and_sort():
        x = x_ref[...]
        gates_t, g_idx = _router(x, wr_ref[...], br_ref[...])
        row8 = lax.broadcasted_iota(jnp.int32, (SUBLANES, 1), 0)
        onehot = row8 == g_idx
        onehot_b = jnp.concatenate([jnp.where(onehot, 1.0, 0.0), jnp.zeros((SUBLANES, n), F32)],
                                   axis=0).astype(BF16)
        csum = _dot(onehot_b, tri_ref[...])[0:SUBLANES, :]
        cnt = csum[:, n - 1:n].astype(jnp.int32)
        c0, c1, c2 = cnt[0, 0], cnt[1, 0], cnt[2, 0]
        off_ref[0] = 0
        off_ref[1] = c0
        off_ref[2] = c0 + c1
        off_ref[3] = c0 + c1 + c2
        off_ref[4] = n
        offv = jnp.where(row8 == 1, c0, jnp.where(row8 == 2, c0 + c1, jnp.where(row8 == 3, c0 + c1 + c2, 0)))
        pos = jnp.sum(jnp.where(onehot, csum + offv.astype(F32), 0.0), axis=0, keepdims=True) - 1.0
        pos_ref[...] = jnp.broadcast_to(pos, (LANES, n)).T
        pos_row = pos.astype(jnp.int32)
        xb = x.astype(BF16)
        gates = jnp.concatenate([gates_t, jnp.zeros((LANES - MOE_LOGIT_ROWS, n), F32)], axis=0).T
        g_hi = gates.astype(BF16)
        g2 = jnp.concatenate([g_hi, (gates - g_hi.astype(F32)).astype(BF16)], axis=1)
        for r in range(n // MOE_PERM_ROWS):
            rows = lax.broadcasted_iota(jnp.int32, (MOE_PERM_ROWS, n), 0) + r * MOE_PERM_ROWS
            perm = jnp.where(rows == pos_row, 1.0, 0.0).astype(BF16)
            sl = slice(r * MOE_PERM_ROWS, (r + 1) * MOE_PERM_ROWS)
            xs_ref[sl, :] = _dot(perm, xb).astype(BF16)
            gg = _dot(perm, g2)
            gs_ref[sl, :] = gg[:, 0:LANES] + gg[:, LANES:2 * LANES]
        acc_ref[...] = jnp.zeros_like(acc_ref)

    lo = off_ref[grp]
    hi = off_ref[grp + 1]

    def chunk(c, carry):
        r0 = pl.multiple_of(c * MOE_CHUNK, MOE_CHUNK)

        @pl.when((lo < r0 + MOE_CHUNK) & (hi > r0))
        def _():
            rs = pl.ds(r0, MOE_CHUNK)
            xs = xs_ref[rs, :]
            gsc = gs_ref[rs, :]
            parts = []
            for e in range(EXPERTS_PER_GROUP):
                hid = _silu(_dot(xs, wg_ref[e])) * _dot(xs, wu_ref[e])
                ge = jnp.sum(jnp.where(lane == N_GROUPS + EXPERTS_PER_GROUP * grp + e, gsc, 0.0),
                             axis=-1, keepdims=True)
                parts.append((hid * ge).astype(BF16))
            acc_ref[rs, :] += _dot(jnp.concatenate(parts, axis=1), wd_ref[...])

        return carry

    lax.fori_loop(0, n // MOE_CHUNK, chunk, 0)

    @pl.when(grp == N_GROUPS - 1)
    def _unsort_and_norm():
        xs_ref[...] = acc_ref[...].astype(BF16)
        cols = lax.broadcasted_iota(jnp.int32, (MOE_PERM_ROWS, n), 1)
        for r in range(n // MOE_PERM_ROWS):
            sl = slice(r * MOE_PERM_ROWS, (r + 1) * MOE_PERM_ROWS)
            perm_t = jnp.where(cols == pos_ref[sl, 0:1].astype(jnp.int32), 1.0, 0.0).astype(BF16)
            y = _dot(perm_t, xs_ref[...])
            o_ref[sl, :] = _layer_norm(DEEPNORM_ALPHA * x_ref[sl, :] + y, g_ref[...], b_ref[...])


def _moe(x, wr, br, wg, wu, wd, g, bb):
    t = x.shape[0]
    n = min(MOE_TILE, t)
    tri = jnp.asarray(np.triu(np.ones((n, n), np.float32)), dtype=BF16)
    full = lambda a: pl.BlockSpec(a.shape, lambda i, e: (0,) * a.ndim)
    wspec = pl.BlockSpec((EXPERTS_PER_GROUP, D_MODEL, EXPERT_FF), lambda i, e: (e, 0, 0))
    dspec = pl.BlockSpec((None, EXPERTS_PER_GROUP * EXPERT_FF, D_MODEL), lambda i, e: (e, 0, 0))
    return pl.pallas_call(
        _moe_kernel,
        grid=(t // n, N_GROUPS),
        in_specs=[pl.BlockSpec((n, D_MODEL), lambda i, e: (i, 0)), full(wr), full(br), full(tri),
                  wspec, wspec, dspec, full(g), full(bb)],
        out_specs=pl.BlockSpec((n, D_MODEL), lambda i, e: (i, 0)),
        out_shape=jax.ShapeDtypeStruct((t, D_MODEL), F32),
        scratch_shapes=[pltpu.VMEM((n, D_MODEL), BF16), pltpu.VMEM((n, LANES), F32),
                        pltpu.VMEM((n, LANES), F32), pltpu.VMEM((n, D_MODEL), F32),
                        pltpu.SMEM((8,), jnp.int32)],
        compiler_params=_cparams(("parallel", "arbitrary")),
        name="moe_ln",
    )(x, wr, br, tri, wg, wu, wd, g, bb)


def _rope_tables(seq):
    rows = seq // GRID_W
    row = jnp.repeat(jnp.arange(rows, dtype=F32), GRID_W)
    col = jnp.tile(jnp.arange(GRID_W, dtype=F32), rows)
    half = HEAD_DIM // 2
    freqs = 1.0 / (ROPE_THETA ** (jnp.arange(0, half, 2, dtype=F32) / half))
    ar, ac = row[:, None] * freqs, col[:, None] * freqs
    cosv = jnp.concatenate([jnp.cos(ar), jnp.cos(ar), jnp.cos(ac), jnp.cos(ac)], axis=-1)
    sinv = jnp.concatenate([-jnp.sin(ar), jnp.sin(ar), -jnp.sin(ac), jnp.sin(ac)], axis=-1)
    return jnp.tile(cosv, (1, 2)), jnp.tile(sinv, (1, 2))


def _position_dft(seq):
    n1 = seq // FFT_N2
    dft = lambda n: 2.0 * np.pi * ((np.arange(n)[:, None] * np.arange(n)[None, :]) % n) / n
    a1 = dft(n1)
    w1 = np.concatenate([np.cos(a1), np.sin(a1)], axis=0) * seq ** -0.5
    at = 2.0 * np.pi * (np.arange(FFT_N2)[:, None] * np.arange(n1)[None, :]) / seq
    lanes = lambda t: jnp.asarray(np.repeat(t[:, :, None], LANES, axis=2), dtype=F32)
    a2 = dft(FFT_N2)
    return (jnp.asarray(w1, dtype=BF16), lanes(np.cos(at)), lanes(np.sin(at)),
            jnp.asarray(np.concatenate([np.cos(a2), np.sin(a2)], axis=1), dtype=BF16))


def _channel_dft():
    n = np.arange(HEAD_DIM)
    ang = 2.0 * np.pi * ((n[:, None] * n[None, :]) % HEAD_DIM) / HEAD_DIM
    eye = np.eye(4)
    cb = np.kron(eye, np.cos(ang)) * HEAD_DIM ** -0.5
    sb = np.kron(eye, np.sin(ang)) * HEAD_DIM ** -0.5
    return jnp.asarray(np.concatenate([cb, sb], axis=1), dtype=BF16)


def _layer_params(l, w_in, q_norm_g, k_norm_g, w_fnet, w_pool, pool_scale, conv_w, conv_b, dt_bias,
                  a_log, d_skip, ssd_norm_g, w_out, ln1_g, ln1_b, w_group, b_group, w_router, b_router,
                  w_gate, w_up, w_down, ln2_g, ln2_b):
    wi = w_in[l]
    hperm = np.array([0, 2, 1, 3])
    wq = wi[:, 0:256].reshape(D_MODEL, 4, HEAD_DIM)[:, hperm].reshape(D_MODEL, 256)
    twice = lambda v: jnp.concatenate([v, v], axis=-1)
    w_proj = jnp.concatenate([wq, wi[:, 256:1792], jnp.pad(twice(wi[:, 1792:1800]), ((0, 0), (0, 112)))],
                             axis=1).astype(BF16)
    wo = w_out[l]
    wo_att = wo[0:256].reshape(4, HEAD_DIM, D_MODEL)[hperm].reshape(256, D_MODEL)
    w_o = jnp.concatenate([wo_att, wo[256:]], axis=0).astype(BF16)
    row = lambda v, n: jnp.pad(v.reshape(1, -1), ((0, 0), (0, n - v.size)))
    w_pool_bd = jnp.zeros((256, 256), F32)
    for gi in range(4):
        w_pool_bd = w_pool_bd.at[gi * 64:(gi + 1) * 64, gi * 64:(gi + 1) * 64].set(w_pool[l, gi])
    wr = jnp.pad(jnp.concatenate([w_group[l], w_router[l]], axis=1), ((0, 0), (0, LANES - 20)))
    wr_hi = wr.astype(BF16)
    att_bound = (1.01 * HEAD_DIM ** -0.5 * math.log2(math.e) * HEAD_DIM
                 * jnp.max(jnp.abs(q_norm_g[l])) * jnp.max(jnp.abs(k_norm_g[l]))).reshape(1).astype(F32)
    return dict(
        w_proj=w_proj, att_bound=att_bound,
        qg=jnp.tile(q_norm_g[l], 2).reshape(1, LANES), kg=jnp.tile(k_norm_g[l], 2).reshape(1, LANES),
        dtb=row(twice(dt_bias[l].reshape(-1)), LANES), w_fnet=w_fnet[l].astype(BF16),
        w_pool=w_pool_bd.astype(BF16), pool_scale=pool_scale[l].reshape(1, 256),
        conv_w=conv_w[l], conv_b=conv_b[l].reshape(1, 512),
        alog=row(twice(a_log[l].reshape(-1)), LANES), dskip=jnp.repeat(d_skip[l], 64).reshape(1, 256),
        ssd_ng=ssd_norm_g[l].reshape(1, 256), w_o=w_o,
        ln1_g=ln1_g[l].reshape(1, D_MODEL), ln1_b=ln1_b[l].reshape(1, D_MODEL),
        wr=jnp.concatenate([wr_hi, (wr - wr_hi.astype(F32)).astype(BF16)], axis=1),
        br=row(jnp.concatenate([b_group[l], b_router[l]]), LANES).reshape(LANES, 1),
        wg=w_gate[l].astype(BF16), wu=w_up[l].astype(BF16),
        wd=w_down[l].reshape(N_GROUPS, EXPERTS_PER_GROUP * EXPERT_FF, D_MODEL).astype(BF16),
        ln2_g=ln2_g[l].reshape(1, D_MODEL), ln2_b=ln2_b[l].reshape(1, D_MODEL),
    )


def _trunk(x3, params, tables):
    b, s, _ = x3.shape
    x = x3.reshape(b * s, D_MODEL)
    cosv, sinv, pos_dft, bd, dftc = tables
    for p in params:
        q, k, v, ab, up, z, xbc, dt = _in_proj(x, p["w_proj"], p["qg"], p["kg"], cosv, sinv, bd, dftc,
                                               p["dtb"], s)
        att = _attention(p["att_bound"], q, k, v, b, s)
        fmix = _fourier(pos_dft, ab, b, s)
        pol = _pool(up, p["w_pool"], p["pool_scale"], b, s)
        xc = _conv(xbc, p["conv_w"], p["conv_b"], b, s)
        yf, yb = _ssd(xc, dt, p["alog"], p["dskip"], b, s)
        x = _out_proj(x, att, fmix, pol, yf, yb, z, p["w_fnet"], p["w_o"], p["ssd_ng"], p["ln1_g"],
                      p["ln1_b"])
        x = _moe(x, p["wr"], p["br"], p["wg"], p["wu"], p["wd"], p["ln2_g"], p["ln2_b"])
    return x.reshape(b, s, D_MODEL)


def _tables(seq):
    cosv, sinv = _rope_tables(seq)
    lane = np.arange(LANES)
    bd = jnp.asarray(np.tile((lane[:, None] // HEAD_DIM) == (lane[None, :] // HEAD_DIM), (2, 1)), dtype=BF16)
    return cosv, sinv, _position_dft(seq), bd, _channel_dft()


def kernel(x_prompt, x_sample, w_in, q_norm_g, k_norm_g, w_fnet, w_pool, pool_scale, conv_w, conv_b, dt_bias, a_log, d_skip, ssd_norm_g, w_out, ln1_g, ln1_b, w_group, b_group, w_router, b_router, w_gate, w_up, w_down, ln2_g, ln2_b):
    weights = (w_in, q_norm_g, k_norm_g, w_fnet, w_pool, pool_scale, conv_w, conv_b, dt_bias, a_log,
               d_skip, ssd_norm_g, w_out, ln1_g, ln1_b, w_group, b_group, w_router, b_router,
               w_gate, w_up, w_down, ln2_g, ln2_b)
    params = [_layer_params(l, *weights) for l in range(w_in.shape[0])]
    y_prompt = _trunk(x_prompt, params, _tables(x_prompt.shape[1]))
    y_sample = _trunk(x_sample, params, _tables(x_sample.shape[1]))
    return (y_prompt, y_sample)
```

```python
import functools
import math

import numpy as np
import jax
import jax.numpy as jnp
from jax import lax
from jax.experimental import pallas as pl
from jax.experimental.pallas import tpu as pltpu

F32 = jnp.float32
BF16 = jnp.bfloat16

D_MODEL = 1024
DEPTH = 2
GRID_W = 64
HEAD_DIM = 64
ROPE_THETA = 10000.0
ATT_Q_HEADS = 4
ATT_KV_HEADS = 2
POOL_WINDOWS = (2, 4, 8, 16)
SSD_HEADS = 4
SSD_CONV = 4
SSD_CHUNK = 128
N_GROUPS = 4
EXPERTS_PER_GROUP = 4
N_EXPERTS = 16
EXPERT_FF = 256
DEEPNORM_ALPHA = (2 * DEPTH) ** 0.25
LN_EPS = 1e-5
RMS_EPS = 1e-6

LANES = 128
SUBLANES = 8
HALO = SUBLANES
VMEM_LIMIT = 56 * 1024 * 1024

C_Q, C_K, C_V, C_UF, C_UP, C_Z, C_XBC, C_DT, N_PROJ = 0, 256, 384, 512, 768, 1024, 1280, 1792, 1920
HI = lax.Precision.HIGHEST


def _cparams(sem):
    return pltpu.CompilerParams(dimension_semantics=sem, vmem_limit_bytes=VMEM_LIMIT)


def _dot(a, b):
    return jnp.dot(a, b, preferred_element_type=F32)


def _dot_nt(a, b):
    return lax.dot_general(a, b, (((1,), (1,)), ((), ())), preferred_element_type=F32)


def _silu(x):
    return x * (1.0 / (1.0 + jnp.exp(-x)))


def _layer_norm(x, g, b):
    mu = jnp.mean(x, axis=-1, keepdims=True)
    xc = x - mu
    var = jnp.mean(xc * xc, axis=-1, keepdims=True)
    return xc * lax.rsqrt(var + LN_EPS) * g + b


def _head_sumsq(x, ones_bd):
    sq = x * x
    hi = sq.astype(BF16)
    lo = (sq - hi.astype(F32)).astype(BF16)
    return _dot(jnp.concatenate([hi, lo], axis=1), ones_bd)


def _rope(x, cosv, sinv, first_half):
    w = x.shape[-1]
    partner = jnp.where(first_half, pltpu.roll(x, w - 16, axis=1), pltpu.roll(x, 16, axis=1))
    return x * cosv + partner * sinv


IN_PROJ_SUB = 512


def _in_proj_kernel(x_ref, w_ref, qg_ref, kg_ref, cos_ref, sin_ref, bd_ref, dftc_ref, dtb_ref,
                    q_ref, k_ref, v_ref, ab_ref, up_ref, z_ref, xbc_ref, dt_ref):
    bd = bd_ref[...]
    lane = lax.broadcasted_iota(jnp.int32, (1, LANES), 1)
    first_half = (lane & 31) < 16
    sub = min(IN_PROJ_SUB, x_ref.shape[0])
    nsub = x_ref.shape[0] // sub

    def project(i):
        return _dot(x_ref[i * sub:(i + 1) * sub, :].astype(BF16), w_ref[...])

    h_next = project(0)
    for i in range(nsub):
        h = h_next
        if i + 1 < nsub:
            h_next = project(i + 1)
        rows = slice(i * sub, (i + 1) * sub)
        cosv = cos_ref[rows, :]
        sinv = sin_ref[rows, :]
        for c in range(2):
            qc = h[:, C_Q + c * LANES:C_Q + (c + 1) * LANES]
            ss = _head_sumsq(qc, bd)
            qn = qc * lax.rsqrt(ss * (1.0 / HEAD_DIM) + RMS_EPS) * qg_ref[...]
            qr = _rope(qn, cosv, sinv, first_half) * (HEAD_DIM ** -0.5 * math.log2(math.e))
            q_ref[rows, c * LANES:(c + 1) * LANES] = qr.astype(BF16)
        kc = h[:, C_K:C_K + LANES]
        ss = _head_sumsq(kc, bd)
        kn = kc * lax.rsqrt(ss * (1.0 / HEAD_DIM) + RMS_EPS) * kg_ref[...]
        k_ref[rows, :] = _rope(kn, cosv, sinv, first_half).astype(BF16)
        v_ref[rows, :] = h[:, C_V:C_V + LANES].astype(BF16)
        uf = h[:, C_UF:C_UF + 256].astype(BF16)
        ab_ref[rows, :] = _dot(uf, dftc_ref[...])
        up_ref[rows, :] = h[:, C_UP:C_UP + 256]
        z_ref[rows, :] = h[:, C_Z:C_Z + 256]
        xbc_ref[rows, :] = h[:, C_XBC:C_XBC + 512]
        dr = h[:, C_DT:C_DT + LANES] + dtb_ref[...]
        dt_ref[rows, :] = jnp.maximum(dr, 0.0) + jnp.log(1.0 + jnp.exp(-jnp.abs(dr)))


def _in_proj(x, w, qg, kg, cosv, sinv, bd, dftc, dtb, seq):
    t = x.shape[0]
    tm = min(1024, seq)
    nseq = seq // tm
    full = lambda a: pl.BlockSpec(a.shape, lambda i: (0,) * a.ndim)
    row = lambda wdt: pl.BlockSpec((tm, wdt), lambda i: (i, 0))
    pos = pl.BlockSpec((tm, LANES), lambda i: (i % nseq, 0))
    outs = [(256, BF16), (128, BF16), (128, BF16), (512, F32), (256, F32), (256, F32), (512, F32),
            (128, F32)]
    return pl.pallas_call(
        _in_proj_kernel,
        grid=(t // tm,),
        in_specs=[row(D_MODEL), full(w), full(qg), full(kg), pos, pos, full(bd), full(dftc), full(dtb)],
        out_specs=[row(wd) for wd, _ in outs],
        out_shape=[jax.ShapeDtypeStruct((t, wd), dt) for wd, dt in outs],
        compiler_params=_cparams(("parallel",)),
        name="in_proj",
    )(x, w, qg, kg, cosv, sinv, bd, dftc, dtb)


ATT_KEY_CHUNK = 256
ATT_FIXED_SHIFT_MAX = 60.0


def _attn_kernel(bound_ref, q_ref, k_ref, v_ref, o_ref, vt_ref):
    nch, _, vrows, tk = vt_ref.shape
    tq = q_ref.shape[0]
    lane = lax.broadcasted_iota(jnp.int32, (1, LANES), 1)
    head_lanes = [(lane >> 6) == j for j in range(ATT_KV_HEADS)]

    @pl.when(pl.program_id(1) == 0)
    def _():
        for c in range(nch):
            vt = v_ref[c * tk:(c + 1) * tk, :].astype(F32).T.astype(BF16)
            for j in range(ATT_KV_HEADS):
                vt_ref[c, j, 0:HEAD_DIM, :] = vt[j * HEAD_DIM:(j + 1) * HEAD_DIM, :]
                vt_ref[c, j, HEAD_DIM:vrows, :] = jnp.ones((vrows - HEAD_DIM, tk), BF16)

    qms = []
    for c in range(2):
        qc = q_ref[:, c * LANES:(c + 1) * LANES]
        for j in range(ATT_KV_HEADS):
            qms.append(jnp.where(head_lanes[j], qc, jnp.zeros_like(qc)))

    bound = bound_ref[0]
    fixed_shift_ok = bound <= ATT_FIXED_SHIFT_MAX

    def scores(c):
        kc = k_ref[c * tk:(c + 1) * tk, :]
        return [_dot_nt(kc, qm) for qm in qms]

    def run(shift):
        ms = [jnp.full((1, tq), -jnp.inf, F32)] * ATT_Q_HEADS
        accs = [jnp.zeros((vrows, tq), F32)] * ATT_Q_HEADS
        st_next = scores(0)
        for c in range(nch):
            sts = st_next
            if c + 1 < nch:
                st_next = scores(c + 1)
            for h in range(ATT_Q_HEADS):
                if shift is None:
                    m_new = jnp.maximum(ms[h], jnp.max(sts[h], axis=0, keepdims=True))
                    accs[h] = jnp.exp2(ms[h] - m_new) * accs[h]
                    ms[h] = m_new
                    p = jnp.exp2(sts[h] - m_new).astype(BF16)
                else:
                    p = jnp.exp2(sts[h] - shift).astype(BF16)
                accs[h] = accs[h] + _dot(vt_ref[c, h % ATT_KV_HEADS], p)
        for c in range(2):
            ot = jnp.concatenate([accs[2 * c + j][0:HEAD_DIM] / accs[2 * c + j][HEAD_DIM:HEAD_DIM + 1]
                                  for j in range(ATT_KV_HEADS)], axis=0)
            o_ref[:, c * LANES:(c + 1) * LANES] = ot.T.astype(BF16)

    @pl.when(fixed_shift_ok)
    def _():
        run(bound)

    @pl.when(jnp.logical_not(fixed_shift_ok))
    def _():
        run(None)


def _attention(bound, q, k, v, b, s):
    tq = min(512, s)
    tk = min(ATT_KEY_CHUNK, s)
    q3, k3, v3 = q.reshape(b, s, 256), k.reshape(b, s, LANES), v.reshape(b, s, LANES)
    out = pl.pallas_call(
        _attn_kernel,
        grid=(b, s // tq),
        in_specs=[pl.BlockSpec(memory_space=pltpu.SMEM),
                  pl.BlockSpec((None, tq, 256), lambda bi, i: (bi, i, 0)),
                  pl.BlockSpec((None, s, LANES), lambda bi, i: (bi, 0, 0)),
                  pl.BlockSpec((None, s, LANES), lambda bi, i: (bi, 0, 0))],
        out_specs=pl.BlockSpec((None, tq, 256), lambda bi, i: (bi, i, 0)),
        out_shape=jax.ShapeDtypeStruct((b, s, 256), BF16),
        scratch_shapes=[pltpu.VMEM((s // tk, ATT_KV_HEADS, HEAD_DIM + 16, tk), BF16)],
        compiler_params=_cparams(("parallel", "arbitrary")),
        name="attention",
    )(bound, q3, k3, v3)
    return out.reshape(b * s, 256)


FFT_N2 = 64


def _fourier_kernel(a_ref, b_ref, w1_ref, tc_ref, ts_ref, cs2_ref, o_ref, yr_ref, yi_ref):
    n1 = a_ref.shape[0]
    w1 = w1_ref[...]
    for j in range(FFT_N2 // SUBLANES):
        rows = slice(j * SUBLANES, (j + 1) * SUBLANES)
        at = jnp.swapaxes(a_ref[:, rows, :], 0, 1)
        bt = jnp.swapaxes(b_ref[:, rows, :], 0, 1)
        yrs, yis = [], []
        for i in range(SUBLANES):
            n2 = j * SUBLANES + i
            r = _dot(w1, jnp.concatenate([at[i], bt[i]], axis=1).astype(BF16))
            yr = r[0:n1, 0:LANES] - r[n1:2 * n1, LANES:2 * LANES]
            ym = r[0:n1, LANES:2 * LANES] + r[n1:2 * n1, 0:LANES]
            tc = tc_ref[n2]
            ts = ts_ref[n2]
            yrs.append(yr * tc - ym * ts)
            yis.append(-(ym * tc) - yr * ts)
        yr_ref[:, rows, :] = jnp.swapaxes(jnp.stack(yrs, axis=0), 0, 1)
        yi_ref[:, rows, :] = jnp.swapaxes(jnp.stack(yis, axis=0), 0, 1)
    cs2 = cs2_ref[...]
    for k0 in range(0, n1, SUBLANES):
        ks = range(k0, k0 + SUBLANES)
        yr = jnp.concatenate([yr_ref[k1] for k1 in ks], axis=1)
        yi = jnp.concatenate([yi_ref[k1] for k1 in ks], axis=1)
        z = _dot(cs2, jnp.concatenate([yr, yi], axis=0).astype(BF16))
        zt = jnp.stack([z[:, g * LANES:(g + 1) * LANES] for g in range(SUBLANES)], axis=0)
        o_ref[:, k0:k0 + SUBLANES, :] = jnp.swapaxes(zt, 0, 1)


def _fourier(tabs, ab, b, s):
    w1, tc, ts, cs2 = tabs
    n1 = s // FFT_N2
    full = lambda a: pl.BlockSpec(a.shape, lambda bi, h: (0,) * a.ndim)
    ab4 = ab.reshape(b, n1, FFT_N2, 512)
    out = pl.pallas_call(
        _fourier_kernel,
        grid=(b, 2),
        in_specs=[pl.BlockSpec((None, n1, FFT_N2, LANES), lambda bi, h: (bi, 0, 0, h)),
                  pl.BlockSpec((None, n1, FFT_N2, LANES), lambda bi, h: (bi, 0, 0, 2 + h)),
                  full(w1), full(tc), full(ts), full(cs2)],
        out_specs=pl.BlockSpec((None, FFT_N2, n1, LANES), lambda bi, h: (bi, 0, 0, h)),
        out_shape=jax.ShapeDtypeStruct((b, FFT_N2, n1, 256), F32),
        scratch_shapes=[pltpu.VMEM((n1, FFT_N2, LANES), F32)] * 2,
        compiler_params=_cparams(("parallel", "parallel")),
        name="fourier",
    )(ab4, ab4, w1, tc, ts, cs2)
    return out.reshape(b * s, 256)


def _halo_specs(tt, width, s):
    nb = tt // HALO
    last = s // HALO - 1
    main = pl.BlockSpec((None, tt, width), lambda bi, i: (bi, i, 0))
    prev = pl.BlockSpec((None, HALO, width), lambda bi, i: (bi, jnp.maximum(i * nb - 1, 0), 0))
    nxt = pl.BlockSpec((None, HALO, width), lambda bi, i: (bi, jnp.minimum((i + 1) * nb, last), 0))
    return [main, prev, nxt]


def _with_halo(main_ref, prev_ref, next_ref):
    i = pl.program_id(1)
    n = pl.num_programs(1)
    prev = jnp.where(i > 0, prev_ref[...], 0.0)
    nxt = jnp.where(i < n - 1, next_ref[...], 0.0)
    return jnp.concatenate([prev, main_ref[...], nxt], axis=0)


def _shifted(ext, d, tt):
    n = ext.shape[0]
    r = ext if d == 0 else pltpu.roll(ext, (-d) % n, axis=0)
    return r[HALO:HALO + tt]


def _pool_kernel(u_ref, up_ref, un_ref, w_ref, sc_ref, o_ref, *, seq):
    tt = u_ref.shape[0]
    ext = _with_halo(u_ref, up_ref, un_ref)
    n = ext.shape[0]
    sh = lambda a, d: a if d == 0 else pltpu.roll(a, (-d) % n, axis=0)
    t = pl.program_id(1) * tt + lax.broadcasted_iota(jnp.int32, (tt, 1), 0)

    def inv_count(w):
        lo = jnp.clip(t - w // 2, 0, seq - 1)
        hi = jnp.clip(t + (w - w // 2) - 1, 0, seq - 1)
        return 1.0 / (hi - lo + 1).astype(F32)

    left = (lax.broadcasted_iota(jnp.int32, (1, LANES), 1) >> 6) == 0
    cols = []
    for c in range(2):
        e = ext[:, c * LANES:(c + 1) * LANES]
        u = u_ref[:, c * LANES:(c + 1) * LANES]
        p2 = sh(e, -1) + e
        p4 = sh(p2, -1) + sh(p2, 1)
        if c == 0:
            small, big, ws, wb = p2, p4, POOL_WINDOWS[0], POOL_WINDOWS[1]
        else:
            p8 = sh(p4, -2) + sh(p4, 2)
            p16 = sh(p8, -4) + sh(p8, 4)
            small, big, ws, wb = p8, p16, POOL_WINDOWS[2], POOL_WINDOWS[3]
        mean = jnp.where(left, small[HALO:HALO + tt] * inv_count(ws), big[HALO:HALO + tt] * inv_count(wb))
        cols.append((mean - u).astype(BF16))
    o_ref[...] = (_dot(jnp.concatenate(cols, axis=1), w_ref[...]) * sc_ref[...]).astype(BF16)


def _pool(u, w_bd, scale, b, s):
    tt = min(1024, s)
    u3 = u.reshape(b, s, 256)
    full = lambda a: pl.BlockSpec(a.shape, lambda bi, i: (0,) * a.ndim)
    out = pl.pallas_call(
        functools.partial(_pool_kernel, seq=s),
        grid=(b, s // tt),
        in_specs=_halo_specs(tt, 256, s) + [full(w_bd), full(scale)],
        out_specs=pl.BlockSpec((None, tt, 256), lambda bi, i: (bi, i, 0)),
        out_shape=jax.ShapeDtypeStruct((b, s, 256), BF16),
        compiler_params=_cparams(("parallel", "parallel")),
        name="pool",
    )(u3, u3, u3, w_bd, scale)
    return out.reshape(b * s, 256)


def _conv_kernel(x_ref, xp_ref, xn_ref, w_ref, b_ref, o_ref):
    tt = x_ref.shape[0]
    ext = _with_halo(x_ref, xp_ref, xn_ref)
    acc = b_ref[...] + _shifted(ext, -2, tt) * w_ref[0:1, :]
    for kk in range(1, SSD_CONV):
        acc = acc + _shifted(ext, kk - 2, tt) * w_ref[kk:kk + 1, :]
    o_ref[...] = _silu(acc)


def _conv(xbc, conv_w, conv_b, b, s):
    tt = min(1024, s)
    x3 = xbc.reshape(b, s, 512)
    full = lambda a: pl.BlockSpec(a.shape, lambda bi, i: (0,) * a.ndim)
    return pl.pallas_call(
        _conv_kernel,
        grid=(b, s // tt),
        in_specs=_halo_specs(tt, 512, s) + [full(conv_w), full(conv_b)],
        out_specs=pl.BlockSpec((None, tt, 512), lambda bi, i: (bi, i, 0)),
        out_shape=jax.ShapeDtypeStruct((b, s, 512), F32),
        compiler_params=_cparams(("parallel", "parallel")),
        name="ssd_conv",
    )(x3, x3, x3, conv_w, conv_b)


def _ssd_kernel(xf_ref, dtf_ref, xb_ref, dtb_ref, alog_ref, dskip_ref, yf_ref, yb_ref, carry_ref):
    @pl.when(pl.program_id(1) == 0)
    def _():
        carry_ref[...] = jnp.zeros_like(carry_ref)

    L = SSD_CHUNK
    nch = xf_ref.shape[0] // L
    a_row = -jnp.exp(alog_ref[...])
    dskip = dskip_ref[...]
    r = lax.broadcasted_iota(jnp.int32, (L, L), 0)
    c = lax.broadcasted_iota(jnp.int32, (L, L), 1)
    keeps = (c <= r, c >= r)
    tris = [k.astype(F32) for k in keeps]
    grp = lax.broadcasted_iota(jnp.int32, (1, LANES), 1) >> 6
    head = lax.broadcasted_iota(jnp.int32, (1, 256), 1) >> 6
    row_grp = lax.broadcasted_iota(jnp.int32, (LANES, 1), 0) >> 6

    units = []
    for ci in range(nch):
        units.append(dict(d=0, rows=slice(ci * L, (ci + 1) * L)))
        units.append(dict(d=1, rows=slice((nch - 1 - ci) * L, (nch - ci) * L)))

    for u in units:
        d = u["d"]
        u["off"] = d * SSD_HEADS
        u["xc"] = (xf_ref, xb_ref)[d][u["rows"], :]
        u["dtc"] = (dtf_ref, dtb_ref)[d][u["rows"], :]
        u["cs"] = jnp.dot(tris[d], u["dtc"] * a_row, precision=HI, preferred_element_type=F32)

    prow = lax.broadcasted_iota(jnp.int32, (LANES, 1), 0)
    spread = [jnp.where((prow < 16) & ((prow & 7) == d * SSD_HEADS + head), 1.0, 0.0).astype(BF16)
              for d in range(2)]
    low8 = lax.broadcasted_iota(jnp.int32, (1, LANES), 1) < 8

    for u in units:
        cs, xc = u["cs"], u["xc"]
        u["cs_t"] = cs.T
        tot = cs[0:1, :] if u["d"] else cs[L - 1:L, :]
        fac = jnp.concatenate([u["dtc"], jnp.exp(tot - cs), jnp.exp(cs)], axis=0)
        head_part = fac.astype(BF16).astype(F32)
        fac_x = _dot(jnp.where(low8, head_part, fac - head_part).astype(BF16), spread[u["d"]])
        u["ds_x"], u["ecs_x"] = fac_x[L:2 * L], fac_x[2 * L:3 * L]
        u["etot_x"] = u["ecs_x"][0:1, :] if u["d"] else u["ecs_x"][L - 1:L, :]
        u["xd"] = xc[:, 0:256] * fac_x[0:L]
        u["xd_b"] = u["xd"].astype(BF16)
        u["b_t"] = xc[:, 256:384].T.astype(BF16)
        cmat = xc[:, 384:512]
        u["g"] = [_dot(jnp.where(grp == g, cmat, 0.0).astype(BF16), u["b_t"]) for g in range(2)]

    for u in units:
        cs, cs_t, off, keep = u["cs"], u["cs_t"], u["off"], keeps[u["d"]]
        y = None
        for h in range(SSD_HEADS):
            diff = cs[:, off + h:off + h + 1] - cs_t[off + h:off + h + 1, :]
            decay = jnp.where(keep, jnp.exp(jnp.where(keep, diff, 0.0)), 0.0)
            sc = (u["g"][h // 2] * decay).astype(BF16)
            yd = _dot(sc, u["xd_b"])
            y = yd if y is None else jnp.where(head == h, yd, y)
        u["y"] = y

    for u in units:
        st = _dot(u["b_t"], (u["xd"] * u["ds_x"]).astype(BF16))
        u["st"] = jnp.where(row_grp == (head >> 1), st, 0.0)

    carries = [carry_ref[0], carry_ref[1]]
    for u in units:
        d, xc = u["d"], u["xc"]
        y_off = _dot(xc[:, 384:512].astype(BF16), carries[d].astype(BF16))
        y = u["y"] + y_off * u["ecs_x"]
        carries[d] = carries[d] * u["etot_x"] + u["st"]
        if d == 0:
            yf_ref[u["rows"], :] = y + xc[:, 0:256] * dskip
        else:
            yb_ref[u["rows"], :] = y
    carry_ref[0] = carries[0]
    carry_ref[1] = carries[1]


def _ssd(xc, dt, alog_row, dskip_row, b, s):
    tt = min(1024, s)
    nt = s // tt
    dt3 = dt.reshape(b, s, LANES)
    fwd = lambda w: pl.BlockSpec((None, tt, w), lambda bi, i: (bi, i, 0))
    bwd = lambda w: pl.BlockSpec((None, tt, w), lambda bi, i: (bi, nt - 1 - i, 0))
    full = lambda a: pl.BlockSpec(a.shape, lambda bi, i: (0,) * a.ndim)
    yf, yb = pl.pallas_call(
        _ssd_kernel,
        grid=(b, nt),
        in_specs=[fwd(512), fwd(LANES), bwd(512), bwd(LANES), full(alog_row), full(dskip_row)],
        out_specs=[fwd(256), bwd(256)],
        out_shape=[jax.ShapeDtypeStruct((b, s, 256), F32)] * 2,
        scratch_shapes=[pltpu.VMEM((2, LANES, 256), F32)],
        compiler_params=_cparams(("parallel", "arbitrary")),
        name="ssd_scan",
    )(xc, dt3, xc, dt3, alog_row, dskip_row)
    return yf.reshape(b * s, 256), yb.reshape(b * s, 256)


OUT_PROJ_SUB = 256


def _out_proj_kernel(x_ref, att_ref, fmix_ref, pol_ref, yf_ref, yb_ref, z_ref, wf_ref, w_ref, ng_ref,
                     g_ref, b_ref, o_ref):
    ng = ng_ref[...]
    sub = min(OUT_PROJ_SUB, x_ref.shape[0])

    def mixed(i):
        rows = slice(i * sub, (i + 1) * sub)
        fno = _dot(fmix_ref[rows, :].astype(BF16), wf_ref[...]).astype(BF16)
        y = (yf_ref[rows, :] + yb_ref[rows, :]) * _silu(z_ref[rows, :])
        parts = []
        for gi in range(2):
            yg = y[:, gi * LANES:(gi + 1) * LANES]
            ms = jnp.mean(yg * yg, axis=-1, keepdims=True)
            parts.append((yg * lax.rsqrt(ms + RMS_EPS) * ng[:, gi * LANES:(gi + 1) * LANES]).astype(BF16))
        return _dot(jnp.concatenate([att_ref[rows, :], fno, pol_ref[rows, :]] + parts, axis=1), w_ref[...])

    nsub = x_ref.shape[0] // sub
    mix_next = mixed(0)
    for i in range(nsub):
        mix = mix_next
        if i + 1 < nsub:
            mix_next = mixed(i + 1)
        rows = slice(i * sub, (i + 1) * sub)
        o_ref[rows, :] = _layer_norm(DEEPNORM_ALPHA * x_ref[rows, :] + mix, g_ref[...], b_ref[...])


def _out_proj(x, att, fmix, pol, yf, yb, z, wf, w, ng, g, bb):
    t = x.shape[0]
    tm = min(1024, t)
    row = lambda wd: pl.BlockSpec((tm, wd), lambda i: (i, 0))
    full = lambda a: pl.BlockSpec(a.shape, lambda i: (0,) * a.ndim)
    return pl.pallas_call(
        _out_proj_kernel,
        grid=(t // tm,),
        in_specs=[row(D_MODEL)] + [row(256)] * 6 + [full(wf), full(w), full(ng), full(g), full(bb)],
        out_specs=row(D_MODEL),
        out_shape=jax.ShapeDtypeStruct((t, D_MODEL), F32),
        compiler_params=_cparams(("parallel",)),
        name="out_proj_ln",
    )(x, att, fmix, pol, yf, yb, z, wf, w, ng, g, bb)


MOE_TILE = 1024
MOE_CHUNK = 128
MOE_PERM_ROWS = 256


MOE_LOGIT_ROWS = 24


def _router(x, w2, br_col):
    xh = x.astype(BF16)
    xl = (x - xh.astype(F32)).astype(BF16)
    l1 = _dot(xh, w2)
    logits = l1[:, 0:LANES] + l1[:, LANES:2 * LANES] + _dot(xl, w2[:, 0:LANES])
    lt = logits.T[0:MOE_LOGIT_ROWS, :] + br_col[0:MOE_LOGIT_ROWS, :]
    row = lax.broadcasted_iota(jnp.int32, (MOE_LOGIT_ROWS, 1), 0)
    ninf = -jnp.inf
    row8 = row[0:SUBLANES]
    gl = jnp.where(row8 < N_GROUPS, lt[0:SUBLANES], ninf)
    gmax = jnp.max(gl, axis=0, keepdims=True)
    g_p = 1.0 / jnp.sum(jnp.exp(gl - gmax), axis=0, keepdims=True)
    g_idx = jnp.min(jnp.where(gl == gmax, row8, LANES), axis=0, keepdims=True)
    e_row = row - N_GROUPS
    in_grp = (e_row >= 0) & (e_row < N_EXPERTS) & ((e_row >> 2) == g_idx)
    el = jnp.where(in_grp, lt, ninf)
    m1 = jnp.max(el, axis=0, keepdims=True)
    i1 = jnp.min(jnp.where(el == m1, row, LANES), axis=0, keepdims=True)
    el2 = jnp.where(row == i1, ninf, el)
    m2 = jnp.max(el2, axis=0, keepdims=True)
    i2 = jnp.min(jnp.where(el2 == m2, row, LANES), axis=0, keepdims=True)
    e2 = jnp.exp(m2 - m1)
    w1 = 1.0 / (1.0 + e2)
    gates = g_p * (jnp.where(row == i1, w1, 0.0) + jnp.where(row == i2, e2 * w1, 0.0))
    return gates, g_idx


def _moe_kernel(x_ref, wr_ref, br_ref, tri_ref, wg_ref, wu_ref, wd_ref, g_ref, b_ref, o_ref,
                xs_ref, gs_ref, pos_ref, acc_ref, off_ref):
    grp = pl.program_id(1)
    n = x_ref.shape[0]
    lane = lax.broadcasted_iota(jnp.int32, (1, LANES), 1)

    @pl.when(grp == 0)
    def _route_and_sort():
        x = x_ref[...]
        gates_t, g_idx = _router(x, wr_ref[...], br_ref[...])
        row8 = lax.broadcasted_iota(jnp.int32, (SUBLANES, 1), 0)
        onehot = row8 == g_idx
        onehot_b = jnp.concatenate([jnp.where(onehot, 1.0, 0.0), jnp.zeros((SUBLANES, n), F32)],
                                   axis=0).astype(BF16)
        csum = _dot(onehot_b, tri_ref[...])[0:SUBLANES, :]
        cnt = csum[:, n - 1:n].astype(jnp.int32)
        c0, c1, c2 = cnt[0, 0], cnt[1, 0], cnt[2, 0]
        off_ref[0] = 0
        off_ref[1] = c0
        off_ref[2] = c0 + c1
        off_ref[3] = c0 + c1 + c2
        off_ref[4] = n
        offv = jnp.where(row8 == 1, c0, jnp.where(row8 == 2, c0 + c1, jnp.where(row8 == 3, c0 + c1 + c2, 0)))
        pos = jnp.sum(jnp.where(onehot, csum + offv.astype(F32), 0.0), axis=0, keepdims=True) - 1.0
        pos_ref[...] = jnp.broadcast_to(pos, (LANES, n)).T
        pos_row = pos.astype(jnp.int32)
        xb = x.astype(BF16)
        gates = jnp.concatenate([gates_t, jnp.zeros((LANES - MOE_LOGIT_ROWS, n), F32)], axis=0).T
        g_hi = gates.astype(BF16)
        g2 = jnp.concatenate([g_hi, (gates - g_hi.astype(F32)).astype(BF16)], axis=1)
        for r in range(n // MOE_PERM_ROWS):
            rows = lax.broadcasted_iota(jnp.int32, (MOE_PERM_ROWS, n), 0) + r * MOE_PERM_ROWS
            perm = jnp.where(rows == pos_row, 1.0, 0.0).astype(BF16)
            sl = slice(r * MOE_PERM_ROWS, (r + 1) * MOE_PERM_ROWS)
            xs_ref[sl, :] = _dot(perm, xb).astype(BF16)
            gg = _dot(perm, g2)
            gs_ref[sl, :] = gg[:, 0:LANES] + gg[:, LANES:2 * LANES]
        acc_ref[...] = jnp.zeros_like(acc_ref)

    lo = off_ref[grp]
    hi = off_ref[grp + 1]

    def chunk(c, carry):
        r0 = pl.multiple_of(c * MOE_CHUNK, MOE_CHUNK)

        @pl.when((lo < r0 + MOE_CHUNK) & (hi > r0))
        def _():
            rs = pl.ds(r0, MOE_CHUNK)
            xs = xs_ref[rs, :]
            gsc = gs_ref[rs, :]
            parts = []
            for e in range(EXPERTS_PER_GROUP):
                hid = _silu(_dot(xs, wg_ref[e])) * _dot(xs, wu_ref[e])
                ge = jnp.sum(jnp.where(lane == N_GROUPS + EXPERTS_PER_GROUP * grp + e, gsc, 0.0),
                             axis=-1, keepdims=True)
                parts.append((hid * ge).astype(BF16))
            acc_ref[rs, :] += _dot(jnp.concatenate(parts, axis=1), wd_ref[...])

        return carry

    lax.fori_loop(0, n // MOE_CHUNK, chunk, 0)

    @pl.when(grp == N_GROUPS - 1)
    def _unsort_and_norm():
        xs_ref[...] = acc_ref[...].astype(BF16)
        cols = lax.broadcasted_iota(jnp.int32, (MOE_PERM_ROWS, n), 1)
        for r in range(n // MOE_PERM_ROWS):
            sl = slice(r * MOE_PERM_ROWS, (r + 1) * MOE_PERM_ROWS)
            perm_t = jnp.where(cols == pos_ref[sl, 0:1].astype(jnp.int32), 1.0, 0.0).astype(BF16)
            y = _dot(perm_t, xs_ref[...])
            o_ref[sl, :] = _layer_norm(DEEPNORM_ALPHA * x_ref[sl, :] + y, g_ref[...], b_ref[...])


def _moe(x, wr, br, wg, wu, wd, g, bb):
    t = x.shape[0]
    n = min(MOE_TILE, t)
    tri = jnp.asarray(np.triu(np.ones((n, n), np.float32)), dtype=BF16)
    full = lambda a: pl.BlockSpec(a.shape, lambda i, e: (0,) * a.ndim)
    wspec = pl.BlockSpec((EXPERTS_PER_GROUP, D_MODEL, EXPERT_FF), lambda i, e: (e, 0, 0))
    dspec = pl.BlockSpec((None, EXPERTS_PER_GROUP * EXPERT_FF, D_MODEL), lambda i, e: (e, 0, 0))
    return pl.pallas_call(
        _moe_kernel,
        grid=(t // n, N_GROUPS),
        in_specs=[pl.BlockSpec((n, D_MODEL), lambda i, e: (i, 0)), full(wr), full(br), full(tri),
                  wspec, wspec, dspec, full(g), full(bb)],
        out_specs=pl.BlockSpec((n, D_MODEL), lambda i, e: (i, 0)),
        out_shape=jax.ShapeDtypeStruct((t, D_MODEL), F32),
        scratch_shapes=[pltpu.VMEM((n, D_MODEL), BF16), pltpu.VMEM((n, LANES), F32),
                        pltpu.VMEM((n, LANES), F32), pltpu.VMEM((n, D_MODEL), F32),
                        pltpu.SMEM((8,), jnp.int32)],
        compiler_params=_cparams(("parallel", "arbitrary")),
        name="moe_ln",
    )(x, wr, br, tri, wg, wu, wd, g, bb)


def _rope_tables(seq):
    rows = seq // GRID_W
    row = jnp.repeat(jnp.arange(rows, dtype=F32), GRID_W)
    col = jnp.tile(jnp.arange(GRID_W, dtype=F32), rows)
    half = HEAD_DIM // 2
    freqs = 1.0 / (ROPE_THETA ** (jnp.arange(0, half, 2, dtype=F32) / half))
    ar, ac = row[:, None] * freqs, col[:, None] * freqs
    cosv = jnp.concatenate([jnp.cos(ar), jnp.cos(ar), jnp.cos(ac), jnp.cos(ac)], axis=-1)
    sinv = jnp.concatenate([-jnp.sin(ar), jnp.sin(ar), -jnp.sin(ac), jnp.sin(ac)], axis=-1)
    return jnp.tile(cosv, (1, 2)), jnp.tile(sinv, (1, 2))


def _position_dft(seq):
    n1 = seq // FFT_N2
    dft = lambda n: 2.0 * np.pi * ((np.arange(n)[:, None] * np.arange(n)[None, :]) % n) / n
    a1 = dft(n1)
    w1 = np.concatenate([np.cos(a1), np.sin(a1)], axis=0) * seq ** -0.5
    at = 2.0 * np.pi * (np.arange(FFT_N2)[:, None] * np.arange(n1)[None, :]) / seq
    lanes = lambda t: jnp.asarray(np.repeat(t[:, :, None], LANES, axis=2), dtype=F32)
    a2 = dft(FFT_N2)
    return (jnp.asarray(w1, dtype=BF16), lanes(np.cos(at)), lanes(np.sin(at)),
            jnp.asarray(np.concatenate([np.cos(a2), np.sin(a2)], axis=1), dtype=BF16))


def _channel_dft():
    n = np.arange(HEAD_DIM)
    ang = 2.0 * np.pi * ((n[:, None] * n[None, :]) % HEAD_DIM) / HEAD_DIM
    eye = np.eye(4)
    cb = np.kron(eye, np.cos(ang)) * HEAD_DIM ** -0.5
    sb = np.kron(eye, np.sin(ang)) * HEAD_DIM ** -0.5
    return jnp.asarray(np.concatenate([cb, sb], axis=1), dtype=BF16)


def _layer_params(l, w_in, q_norm_g, k_norm_g, w_fnet, w_pool, pool_scale, conv_w, conv_b, dt_bias,
                  a_log, d_skip, ssd_norm_g, w_out, ln1_g, ln1_b, w_group, b_group, w_router, b_router,
                  w_gate, w_up, w_down, ln2_g, ln2_b):
    wi = w_in[l]
    hperm = np.array([0, 2, 1, 3])
    wq = wi[:, 0:256].reshape(D_MODEL, 4, HEAD_DIM)[:, hperm].reshape(D_MODEL, 256)
    twice = lambda v: jnp.concatenate([v, v], axis=-1)
    w_proj = jnp.concatenate([wq, wi[:, 256:1792], jnp.pad(twice(wi[:, 1792:1800]), ((0, 0), (0, 112)))],
                             axis=1).astype(BF16)
    wo = w_out[l]
    wo_att = wo[0:256].reshape(4, HEAD_DIM, D_MODEL)[hperm].reshape(256, D_MODEL)
    w_o = jnp.concatenate([wo_att, wo[256:]], axis=0).astype(BF16)
    row = lambda v, n: jnp.pad(v.reshape(1, -1), ((0, 0), (0, n - v.size)))
    w_pool_bd = jnp.zeros((256, 256), F32)
    for gi in range(4):
        w_pool_bd = w_pool_bd.at[gi * 64:(gi + 1) * 64, gi * 64:(gi + 1) * 64].set(w_pool[l, gi])
    wr = jnp.pad(jnp.concatenate([w_group[l], w_router[l]], axis=1), ((0, 0), (0, LANES - 20)))
    wr_hi = wr.astype(BF16)
    att_bound = (1.01 * HEAD_DIM ** -0.5 * math.log2(math.e) * HEAD_DIM
                 * jnp.max(jnp.abs(q_norm_g[l])) * jnp.max(jnp.abs(k_norm_g[l]))).reshape(1).astype(F32)
    return dict(
        w_proj=w_proj, att_bound=att_bound,
        qg=jnp.tile(q_norm_g[l], 2).reshape(1, LANES), kg=jnp.tile(k_norm_g[l], 2).reshape(1, LANES),
        dtb=row(twice(dt_bias[l].reshape(-1)), LANES), w_fnet=w_fnet[l].astype(BF16),
        w_pool=w_pool_bd.astype(BF16), pool_scale=pool_scale[l].reshape(1, 256),
        conv_w=conv_w[l], conv_b=conv_b[l].reshape(1, 512),
        alog=row(twice(a_log[l].reshape(-1)), LANES), dskip=jnp.repeat(d_skip[l], 64).reshape(1, 256),
        ssd_ng=ssd_norm_g[l].reshape(1, 256), w_o=w_o,
        ln1_g=ln1_g[l].reshape(1, D_MODEL), ln1_b=ln1_b[l].reshape(1, D_MODEL),
        wr=jnp.concatenate([wr_hi, (wr - wr_hi.astype(F32)).astype(BF16)], axis=1),
        br=row(jnp.concatenate([b_group[l], b_router[l]]), LANES).reshape(LANES, 1),
        wg=w_gate[l].astype(BF16), wu=w_up[l].astype(BF16),
        wd=w_down[l].reshape(N_GROUPS, EXPERTS_PER_GROUP * EXPERT_FF, D_MODEL).astype(BF16),
        ln2_g=ln2_g[l].reshape(1, D_MODEL), ln2_b=ln2_b[l].reshape(1, D_MODEL),
    )


def _trunk(x3, params, tables):
    b, s, _ = x3.shape
    x = x3.reshape(b * s, D_MODEL)
    cosv, sinv, pos_dft, bd, dftc = tables
    for p in params:
        q, k, v, ab, up, z, xbc, dt = _in_proj(x, p["w_proj"], p["qg"], p["kg"], cosv, sinv, bd, dftc,
                                               p["dtb"], s)
        att = _attention(p["att_bound"], q, k, v, b, s)
        fmix = _fourier(pos_dft, ab, b, s)
        pol = _pool(up, p["w_pool"], p["pool_scale"], b, s)
        xc = _conv(xbc, p["conv_w"], p["conv_b"], b, s)
        yf, yb = _ssd(xc, dt, p["alog"], p["dskip"], b, s)
        x = _out_proj(x, att, fmix, pol, yf, yb, z, p["w_fnet"], p["w_o"], p["ssd_ng"], p["ln1_g"],
                      p["ln1_b"])
        x = _moe(x, p["wr"], p["br"], p["wg"], p["wu"], p["wd"], p["ln2_g"], p["ln2_b"])
    return x.reshape(b, s, D_MODEL)


def _tables(seq):
    cosv, sinv = _rope_tables(seq)
    lane = np.arange(LANES)
    bd = jnp.asarray(np.tile((lane[:, None] // HEAD_DIM) == (lane[None, :] // HEAD_DIM), (2, 1)), dtype=BF16)
    return cosv, sinv, _position_dft(seq), bd, _channel_dft()


def kernel(x_prompt, x_sample, w_in, q_norm_g, k_norm_g, w_fnet, w_pool, pool_scale, conv_w, conv_b, dt_bias, a_log, d_skip, ssd_norm_g, w_out, ln1_g, ln1_b, w_group, b_group, w_router, b_router, w_gate, w_up, w_down, ln2_g, ln2_b):
    weights = (w_in, q_norm_g, k_norm_g, w_fnet, w_pool, pool_scale, conv_w, conv_b, dt_bias, a_log,
               d_skip, ssd_norm_g, w_out, ln1_g, ln1_b, w_group, b_group, w_router, b_router,
               w_gate, w_up, w_down, ln2_g, ln2_b)
    params = [_layer_params(l, *weights) for l in range(w_in.shape[0])]
    y_prompt = _trunk(x_prompt, params, _tables(x_prompt.shape[1]))
    y_sample = _trunk(x_sample, params, _tables(x_sample.shape[1]))
    return (y_prompt, y_sample)
```

```python
import functools
import math

import numpy as np
import jax
import jax.numpy as jnp
from jax import lax
from jax.experimental import pallas as pl
from jax.experimental.pallas import tpu as pltpu

F32 = jnp.float32
BF16 = jnp.bfloat16

D_MODEL = 1024
DEPTH = 2
GRID_W = 64
HEAD_DIM = 64
ROPE_THETA = 10000.0
ATT_Q_HEADS = 4
ATT_KV_HEADS = 2
POOL_WINDOWS = (2, 4, 8, 16)
SSD_HEADS = 4
SSD_CONV = 4
SSD_CHUNK = 128
N_GROUPS = 4
EXPERTS_PER_GROUP = 4
N_EXPERTS = 16
EXPERT_FF = 256
DEEPNORM_ALPHA = (2 * DEPTH) ** 0.25
LN_EPS = 1e-5
RMS_EPS = 1e-6

LANES = 128
SUBLANES = 8
HALO = SUBLANES
VMEM_LIMIT = 56 * 1024 * 1024

C_Q, C_K, C_V, C_UF, C_UP, C_Z, C_XBC, C_DT, N_PROJ = 0, 256, 384, 512, 768, 1024, 1280, 1792, 1920
HI = lax.Precision.HIGHEST


def _cparams(sem):
    return pltpu.CompilerParams(dimension_semantics=sem, vmem_limit_bytes=VMEM_LIMIT)


def _dot(a, b):
    return jnp.dot(a, b, preferred_element_type=F32)


def _dot_nt(a, b):
    return lax.dot_general(a, b, (((1,), (1,)), ((), ())), preferred_element_type=F32)


def _silu(x):
    return x * (1.0 / (1.0 + jnp.exp(-x)))


def _layer_norm(x, g, b):
    mu = jnp.mean(x, axis=-1, keepdims=True)
    xc = x - mu
    var = jnp.mean(xc * xc, axis=-1, keepdims=True)
    return xc * lax.rsqrt(var + LN_EPS) * g + b


def _head_sumsq(x, ones_bd):
    sq = x * x
    hi = sq.astype(BF16)
    lo = (sq - hi.astype(F32)).astype(BF16)
    return _dot(jnp.concatenate([hi, lo], axis=1), ones_bd)


def _rope(x, cosv, sinv, first_half):
    w = x.shape[-1]
    partner = jnp.where(first_half, pltpu.roll(x, w - 16, axis=1), pltpu.roll(x, 16, axis=1))
    return x * cosv + partner * sinv


IN_PROJ_SUB = 512


def _in_proj_kernel(x_ref, w_ref, qg_ref, kg_ref, cos_ref, sin_ref, bd_ref, dftc_ref, dtb_ref,
                    q_ref, k_ref, v_ref, ab_ref, up_ref, z_ref, xbc_ref, dt_ref):
    bd = bd_ref[...]
    lane = lax.broadcasted_iota(jnp.int32, (1, LANES), 1)
    first_half = (lane & 31) < 16
    sub = min(IN_PROJ_SUB, x_ref.shape[0])
    nsub = x_ref.shape[0] // sub

    def project(i):
        return _dot(x_ref[i * sub:(i + 1) * sub, :].astype(BF16), w_ref[...])

    h_next = project(0)
    for i in range(nsub):
        h = h_next
        if i + 1 < nsub:
            h_next = project(i + 1)
        rows = slice(i * sub, (i + 1) * sub)
        cosv = cos_ref[rows, :]
        sinv = sin_ref[rows, :]
        for c in range(2):
            qc = h[:, C_Q + c * LANES:C_Q + (c + 1) * LANES]
            ss = _head_sumsq(qc, bd)
            qn = qc * lax.rsqrt(ss * (1.0 / HEAD_DIM) + RMS_EPS) * qg_ref[...]
            qr = _rope(qn, cosv, sinv, first_half) * (HEAD_DIM ** -0.5 * math.log2(math.e))
            q_ref[rows, c * LANES:(c + 1) * LANES] = qr.astype(BF16)
        kc = h[:, C_K:C_K + LANES]
        ss = _head_sumsq(kc, bd)
        kn = kc * lax.rsqrt(ss * (1.0 / HEAD_DIM) + RMS_EPS) * kg_ref[...]
        k_ref[rows, :] = _rope(kn, cosv, sinv, first_half).astype(BF16)
        v_ref[rows, :] = h[:, C_V:C_V + LANES].astype(BF16)
        uf = h[:, C_UF:C_UF + 256].astype(BF16)
        ab_ref[rows, :] = _dot(uf, dftc_ref[...])
        up_ref[rows, :] = h[:, C_UP:C_UP + 256]
        z_ref[rows, :] = h[:, C_Z:C_Z + 256]
        xbc_ref[rows, :] = h[:, C_XBC:C_XBC + 512]
        dr = h[:, C_DT:C_DT + LANES] + dtb_ref[...]
        dt_ref[rows, :] = jnp.maximum(dr, 0.0) + jnp.log(1.0 + jnp.exp(-jnp.abs(dr)))


def _in_proj(x, w, qg, kg, cosv, sinv, bd, dftc, dtb, seq):
    t = x.shape[0]
    tm = min(1024, seq)
    nseq = seq // tm
    full = lambda a: pl.BlockSpec(a.shape, lambda i: (0,) * a.ndim)
    row = lambda wdt: pl.BlockSpec((tm, wdt), lambda i: (i, 0))
    pos = pl.BlockSpec((tm, LANES), lambda i: (i % nseq, 0))
    outs = [(256, BF16), (128, BF16), (128, BF16), (512, F32), (256, F32), (256, F32), (512, F32),
            (128, F32)]
    return pl.pallas_call(
        _in_proj_kernel,
        grid=(t // tm,),
        in_specs=[row(D_MODEL), full(w), full(qg), full(kg), pos, pos, full(bd), full(dftc), full(dtb)],
        out_specs=[row(wd) for wd, _ in outs],
        out_shape=[jax.ShapeDtypeStruct((t, wd), dt) for wd, dt in outs],
        compiler_params=_cparams(("parallel",)),
        name="in_proj",
    )(x, w, qg, kg, cosv, sinv, bd, dftc, dtb)


ATT_KEY_CHUNK = 256
ATT_FIXED_SHIFT_MAX = 60.0


def _attn_kernel(bound_ref, q_ref, k_ref, v_ref, o_ref, vt_ref):
    nch, _, vrows, tk = vt_ref.shape
    tq = q_ref.shape[0]
    lane = lax.broadcasted_iota(jnp.int32, (1, LANES), 1)
    head_lanes = [(lane >> 6) == j for j in range(ATT_KV_HEADS)]

    @pl.when(pl.program_id(1) == 0)
    def _():
        for c in range(nch):
            vt = v_ref[c * tk:(c + 1) * tk, :].astype(F32).T.astype(BF16)
            for j in range(ATT_KV_HEADS):
                vt_ref[c, j, 0:HEAD_DIM, :] = vt[j * HEAD_DIM:(j + 1) * HEAD_DIM, :]
                vt_ref[c, j, HEAD_DIM:vrows, :] = jnp.ones((vrows - HEAD_DIM, tk), BF16)

    qms = []
    for c in range(2):
        qc = q_ref[:, c * LANES:(c + 1) * LANES]
        for j in range(ATT_KV_HEADS):
            qms.append(jnp.where(head_lanes[j], qc, jnp.zeros_like(qc)))

    bound = bound_ref[0]
    fixed_shift_ok = bound <= ATT_FIXED_SHIFT_MAX

    def scores(c):
        kc = k_ref[c * tk:(c + 1) * tk, :]
        return [_dot_nt(kc, qm) for qm in qms]

    def run(shift):
        ms = [jnp.full((1, tq), -jnp.inf, F32)] * ATT_Q_HEADS
        accs = [jnp.zeros((vrows, tq), F32)] * ATT_Q_HEADS
        st_next = scores(0)
        for c in range(nch):
            sts = st_next
            if c + 1 < nch:
                st_next = scores(c + 1)
            for h in range(ATT_Q_HEADS):
                if shift is None:
                    m_new = jnp.maximum(ms[h], jnp.max(sts[h], axis=0, keepdims=True))
                    accs[h] = jnp.exp2(ms[h] - m_new) * accs[h]
                    ms[h] = m_new
                    p = jnp.exp2(sts[h] - m_new).astype(BF16)
                else:
                    p = jnp.exp2(sts[h] - shift).astype(BF16)
                accs[h] = accs[h] + _dot(vt_ref[c, h % ATT_KV_HEADS], p)
        for c in range(2):
            ot = jnp.concatenate([accs[2 * c + j][0:HEAD_DIM] / accs[2 * c + j][HEAD_DIM:HEAD_DIM + 1]
                                  for j in range(ATT_KV_HEADS)], axis=0)
            o_ref[:, c * LANES:(c + 1) * LANES] = ot.T.astype(BF16)

    @pl.when(fixed_shift_ok)
    def _():
        run(bound)

    @pl.when(jnp.logical_not(fixed_shift_ok))
    def _():
        run(None)


def _attention(bound, q, k, v, b, s):
    tq = min(512, s)
    tk = min(ATT_KEY_CHUNK, s)
    q3, k3, v3 = q.reshape(b, s, 256), k.reshape(b, s, LANES), v.reshape(b, s, LANES)
    out = pl.pallas_call(
        _attn_kernel,
        grid=(b, s // tq),
        in_specs=[pl.BlockSpec(memory_space=pltpu.SMEM),
                  pl.BlockSpec((None, tq, 256), lambda bi, i: (bi, i, 0)),
                  pl.BlockSpec((None, s, LANES), lambda bi, i: (bi, 0, 0)),
                  pl.BlockSpec((None, s, LANES), lambda bi, i: (bi, 0, 0))],
        out_specs=pl.BlockSpec((None, tq, 256), lambda bi, i: (bi, i, 0)),
        out_shape=jax.ShapeDtypeStruct((b, s, 256), BF16),
        scratch_shapes=[pltpu.VMEM((s // tk, ATT_KV_HEADS, HEAD_DIM + 16, tk), BF16)],
        compiler_params=_cparams(("parallel", "arbitrary")),
        name="attention",
    )(bound, q3, k3, v3)
    return out.reshape(b * s, 256)


FFT_N2 = 64


def _fourier_kernel(a_ref, b_ref, w1_ref, tc_ref, ts_ref, cs2_ref, o_ref, yr_ref, yi_ref):
    n1 = a_ref.shape[0]
    w1 = w1_ref[...]
    for j in range(FFT_N2 // SUBLANES):
        rows = slice(j * SUBLANES, (j + 1) * SUBLANES)
        at = jnp.swapaxes(a_ref[:, rows, :], 0, 1)
        bt = jnp.swapaxes(b_ref[:, rows, :], 0, 1)
        yrs, yis = [], []
        for i in range(SUBLANES):
            n2 = j * SUBLANES + i
            r = _dot(w1, jnp.concatenate([at[i], bt[i]], axis=1).astype(BF16))
            yr = r[0:n1, 0:LANES] - r[n1:2 * n1, LANES:2 * LANES]
            ym = r[0:n1, LANES:2 * LANES] + r[n1:2 * n1, 0:LANES]
            tc = tc_ref[n2]
            ts = ts_ref[n2]
            yrs.append(yr * tc - ym * ts)
            yis.append(-(ym * tc) - yr * ts)
        yr_ref[:, rows, :] = jnp.swapaxes(jnp.stack(yrs, axis=0), 0, 1)
        yi_ref[:, rows, :] = jnp.swapaxes(jnp.stack(yis, axis=0), 0, 1)
    cs2 = cs2_ref[...]
    for k0 in range(0, n1, SUBLANES):
        ks = range(k0, k0 + SUBLANES)
        yr = jnp.concatenate([yr_ref[k1] for k1 in ks], axis=1)
        yi = jnp.concatenate([yi_ref[k1] for k1 in ks], axis=1)
        z = _dot(cs2, jnp.concatenate([yr, yi], axis=0).astype(BF16))
        zt = jnp.stack([z[:, g * LANES:(g + 1) * LANES] for g in range(SUBLANES)], axis=0)
        o_ref[:, k0:k0 + SUBLANES, :] = jnp.swapaxes(zt, 0, 1)


def _fourier(tabs, ab, b, s):
    w1, tc, ts, cs2 = tabs
    n1 = s // FFT_N2
    full = lambda a: pl.BlockSpec(a.shape, lambda bi, h: (0,) * a.ndim)
    ab4 = ab.reshape(b, n1, FFT_N2, 512)
    out = pl.pallas_call(
        _fourier_kernel,
        grid=(b, 2),
        in_specs=[pl.BlockSpec((None, n1, FFT_N2, LANES), lambda bi, h: (bi, 0, 0, h)),
                  pl.BlockSpec((None, n1, FFT_N2, LANES), lambda bi, h: (bi, 0, 0, 2 + h)),
                  full(w1), full(tc), full(ts), full(cs2)],
        out_specs=pl.BlockSpec((None, FFT_N2, n1, LANES), lambda bi, h: (bi, 0, 0, h)),
        out_shape=jax.ShapeDtypeStruct((b, FFT_N2, n1, 256), F32),
        scratch_shapes=[pltpu.VMEM((n1, FFT_N2, LANES), F32)] * 2,
        compiler_params=_cparams(("parallel", "parallel")),
        name="fourier",
    )(ab4, ab4, w1, tc, ts, cs2)
    return out.reshape(b * s, 256)


def _halo_specs(tt, width, s):
    nb = tt // HALO
    last = s // HALO - 1
    main = pl.BlockSpec((None, tt, width), lambda bi, i: (bi, i, 0))
    prev = pl.BlockSpec((None, HALO, width), lambda bi, i: (bi, jnp.maximum(i * nb - 1, 0), 0))
    nxt = pl.BlockSpec((None, HALO, width), lambda bi, i: (bi, jnp.minimum((i + 1) * nb, last), 0))
    return [main, prev, nxt]


def _with_halo(main_ref, prev_ref, next_ref):
    i = pl.program_id(1)
    n = pl.num_programs(1)
    prev = jnp.where(i > 0, prev_ref[...], 0.0)
    nxt = jnp.where(i < n - 1, next_ref[...], 0.0)
    return jnp.concatenate([prev, main_ref[...], nxt], axis=0)


def _shifted(ext, d, tt):
    n = ext.shape[0]
    r = ext if d == 0 else pltpu.roll(ext, (-d) % n, axis=0)
    return r[HALO:HALO + tt]


def _pool_kernel(u_ref, up_ref, un_ref, w_ref, sc_ref, o_ref, *, seq):
    tt = u_ref.shape[0]
    ext = _with_halo(u_ref, up_ref, un_ref)
    n = ext.shape[0]
    sh = lambda a, d: a if d == 0 else pltpu.roll(a, (-d) % n, axis=0)
    t = pl.program_id(1) * tt + lax.broadcasted_iota(jnp.int32, (tt, 1), 0)

    def inv_count(w):
        lo = jnp.clip(t - w // 2, 0, seq - 1)
        hi = jnp.clip(t + (w - w // 2) - 1, 0, seq - 1)
        return 1.0 / (hi - lo + 1).astype(F32)

    left = (lax.broadcasted_iota(jnp.int32, (1, LANES), 1) >> 6) == 0
    cols = []
    for c in range(2):
        e = ext[:, c * LANES:(c + 1) * LANES]
        u = u_ref[:, c * LANES:(c + 1) * LANES]
        p2 = sh(e, -1) + e
        p4 = sh(p2, -1) + sh(p2, 1)
        if c == 0:
            small, big, ws, wb = p2, p4, POOL_WINDOWS[0], POOL_WINDOWS[1]
        else:
            p8 = sh(p4, -2) + sh(p4, 2)
            p16 = sh(p8, -4) + sh(p8, 4)
            small, big, ws, wb = p8, p16, POOL_WINDOWS[2], POOL_WINDOWS[3]
        mean = jnp.where(left, small[HALO:HALO + tt] * inv_count(ws), big[HALO:HALO + tt] * inv_count(wb))
        cols.append((mean - u).astype(BF16))
    o_ref[...] = (_dot(jnp.concatenate(cols, axis=1), w_ref[...]) * sc_ref[...]).astype(BF16)


def _pool(u, w_bd, scale, b, s):
    tt = min(1024, s)
    u3 = u.reshape(b, s, 256)
    full = lambda a: pl.BlockSpec(a.shape, lambda bi, i: (0,) * a.ndim)
    out = pl.pallas_call(
        functools.partial(_pool_kernel, seq=s),
        grid=(b, s // tt),
        in_specs=_halo_specs(tt, 256, s) + [full(w_bd), full(scale)],
        out_specs=pl.BlockSpec((None, tt, 256), lambda bi, i: (bi, i, 0)),
        out_shape=jax.ShapeDtypeStruct((b, s, 256), BF16),
        compiler_params=_cparams(("parallel", "parallel")),
        name="pool",
    )(u3, u3, u3, w_bd, scale)
    return out.reshape(b * s, 256)


def _conv_kernel(x_ref, xp_ref, xn_ref, w_ref, b_ref, o_ref):
    tt = x_ref.shape[0]
    ext = _with_halo(x_ref, xp_ref, xn_ref)
    acc = b_ref[...] + _shifted(ext, -2, tt) * w_ref[0:1, :]
    for kk in range(1, SSD_CONV):
        acc = acc + _shifted(ext, kk - 2, tt) * w_ref[kk:kk + 1, :]
    o_ref[...] = _silu(acc)


def _conv(xbc, conv_w, conv_b, b, s):
    tt = min(1024, s)
    x3 = xbc.reshape(b, s, 512)
    full = lambda a: pl.BlockSpec(a.shape, lambda bi, i: (0,) * a.ndim)
    return pl.pallas_call(
        _conv_kernel,
        grid=(b, s // tt),
        in_specs=_halo_specs(tt, 512, s) + [full(conv_w), full(conv_b)],
        out_specs=pl.BlockSpec((None, tt, 512), lambda bi, i: (bi, i, 0)),
        out_shape=jax.ShapeDtypeStruct((b, s, 512), F32),
        compiler_params=_cparams(("parallel", "parallel")),
        name="ssd_conv",
    )(x3, x3, x3, conv_w, conv_b)


def _ssd_kernel(xf_ref, dtf_ref, xb_ref, dtb_ref, alog_ref, dskip_ref, yf_ref, yb_ref, carry_ref):
    @pl.when(pl.program_id(1) == 0)
    def _():
        carry_ref[...] = jnp.zeros_like(carry_ref)

    L = SSD_CHUNK
    nch = xf_ref.shape[0] // L
    a_row = -jnp.exp(alog_ref[...])
    dskip = dskip_ref[...]
    r = lax.broadcasted_iota(jnp.int32, (L, L), 0)
    c = lax.broadcasted_iota(jnp.int32, (L, L), 1)
    keeps = (c <= r, c >= r)
    tris = [k.astype(F32) for k in keeps]
    grp = lax.broadcasted_iota(jnp.int32, (1, LANES), 1) >> 6
    head = lax.broadcasted_iota(jnp.int32, (1, 256), 1) >> 6
    row_grp = lax.broadcasted_iota(jnp.int32, (LANES, 1), 0) >> 6

    units = []
    for ci in range(nch):
        units.append(dict(d=0, rows=slice(ci * L, (ci + 1) * L)))
        units.append(dict(d=1, rows=slice((nch - 1 - ci) * L, (nch - ci) * L)))

    for u in units:
        d = u["d"]
        u["off"] = d * SSD_HEADS
        u["xc"] = (xf_ref, xb_ref)[d][u["rows"], :]
        u["dtc"] = (dtf_ref, dtb_ref)[d][u["rows"], :]
        u["cs"] = jnp.dot(tris[d], u["dtc"] * a_row, precision=HI, preferred_element_type=F32)

    prow = lax.broadcasted_iota(jnp.int32, (LANES, 1), 0)
    spread = [jnp.where((prow < 16) & ((prow & 7) == d * SSD_HEADS + head), 1.0, 0.0).astype(BF16)
              for d in range(2)]
    low8 = lax.broadcasted_iota(jnp.int32, (1, LANES), 1) < 8

    for u in units:
        cs, xc = u["cs"], u["xc"]
        u["cs_t"] = cs.T
        tot = cs[0:1, :] if u["d"] else cs[L - 1:L, :]
        fac = jnp.concatenate([u["dtc"], jnp.exp(tot - cs), jnp.exp(cs)], axis=0)
        head_part = fac.astype(BF16).astype(F32)
        fac_x = _dot(jnp.where(low8, head_part, fac - head_part).astype(BF16), spread[u["d"]])
        u["ds_x"], u["ecs_x"] = fac_x[L:2 * L], fac_x[2 * L:3 * L]
        u["etot_x"] = u["ecs_x"][0:1, :] if u["d"] else u["ecs_x"][L - 1:L, :]
        u["xd"] = xc[:, 0:256] * fac_x[0:L]
        u["xd_b"] = u["xd"].astype(BF16)
        u["b_t"] = xc[:, 256:384].T.astype(BF16)
        cmat = xc[:, 384:512]
        u["g"] = [_dot(jnp.where(grp == g, cmat, 0.0).astype(BF16), u["b_t"]) for g in range(2)]

    for u in units:
        cs, cs_t, off, keep = u["cs"], u["cs_t"], u["off"], keeps[u["d"]]
        y = None
        for h in range(SSD_HEADS):
            diff = cs[:, off + h:off + h + 1] - cs_t[off + h:off + h + 1, :]
            decay = jnp.where(keep, jnp.exp(jnp.where(keep, diff, 0.0)), 0.0)
            sc = (u["g"][h // 2] * decay).astype(BF16)
            yd = _dot(sc, u["xd_b"])
            y = yd if y is None else jnp.where(head == h, yd, y)
        u["y"] = y

    for u in units:
        st = _dot(u["b_t"], (u["xd"] * u["ds_x"]).astype(BF16))
        u["st"] = jnp.where(row_grp == (head >> 1), st, 0.0)

    carries = [carry_ref[0], carry_ref[1]]
    for u in units:
        d, xc = u["d"], u["xc"]
        y_off = _dot(xc[:, 384:512].astype(BF16), carries[d].astype(BF16))
        y = u["y"] + y_off * u["ecs_x"]
        carries[d] = carries[d] * u["etot_x"] + u["st"]
        if d == 0:
            yf_ref[u["rows"], :] = y + xc[:, 0:256] * dskip
        else:
            yb_ref[u["rows"], :] = y
    carry_ref[0] = carries[0]
    carry_ref[1] = carries[1]


def _ssd(xc, dt, alog_row, dskip_row, b, s):
    tt = min(1024, s)
    nt = s // tt
    dt3 = dt.reshape(b, s, LANES)
    fwd = lambda w: pl.BlockSpec((None, tt, w), lambda bi, i: (bi, i, 0))
    bwd = lambda w: pl.BlockSpec((None, tt, w), lambda bi, i: (bi, nt - 1 - i, 0))
    full = lambda a: pl.BlockSpec(a.shape, lambda bi, i: (0,) * a.ndim)
    yf, yb = pl.pallas_call(
        _ssd_kernel,
        grid=(b, nt),
        in_specs=[fwd(512), fwd(LANES), bwd(512), bwd(LANES), full(alog_row), full(dskip_row)],
        out_specs=[fwd(256), bwd(256)],
        out_shape=[jax.ShapeDtypeStruct((b, s, 256), F32)] * 2,
        scratch_shapes=[pltpu.VMEM((2, LANES, 256), F32)],
        compiler_params=_cparams(("parallel", "arbitrary")),
        name="ssd_scan",
    )(xc, dt3, xc, dt3, alog_row, dskip_row)
    return yf.reshape(b * s, 256), yb.reshape(b * s, 256)


OUT_PROJ_SUB = 256


def _out_proj_kernel(x_ref, att_ref, fmix_ref, pol_ref, yf_ref, yb_ref, z_ref, wf_ref, w_ref, ng_ref,
                     g_ref, b_ref, o_ref):
    ng = ng_ref[...]
    sub = min(OUT_PROJ_SUB, x_ref.shape[0])

    def mixed(i):
        rows = slice(i * sub, (i + 1) * sub)
        fno = _dot(fmix_ref[rows, :].astype(BF16), wf_ref[...]).astype(BF16)
        y = (yf_ref[rows, :] + yb_ref[rows, :]) * _silu(z_ref[rows, :])
        parts = []
        for gi in range(2):
            yg = y[:, gi * LANES:(gi + 1) * LANES]
            ms = jnp.mean(yg * yg, axis=-1, keepdims=True)
            parts.append((yg * lax.rsqrt(ms + RMS_EPS) * ng[:, gi * LANES:(gi + 1) * LANES]).astype(BF16))
        return _dot(jnp.concatenate([att_ref[rows, :], fno, pol_ref[rows, :]] + parts, axis=1), w_ref[...])

    nsub = x_ref.shape[0] // sub
    mix_next = mixed(0)
    for i in range(nsub):
        mix = mix_next
        if i + 1 < nsub:
            mix_next = mixed(i + 1)
        rows = slice(i * sub, (i + 1) * sub)
        o_ref[rows, :] = _layer_norm(DEEPNORM_ALPHA * x_ref[rows, :] + mix, g_ref[...], b_ref[...])


def _out_proj(x, att, fmix, pol, yf, yb, z, wf, w, ng, g, bb):
    t = x.shape[0]
    tm = min(1024, t)
    row = lambda wd: pl.BlockSpec((tm, wd), lambda i: (i, 0))
    full = lambda a: pl.BlockSpec(a.shape, lambda i: (0,) * a.ndim)
    return pl.pallas_call(
        _out_proj_kernel,
        grid=(t // tm,),
        in_specs=[row(D_MODEL)] + [row(256)] * 6 + [full(wf), full(w), full(ng), full(g), full(bb)],
        out_specs=row(D_MODEL),
        out_shape=jax.ShapeDtypeStruct((t, D_MODEL), F32),
        compiler_params=_cparams(("parallel",)),
        name="out_proj_ln",
    )(x, att, fmix, pol, yf, yb, z, wf, w, ng, g, bb)


MOE_TILE = 1024
MOE_CHUNK = 128
MOE_PERM_ROWS = 256


MOE_LOGIT_ROWS = 24


def _router(x, w2, br_col):
    xh = x.astype(BF16)
    xl = (x - xh.astype(F32)).astype(BF16)
    l1 = _dot(xh, w2)
    logits = l1[:, 0:LANES] + l1[:, LANES:2 * LANES] + _dot(xl, w2[:, 0:LANES])
    lt = logits.T[0:MOE_LOGIT_ROWS, :] + br_col[0:MOE_LOGIT_ROWS, :]
    row = lax.broadcasted_iota(jnp.int32, (MOE_LOGIT_ROWS, 1), 0)
    ninf = -jnp.inf
    row8 = row[0:SUBLANES]
    gl = jnp.where(row8 < N_GROUPS, lt[0:SUBLANES], ninf)
    gmax = jnp.max(gl, axis=0, keepdims=True)
    g_p = 1.0 / jnp.sum(jnp.exp(gl - gmax), axis=0, keepdims=True)
    g_idx = jnp.min(jnp.where(gl == gmax, row8, LANES), axis=0, keepdims=True)
    e_row = row - N_GROUPS
    in_grp = (e_row >= 0) & (e_row < N_EXPERTS) & ((e_row >> 2) == g_idx)
    el = jnp.where(in_grp, lt, ninf)
    m1 = jnp.max(el, axis=0, keepdims=True)
    i1 = jnp.min(jnp.where(el == m1, row, LANES), axis=0, keepdims=True)
    el2 = jnp.where(row == i1, ninf, el)
    m2 = jnp.max(el2, axis=0, keepdims=True)
    i2 = jnp.min(jnp.where(el2 == m2, row, LANES), axis=0, keepdims=True)
    e2 = jnp.exp(m2 - m1)
    w1 = 1.0 / (1.0 + e2)
    gates = g_p * (jnp.where(row == i1, w1, 0.0) + jnp.where(row == i2, e2 * w1, 0.0))
    return gates, g_idx


def _moe_kernel(x_ref, wr_ref, br_ref, tri_ref, wg_ref, wu_ref, wd_ref, g_ref, b_ref, o_ref,
                xs_ref, gs_ref, pos_ref, acc_ref, off_ref):
    grp = pl.program_id(1)
    n = x_ref.shape[0]
    lane = lax.broadcasted_iota(jnp.int32, (1, LANES), 1)

    @pl.when(grp == 0)
    def _route_and_sort():
        x = x_ref[...]
        gates_t, g_idx = _router(x, wr_ref[...], br_ref[...])
        row8 = lax.broadcasted_iota(jnp.int32, (SUBLANES, 1), 0)
        onehot = row8 == g_idx
        onehot_b = jnp.concatenate([jnp.where(onehot, 1.0, 0.0), jnp.zeros((SUBLANES, n), F32)],
                                   axis=0).astype(BF16)
        csum = _dot(onehot_b, tri_ref[...])[0:SUBLANES, :]
        cnt = csum[:, n - 1:n].astype(jnp.int32)
        c0, c1, c2 = cnt[0, 0], cnt[1, 0], cnt[2, 0]
        off_ref[0] = 0
        off_ref[1] = c0
        off_ref[2] = c0 + c1
        off_ref[3] = c0 + c1 + c2
        off_ref[4] = n
        offv = jnp.where(row8 == 1, c0, jnp.where(row8 == 2, c0 + c1, jnp.where(row8 == 3, c0 + c1 + c2, 0)))
        pos = jnp.sum(jnp.where(onehot, csum + offv.astype(F32), 0.0), axis=0, keepdims=True) - 1.0
        pos_ref[...] = jnp.broadcast_to(pos, (LANES, n)).T
        pos_row = pos.astype(jnp.int32)
        xb = x.astype(BF16)
        gates = jnp.concatenate([gates_t, jnp.zeros((LANES - MOE_LOGIT_ROWS, n), F32)], axis=0).T
        g_hi = gates.astype(BF16)
        g2 = jnp.concatenate([g_hi, (gates - g_hi.astype(F32)).astype(BF16)], axis=1)
        for r in range(n // MOE_PERM_ROWS):
            rows = lax.broadcasted_iota(jnp.int32, (MOE_PERM_ROWS, n), 0) + r * MOE_PERM_ROWS
            perm = jnp.where(rows == pos_row, 1.0, 0.0).astype(BF16)
            sl = slice(r * MOE_PERM_ROWS, (r + 1) * MOE_PERM_ROWS)
            xs_ref[sl, :] = _dot(perm, xb).astype(BF16)
            gg = _dot(perm, g2)
            gs_ref[sl, :] = gg[:, 0:LANES] + gg[:, LANES:2 * LANES]

    lo = off_ref[grp]
    hi = off_ref[grp + 1]

    def chunk(c, carry):
        r0 = pl.multiple_of(c * MOE_CHUNK, MOE_CHUNK)

        rs = pl.ds(r0, MOE_CHUNK)

        def experts():
            xs = xs_ref[rs, :]
            gsc = gs_ref[rs, :]
            parts = []
            for e in range(EXPERTS_PER_GROUP):
                hid = _silu(_dot(xs, wg_ref[e])) * _dot(xs, wu_ref[e])
                ge = jnp.sum(jnp.where(lane == N_GROUPS + EXPERTS_PER_GROUP * grp + e, gsc, 0.0),
                             axis=-1, keepdims=True)
                parts.append((hid * ge).astype(BF16))
            return _dot(jnp.concatenate(parts, axis=1), wd_ref[...])

        owns_first_row = (lo <= r0) & (hi > r0)

        @pl.when(owns_first_row)
        def _():
            acc_ref[rs, :] = experts()

        @pl.when((lo > r0) & (lo < r0 + MOE_CHUNK) & (hi > lo))
        def _():
            acc_ref[rs, :] += experts()

        return carry

    lax.fori_loop(0, n // MOE_CHUNK, chunk, 0)

    @pl.when(grp == N_GROUPS - 1)
    def _unsort_and_norm():
        xs_ref[...] = acc_ref[...].astype(BF16)
        cols = lax.broadcasted_iota(jnp.int32, (MOE_PERM_ROWS, n), 1)
        for r in range(n // MOE_PERM_ROWS):
            sl = slice(r * MOE_PERM_ROWS, (r + 1) * MOE_PERM_ROWS)
            perm_t = jnp.where(cols == pos_ref[sl, 0:1].astype(jnp.int32), 1.0, 0.0).astype(BF16)
            y = _dot(perm_t, xs_ref[...])
            o_ref[sl, :] = _layer_norm(DEEPNORM_ALPHA * x_ref[sl, :] + y, g_ref[...], b_ref[...])


def _moe(x, wr, br, wg, wu, wd, g, bb):
    t = x.shape[0]
    n = min(MOE_TILE, t)
    tri = jnp.asarray(np.triu(np.ones((n, n), np.float32)), dtype=BF16)
    full = lambda a: pl.BlockSpec(a.shape, lambda i, e: (0,) * a.ndim)
    wspec = pl.BlockSpec((EXPERTS_PER_GROUP, D_MODEL, EXPERT_FF), lambda i, e: (e, 0, 0))
    dspec = pl.BlockSpec((None, EXPERTS_PER_GROUP * EXPERT_FF, D_MODEL), lambda i, e: (e, 0, 0))
    return pl.pallas_call(
        _moe_kernel,
        grid=(t // n, N_GROUPS),
        in_specs=[pl.BlockSpec((n, D_MODEL), lambda i, e: (i, 0)), full(wr), full(br), full(tri),
                  wspec, wspec, dspec, full(g), full(bb)],
        out_specs=pl.BlockSpec((n, D_MODEL), lambda i, e: (i, 0)),
        out_shape=jax.ShapeDtypeStruct((t, D_MODEL), F32),
        scratch_shapes=[pltpu.VMEM((n, D_MODEL), BF16), pltpu.VMEM((n, LANES), F32),
                        pltpu.VMEM((n, LANES), F32), pltpu.VMEM((n, D_MODEL), F32),
                        pltpu.SMEM((8,), jnp.int32)],
        compiler_params=_cparams(("parallel", "arbitrary")),
        name="moe_ln",
    )(x, wr, br, tri, wg, wu, wd, g, bb)


def _rope_tables(seq):
    rows = seq // GRID_W
    row = jnp.repeat(jnp.arange(rows, dtype=F32), GRID_W)
    col = jnp.tile(jnp.arange(GRID_W, dtype=F32), rows)
    half = HEAD_DIM // 2
    freqs = 1.0 / (ROPE_THETA ** (jnp.arange(0, half, 2, dtype=F32) / half))
    ar, ac = row[:, None] * freqs, col[:, None] * freqs
    cosv = jnp.concatenate([jnp.cos(ar), jnp.cos(ar), jnp.cos(ac), jnp.cos(ac)], axis=-1)
    sinv = jnp.concatenate([-jnp.sin(ar), jnp.sin(ar), -jnp.sin(ac), jnp.sin(ac)], axis=-1)
    return jnp.tile(cosv, (1, 2)), jnp.tile(sinv, (1, 2))


def _position_dft(seq):
    n1 = seq // FFT_N2
    dft = lambda n: 2.0 * np.pi * ((np.arange(n)[:, None] * np.arange(n)[None, :]) % n) / n
    a1 = dft(n1)
    w1 = np.concatenate([np.cos(a1), np.sin(a1)], axis=0) * seq ** -0.5
    at = 2.0 * np.pi * (np.arange(FFT_N2)[:, None] * np.arange(n1)[None, :]) / seq
    lanes = lambda t: jnp.asarray(np.repeat(t[:, :, None], LANES, axis=2), dtype=F32)
    a2 = dft(FFT_N2)
    return (jnp.asarray(w1, dtype=BF16), lanes(np.cos(at)), lanes(np.sin(at)),
            jnp.asarray(np.concatenate([np.cos(a2), np.sin(a2)], axis=1), dtype=BF16))


def _channel_dft():
    n = np.arange(HEAD_DIM)
    ang = 2.0 * np.pi * ((n[:, None] * n[None, :]) % HEAD_DIM) / HEAD_DIM
    eye = np.eye(4)
    cb = np.kron(eye, np.cos(ang)) * HEAD_DIM ** -0.5
    sb = np.kron(eye, np.sin(ang)) * HEAD_DIM ** -0.5
    return jnp.asarray(np.concatenate([cb, sb], axis=1), dtype=BF16)


def _layer_params(l, w_in, q_norm_g, k_norm_g, w_fnet, w_pool, pool_scale, conv_w, conv_b, dt_bias,
                  a_log, d_skip, ssd_norm_g, w_out, ln1_g, ln1_b, w_group, b_group, w_router, b_router,
                  w_gate, w_up, w_down, ln2_g, ln2_b):
    wi = w_in[l]
    hperm = np.array([0, 2, 1, 3])
    wq = wi[:, 0:256].reshape(D_MODEL, 4, HEAD_DIM)[:, hperm].reshape(D_MODEL, 256)
    twice = lambda v: jnp.concatenate([v, v], axis=-1)
    w_proj = jnp.concatenate([wq, wi[:, 256:1792], jnp.pad(twice(wi[:, 1792:1800]), ((0, 0), (0, 112)))],
                             axis=1).astype(BF16)
    wo = w_out[l]
    wo_att = wo[0:256].reshape(4, HEAD_DIM, D_MODEL)[hperm].reshape(256, D_MODEL)
    w_o = jnp.concatenate([wo_att, wo[256:]], axis=0).astype(BF16)
    row = lambda v, n: jnp.pad(v.reshape(1, -1), ((0, 0), (0, n - v.size)))
    w_pool_bd = jnp.zeros((256, 256), F32)
    for gi in range(4):
        w_pool_bd = w_pool_bd.at[gi * 64:(gi + 1) * 64, gi * 64:(gi + 1) * 64].set(w_pool[l, gi])
    wr = jnp.pad(jnp.concatenate([w_group[l], w_router[l]], axis=1), ((0, 0), (0, LANES - 20)))
    wr_hi = wr.astype(BF16)
    att_bound = (1.01 * HEAD_DIM ** -0.5 * math.log2(math.e) * HEAD_DIM
                 * jnp.max(jnp.abs(q_norm_g[l])) * jnp.max(jnp.abs(k_norm_g[l]))).reshape(1).astype(F32)
    return dict(
        w_proj=w_proj, att_bound=att_bound,
        qg=jnp.tile(q_norm_g[l], 2).reshape(1, LANES), kg=jnp.tile(k_norm_g[l], 2).reshape(1, LANES),
        dtb=row(twice(dt_bias[l].reshape(-1)), LANES), w_fnet=w_fnet[l].astype(BF16),
        w_pool=w_pool_bd.astype(BF16), pool_scale=pool_scale[l].reshape(1, 256),
        conv_w=conv_w[l], conv_b=conv_b[l].reshape(1, 512),
        alog=row(twice(a_log[l].reshape(-1)), LANES), dskip=jnp.repeat(d_skip[l], 64).reshape(1, 256),
        ssd_ng=ssd_norm_g[l].reshape(1, 256), w_o=w_o,
        ln1_g=ln1_g[l].reshape(1, D_MODEL), ln1_b=ln1_b[l].reshape(1, D_MODEL),
        wr=jnp.concatenate([wr_hi, (wr - wr_hi.astype(F32)).astype(BF16)], axis=1),
        br=row(jnp.concatenate([b_group[l], b_router[l]]), LANES).reshape(LANES, 1),
        wg=w_gate[l].astype(BF16), wu=w_up[l].astype(BF16),
        wd=w_down[l].reshape(N_GROUPS, EXPERTS_PER_GROUP * EXPERT_FF, D_MODEL).astype(BF16),
        ln2_g=ln2_g[l].reshape(1, D_MODEL), ln2_b=ln2_b[l].reshape(1, D_MODEL),
    )


def _trunk(x3, params, tables):
    b, s, _ = x3.shape
    x = x3.reshape(b * s, D_MODEL)
    cosv, sinv, pos_dft, bd, dftc = tables
    for p in params:
        q, k, v, ab, up, z, xbc, dt = _in_proj(x, p["w_proj"], p["qg"], p["kg"], cosv, sinv, bd, dftc,
                                               p["dtb"], s)
        att = _attention(p["att_bound"], q, k, v, b, s)
        fmix = _fourier(pos_dft, ab, b, s)
        pol = _pool(up, p["w_pool"], p["pool_scale"], b, s)
        xc = _conv(xbc, p["conv_w"], p["conv_b"], b, s)
        yf, yb = _ssd(xc, dt, p["alog"], p["dskip"], b, s)
        x = _out_proj(x, att, fmix, pol, yf, yb, z, p["w_fnet"], p["w_o"], p["ssd_ng"], p["ln1_g"],
                      p["ln1_b"])
        x = _moe(x, p["wr"], p["br"], p["wg"], p["wu"], p["wd"], p["ln2_g"], p["ln2_b"])
    return x.reshape(b, s, D_MODEL)


def _tables(seq):
    cosv, sinv = _rope_tables(seq)
    lane = np.arange(LANES)
    bd = jnp.asarray(np.tile((lane[:, None] // HEAD_DIM) == (lane[None, :] // HEAD_DIM), (2, 1)), dtype=BF16)
    return cosv, sinv, _position_dft(seq), bd, _channel_dft()


def kernel(x_prompt, x_sample, w_in, q_norm_g, k_norm_g, w_fnet, w_pool, pool_scale, conv_w, conv_b, dt_bias, a_log, d_skip, ssd_norm_g, w_out, ln1_g, ln1_b, w_group, b_group, w_router, b_router, w_gate, w_up, w_down, ln2_g, ln2_b):
    weights = (w_in, q_norm_g, k_norm_g, w_fnet, w_pool, pool_scale, conv_w, conv_b, dt_bias, a_log,
               d_skip, ssd_norm_g, w_out, ln1_g, ln1_b, w_group, b_group, w_router, b_router,
               w_gate, w_up, w_down, ln2_g, ln2_b)
    params = [_layer_params(l, *weights) for l in range(w_in.shape[0])]
    y_prompt = _trunk(x_prompt, params, _tables(x_prompt.shape[1]))
    y_sample = _trunk(x_sample, params, _tables(x_sample.shape[1]))
    return (y_prompt, y_sample)
```
